```python
import math
import jax
import jax.numpy as jnp
from jax import lax
import numpy as np

D_MODEL = 1024
BATCH = 32
SEQ = 256
DEPTH = 2
DEC_BATCH = 4
DEC_SEQ = 1024
PAST_LEN = 512

GRID_W = 64
MIX_GROUP = D_MODEL // 4
NA_DH = 64
NA_HEADS = MIX_GROUP // NA_DH
NA_WIN_R = 8
NA_WIN_C = 16
NA_BAND = 2 * NA_WIN_C
S5_CH = MIX_GROUP
S5_GROUP = 16
S5_GROUPS = S5_CH // S5_GROUP
S5_N = 64
GQ_DH = 64
GQ_HEADS = MIX_GROUP // GQ_DH
GQ_KV = GQ_HEADS // 2
MLA_V = 64
MLA_HEADS = MIX_GROUP // MLA_V
MLA_NOPE = 64
MLA_ROPE = 32
MLA_QK = MLA_NOPE + MLA_ROPE
MLA_QLORA = (3 * D_MODEL) // 16
MLA_KVLORA = D_MODEL // 8
NA_IN = 3 * NA_HEADS * NA_DH
S5_IN = S5_CH
GQ_IN = (GQ_HEADS + 2 * GQ_KV) * GQ_DH
MLA_IN = MLA_QLORA + MLA_KVLORA + MLA_ROPE
D_IN = NA_IN + S5_IN + GQ_IN + MLA_IN
D_CAT = NA_HEADS * NA_DH + S5_CH + GQ_HEADS * GQ_DH + MLA_HEADS * MLA_V
D_FF = 128 * ((8 * D_MODEL // 3 + 127) // 128)
CONV_W = 3
Q_BLOCK = 128
ROPE_BASE = 10000.0
EPS = 1e-6
NEG = -1e30
DT_MIN = 1e-3
DT_MAX = 1e-1

kernel_name = 'hybrid_diffusion_prefix_trunk_step'


def rmsnorm(x, g):
    xf = x.astype(jnp.float32)
    y = xf * lax.rsqrt(jnp.mean(xf * xf, axis=-1, keepdims=True) + EPS)
    return (y * g.astype(jnp.float32)).astype(x.dtype)


def to_heads(x, n, dh):
    b, L, _ = x.shape
    return x.reshape(b, L, n, dh).transpose(0, 2, 1, 3)


def from_heads(x):
    b, h, L, dh = x.shape
    return x.transpose(0, 2, 1, 3).reshape(b, L, h * dh)


def rope_angles_1d(pos, dim):
    half = dim // 2
    inv = ROPE_BASE ** (-jnp.arange(half, dtype=jnp.float32) / half)
    ang = pos.astype(jnp.float32)[:, None] * inv[None, :]
    return jnp.concatenate([ang, ang], axis=-1)


def rope_2d_tables(t, dim):
    pos = jnp.arange(t)
    ang = jnp.concatenate([rope_angles_1d(pos // GRID_W, dim // 2),
                           rope_angles_1d(pos % GRID_W, dim // 2)], axis=-1)
    return jnp.cos(ang), jnp.sin(ang)


def rotate_half(x):
    x1, x2 = jnp.split(x, 2, axis=-1)
    return jnp.concatenate([-x2, x1], axis=-1)


def apply_rope_2d(x, cos, sin):
    h = x.shape[-1] // 2
    rot = jnp.concatenate([rotate_half(x[..., :h]), rotate_half(x[..., h:])], axis=-1)
    return (x * cos + rot * sin).astype(x.dtype)


def rope_tail(x, cos, sin, r):
    return jnp.concatenate([x[..., :-r], apply_rope_2d(x[..., -r:], cos, sin)], axis=-1)


def attend(q, k, v, scale):
    b, hq, t, dk = q.shape
    hk, dv = k.shape[1], v.shape[-1]
    rep = hq // hk
    qb = Q_BLOCK if t % Q_BLOCK == 0 else t
    qg = jnp.moveaxis(q.reshape(b, hk, rep, t // qb, qb, dk), 3, 0)

    def block(qi):
        s = jnp.einsum('bgrqd,bgkd->bgrqk', qi, k).astype(jnp.float32) * scale
        pr = jax.nn.softmax(s, axis=-1).astype(v.dtype)
        return jnp.einsum('bgrqk,bgkd->bgrqd', pr, v)

    o = lax.map(block, qg)
    return jnp.moveaxis(o, 0, 3).reshape(b, hq, t, dv)


def na_tables(rows, rpb):
    wr = min(NA_WIN_R, rows)
    ncb = GRID_W // NA_WIN_C
    r = np.arange(rows)
    row_idx = np.clip(r - wr // 2, 0, rows - wr)[:, None] + np.arange(wr)[None, :]
    band_start = np.clip(np.arange(ncb) * NA_WIN_C - NA_WIN_C // 2, 0, GRID_W - NA_BAND)
    col_idx = band_start[:, None] + np.arange(NA_BAND)[None, :]
    qcol = np.arange(GRID_W).reshape(ncb, NA_WIN_C)
    col_start = np.clip(qcol - NA_WIN_C // 2, 0, GRID_W - NA_WIN_C)
    kcol = col_idx[:, None, :]
    valid = (kcol >= col_start[..., None]) & (kcol < col_start[..., None] + NA_WIN_C)
    d_r = row_idx - r[:, None] + NA_WIN_R - 1
    d_c = np.clip(kcol - qcol[..., None], 1 - NA_WIN_C, NA_WIN_C - 1) + NA_WIN_C - 1
    bias = rpb[:, d_r[:, None, None, :, None], d_c[None, :, :, None, :]].astype(jnp.float32)
    bias = jnp.where(jnp.asarray(valid)[None, None, :, :, None, :], bias, NEG)
    return row_idx, col_idx, bias.reshape(rpb.shape[0], rows, ncb, NA_WIN_C, wr * NA_BAND)


def na_latent(q, k, v, k_ctx, v_ctx, rpb):
    b, h, t, dh = q.shape
    rows = t // GRID_W
    wr = min(NA_WIN_R, rows)
    ncb = GRID_W // NA_WIN_C
    row_idx, col_idx, bias = na_tables(rows, rpb)

    def gather(x):
        xg = jnp.take(x.reshape(b, h, rows, GRID_W, dh), jnp.asarray(row_idx), axis=2)
        xg = jnp.take(xg, jnp.asarray(col_idx), axis=4)
        return jnp.swapaxes(xg, 3, 4).reshape(b, h, rows, ncb, wr * NA_BAND, dh)

    kg, vg = gather(k), gather(v)
    qg = q.reshape(b, h, rows, ncb, NA_WIN_C, dh)
    scale = NA_DH ** -0.5
    s_loc = jnp.einsum('bhrnqd,bhrnkd->bhrnqk', qg, kg).astype(jnp.float32) * scale + bias[None]
    s_ctx = jnp.einsum('bhrnqd,bhpd->bhrnqp', qg, k_ctx).astype(jnp.float32) * scale
    pr = jax.nn.softmax(jnp.concatenate([s_loc, s_ctx], axis=-1), axis=-1).astype(v.dtype)
    nl = wr * NA_BAND
    o = (jnp.einsum('bhrnqk,bhrnkd->bhrnqd', pr[..., :nl], vg)
         + jnp.einsum('bhrnqp,bhpd->bhrnqd', pr[..., nl:], v_ctx))
    return o.reshape(b, h, t, dh)


def s5_discretize(lam_re, lam_im, log_dt, b_re, b_im):
    dt = jnp.exp(log_dt.astype(jnp.float32))[:, None]
    lr, li = lam_re.astype(jnp.float32), lam_im.astype(jnp.float32)
    mag = jnp.exp(lr * dt)
    a_re, a_im = mag * jnp.cos(li * dt), mag * jnp.sin(li * dt)
    den = lr * lr + li * li
    f_re = ((a_re - 1.0) * lr + a_im * li) / den
    f_im = (a_im * lr - (a_re - 1.0) * li) / den
    br, bi = b_re.astype(jnp.float32), b_im.astype(jnp.float32)
    bb_re = f_re[..., None] * br - f_im[..., None] * bi
    bb_im = f_re[..., None] * bi + f_im[..., None] * br
    return a_re, a_im, bb_re, bb_im


def complex_scan_op(e1, e2):
    a1r, a1i, b1r, b1i = e1
    a2r, a2i, b2r, b2i = e2
    return (a1r * a2r - a1i * a2i, a1r * a2i + a1i * a2r,
            a2r * b1r - a2i * b1i + b2r, a2r * b1i + a2i * b1r + b2i)


def s5_direction(u, a_re, a_im, bb_re, bb_im, c_re, c_im, h0_re, h0_im, reverse):
    bu_re = jnp.einsum('gnc,blgc->blgn', bb_re, u)
    bu_im = jnp.einsum('gnc,blgc->blgn', bb_im, u)
    first = -1 if reverse else 0
    bu_re = bu_re.at[:, first].add(a_re * h0_re - a_im * h0_im)
    bu_im = bu_im.at[:, first].add(a_re * h0_im + a_im * h0_re)
    ar = jnp.broadcast_to(a_re, bu_re.shape)
    ai = jnp.broadcast_to(a_im, bu_re.shape)
    _, _, h_re, h_im = lax.associative_scan(complex_scan_op, (ar, ai, bu_re, bu_im), axis=1, reverse=reverse)
    y = (jnp.einsum('gcn,blgn->blgc', c_re.astype(jnp.float32), h_re)
         - jnp.einsum('gcn,blgn->blgc', c_im.astype(jnp.float32), h_im))
    last = 0 if reverse else -1
    return y, h_re[:, last], h_im[:, last]


def s5_mixer(u_flat, p, h0):
    b, L, _ = u_flat.shape
    u = u_flat.astype(jnp.float32).reshape(b, L, S5_GROUPS, S5_GROUP)
    h0 = h0.astype(jnp.float32)
    y = p['s5_d'].astype(jnp.float32).reshape(S5_GROUPS, S5_GROUP) * u
    finals = []
    for d, rev in enumerate((False, True)):
        a_re, a_im, bb_re, bb_im = s5_discretize(p['s5_lam_re'][d], p['s5_lam_im'][d], p['s5_log_dt'][d],
                                                 p['s5_b_re'][d], p['s5_b_im'][d])
        y_d, f_re, f_im = s5_direction(u, a_re, a_im, bb_re, bb_im, p['s5_c_re'][d], p['s5_c_im'][d],
                                       h0[:, d, 0], h0[:, d, 1], rev)
        y = y + y_d
        finals.append(jnp.stack([f_re, f_im], axis=1))
    state = jnp.stack(finals, axis=1)
    y = jax.nn.gelu(y.reshape(b, L, S5_CH))
    y = y * jax.nn.sigmoid(y @ p['s5_w_glu'].astype(jnp.float32) + p['s5_b_glu'].astype(jnp.float32))
    return y.astype(u_flat.dtype), state.astype(u_flat.dtype)


def split_proj(z):
    return jnp.split(z, [NA_IN, NA_IN + S5_IN, NA_IN + S5_IN + GQ_IN], axis=-1)


def na_qkv(z, p):
    q, k, v = jnp.split(z, 3, axis=-1)
    return (rmsnorm(to_heads(q, NA_HEADS, NA_DH), p['na_qn']),
            rmsnorm(to_heads(k, NA_HEADS, NA_DH), p['na_kn']),
            to_heads(v, NA_HEADS, NA_DH))


def gq_qkv(z, p):
    q, k, v = jnp.split(z, [GQ_HEADS * GQ_DH, (GQ_HEADS + GQ_KV) * GQ_DH], axis=-1)
    return (rmsnorm(to_heads(q, GQ_HEADS, GQ_DH), p['gq_qn']),
            rmsnorm(to_heads(k, GQ_KV, GQ_DH), p['gq_kn']),
            to_heads(v, GQ_KV, GQ_DH))


def mla_latents(z, p):
    cq, ckv, krope = jnp.split(z, [MLA_QLORA, MLA_QLORA + MLA_KVLORA], axis=-1)
    q = rmsnorm(to_heads(rmsnorm(cq, p['mla_qa_g']) @ p['mla_w_uq'], MLA_HEADS, MLA_QK), p['mla_qn'])
    return q, rmsnorm(ckv, p['mla_kva_g']), krope


def mla_kv(ckv, krope, p):
    b, L, _ = ckv.shape
    kv = to_heads(ckv @ p['mla_w_ukv'], MLA_HEADS, MLA_NOPE + MLA_V)
    kr = jnp.broadcast_to(krope[:, None], (b, MLA_HEADS, L, MLA_ROPE)).astype(kv.dtype)
    k = rmsnorm(jnp.concatenate([kv[..., :MLA_NOPE], kr], axis=-1), p['mla_kn'])
    return k, kv[..., MLA_NOPE:]


def mixers_context(h, p):
    b = h.shape[0]
    na_z, s5_u, gq_z, mla_z = split_proj(h @ p['w_in'])
    nq, nk, nv = na_qkv(na_z, p)
    o_na = attend(nq, nk, nv, NA_DH ** -0.5)
    o_s5, s5_state = s5_mixer(s5_u, p, jnp.zeros((b, 2, 2, S5_GROUPS, S5_N), jnp.float32))
    gq, gk, gv = gq_qkv(gq_z, p)
    o_gq = attend(gq, gk, gv, GQ_DH ** -0.5)
    mq, ckv, krope = mla_latents(mla_z, p)
    mk, mv = mla_kv(ckv, krope, p)
    o_mla = attend(mq, mk, mv, MLA_QK ** -0.5)
    mixed = jnp.concatenate([from_heads(o_na), o_s5, from_heads(o_gq), from_heads(o_mla)], axis=-1)
    return mixed @ p['w_out'], (nk, nv, s5_state, gk, gv, ckv, krope)


def mixers_latent(h, ctx, p):
    na_kc, na_vc, s5_h0, gq_kc, gq_vc, ckv_c, krope_c = ctx
    t = h.shape[1]
    na_z, s5_u, gq_z, mla_z = split_proj(h @ p['w_in'])
    nq, nk, nv = na_qkv(na_z, p)
    o_na = na_latent(nq, nk, nv, na_kc, na_vc, p['na_rpb'])
    o_s5, _ = s5_mixer(s5_u, p, s5_h0)
    cos_g, sin_g = rope_2d_tables(t, GQ_DH)
    gq, gk, gv = gq_qkv(gq_z, p)
    gq = apply_rope_2d(gq, cos_g, sin_g)
    gk = apply_rope_2d(gk, cos_g, sin_g)
    o_gq = attend(gq, jnp.concatenate([gk, gq_kc], axis=2), jnp.concatenate([gv, gq_vc], axis=2), GQ_DH ** -0.5)
    cos_m, sin_m = rope_2d_tables(t, MLA_ROPE)
    mq, ckv, krope = mla_latents(mla_z, p)
    mk, mv = mla_kv(ckv, krope, p)
    mkc, mvc = mla_kv(ckv_c, krope_c, p)
    mq = rope_tail(mq, cos_m, sin_m, MLA_ROPE)
    mk = rope_tail(mk, cos_m, sin_m, MLA_ROPE)
    o_mla = attend(mq, jnp.concatenate([mk, mkc], axis=2), jnp.concatenate([mv, mvc], axis=2), MLA_QK ** -0.5)
    mixed = jnp.concatenate([from_heads(o_na), o_s5, from_heads(o_gq), from_heads(o_mla)], axis=-1)
    return mixed @ p['w_out']


def conv_ffn(h, p):
    u = h @ p['ffn_w_up']
    ch = u.shape[-1]
    rhs = p['ffn_conv_w'][:, None, :].astype(u.dtype)
    u = lax.conv_general_dilated(u, rhs, window_strides=(1,), padding=[(CONV_W // 2, CONV_W // 2)],
                                 dimension_numbers=('NWC', 'WIO', 'NWC'), feature_group_count=ch) + p['ffn_conv_b']
    gate, up = jnp.split(u, 2, axis=-1)
    return (jax.nn.silu(gate) * up) @ p['ffn_w_down']


def ada_mods(cvec, p):
    m = jax.nn.silu(cvec) @ p['ada_w'] + p['ada_b']
    return jnp.split(m[:, None, :], 6, axis=-1)


def trunk_layer(x, cvec, p, mixer_fn):
    sh1, sc1, g1, sh2, sc2, g2 = ada_mods(cvec, p)
    mixed, extra = mixer_fn(rmsnorm(x, p['norm1_g']) * (1 + sc1) + sh1)
    x = x + g1 * mixed
    x = x + g2 * conv_ffn(rmsnorm(x, p['norm2_g']) * (1 + sc2) + sh2, p)
    return x, extra


def setup_inputs(seed: int = 0) -> dict:
    key = jax.random.key(seed)
    ks = iter(jax.random.split(key, 64))

    def nrm(shape, scale):
        return jax.random.normal(next(ks), shape, jnp.float32) * scale

    def gain(shape):
        return 1.0 + nrm(shape, 0.01)

    L, Bd, P = DEPTH, DEC_BATCH, PAST_LEN
    n_idx = jnp.arange(S5_N, dtype=jnp.float32)
    inp = {}
    inp['x_prompt'] = nrm((BATCH, SEQ, D_MODEL), 1.0)
    inp['x_sample'] = nrm((Bd, DEC_SEQ, D_MODEL), 1.0)
    inp['cache_na_k'] = nrm((Bd, L, NA_HEADS, P, NA_DH), 1.0)
    inp['cache_na_v'] = nrm((Bd, L, NA_HEADS, P, NA_DH), 1.0)
    inp['state_s5'] = nrm((Bd, L, 2, 2, S5_GROUPS, S5_N), 0.1)
    inp['cache_gqa_k'] = nrm((Bd, L, GQ_KV, P, GQ_DH), 1.0)
    inp['cache_gqa_v'] = nrm((Bd, L, GQ_KV, P, GQ_DH), 1.0)
    inp['cache_mla_ckv'] = nrm((Bd, L, P, MLA_KVLORA), 1.0)
    inp['cache_mla_krope'] = nrm((Bd, L, P, MLA_ROPE), 1.0)
    inp['c'] = nrm((Bd, D_MODEL), 1.0)
    inp['c_ctx'] = nrm((D_MODEL,), 1.0)
    inp['norm1_g'] = gain((L, D_MODEL))
    inp['norm2_g'] = gain((L, D_MODEL))
    inp['ada_w'] = nrm((L, D_MODEL, 6 * D_MODEL), 0.5 * D_MODEL ** -0.5)
    inp['ada_b'] = nrm((L, 6 * D_MODEL), 0.01)
    inp['w_in'] = nrm((L, D_MODEL, D_IN), D_MODEL ** -0.5)
    inp['na_qn'] = gain((L, NA_DH))
    inp['na_kn'] = gain((L, NA_DH))
    inp['na_rpb'] = nrm((L, NA_HEADS, 2 * NA_WIN_R - 1, 2 * NA_WIN_C - 1), 0.1)
    inp['s5_lam_re'] = -0.5 + nrm((L, 2, S5_GROUPS, S5_N), 0.01)
    inp['s5_lam_im'] = jnp.pi * n_idx + nrm((L, 2, S5_GROUPS, S5_N), 0.01)
    inp['s5_log_dt'] = jax.random.uniform(next(ks), (L, 2, S5_GROUPS), jnp.float32,
                                          math.log(DT_MIN), math.log(DT_MAX))
    inp['s5_b_re'] = nrm((L, 2, S5_GROUPS, S5_N, S5_GROUP), (2 * S5_GROUP) ** -0.5)
    inp['s5_b_im'] = nrm((L, 2, S5_GROUPS, S5_N, S5_GROUP), (2 * S5_GROUP) ** -0.5)
    inp['s5_c_re'] = nrm((L, 2, S5_GROUPS, S5_GROUP, S5_N), S5_N ** -0.5)
    inp['s5_c_im'] = nrm((L, 2, S5_GROUPS, S5_GROUP, S5_N), S5_N ** -0.5)
    inp['s5_d'] = nrm((L, S5_CH), 1.0)
    inp['s5_w_glu'] = nrm((L, S5_CH, S5_CH), S5_CH ** -0.5)
    inp['s5_b_glu'] = nrm((L, S5_CH), 0.01)
    inp['gq_qn'] = gain((L, GQ_DH))
    inp['gq_kn'] = gain((L, GQ_DH))
    inp['mla_qa_g'] = gain((L, MLA_QLORA))
    inp['mla_kva_g'] = gain((L, MLA_KVLORA))
    inp['mla_w_uq'] = nrm((L, MLA_QLORA, MLA_HEADS * MLA_QK), MLA_QLORA ** -0.5)
    inp['mla_w_ukv'] = nrm((L, MLA_KVLORA, MLA_HEADS * (MLA_NOPE + MLA_V)), MLA_KVLORA ** -0.5)
    inp['mla_qn'] = gain((L, MLA_QK))
    inp['mla_kn'] = gain((L, MLA_QK))
    inp['w_out'] = nrm((L, D_CAT, D_MODEL), D_CAT ** -0.5)
    inp['ffn_w_up'] = nrm((L, D_MODEL, 2 * D_FF), D_MODEL ** -0.5)
    inp['ffn_conv_w'] = nrm((L, CONV_W, 2 * D_FF), CONV_W ** -0.5)
    inp['ffn_conv_b'] = nrm((L, 2 * D_FF), 0.01)
    inp['ffn_w_down'] = nrm((L, D_FF, D_MODEL), D_FF ** -0.5)
    return inp


def reference(x_prompt, x_sample, cache_na_k, cache_na_v, state_s5, cache_gqa_k, cache_gqa_v,
              cache_mla_ckv, cache_mla_krope, c, c_ctx, norm1_g, norm2_g, ada_w, ada_b, w_in,
              na_qn, na_kn, na_rpb, s5_lam_re, s5_lam_im, s5_log_dt, s5_b_re, s5_b_im, s5_c_re, s5_c_im,
              s5_d, s5_w_glu, s5_b_glu, gq_qn, gq_kn, mla_qa_g, mla_kva_g, mla_w_uq, mla_w_ukv,
              mla_qn, mla_kn, w_out, ffn_w_up, ffn_conv_w, ffn_conv_b, ffn_w_down):
    stacked = dict(norm1_g=norm1_g, norm2_g=norm2_g, ada_w=ada_w, ada_b=ada_b, w_in=w_in,
                   na_qn=na_qn, na_kn=na_kn, na_rpb=na_rpb, s5_lam_re=s5_lam_re, s5_lam_im=s5_lam_im,
                   s5_log_dt=s5_log_dt, s5_b_re=s5_b_re, s5_b_im=s5_b_im, s5_c_re=s5_c_re, s5_c_im=s5_c_im,
                   s5_d=s5_d, s5_w_glu=s5_w_glu, s5_b_glu=s5_b_glu, gq_qn=gq_qn, gq_kn=gq_kn,
                   mla_qa_g=mla_qa_g, mla_kva_g=mla_kva_g, mla_w_uq=mla_w_uq, mla_w_ukv=mla_w_ukv,
                   mla_qn=mla_qn, mla_kn=mla_kn, w_out=w_out, ffn_w_up=ffn_w_up,
                   ffn_conv_w=ffn_conv_w, ffn_conv_b=ffn_conv_b, ffn_w_down=ffn_w_down)

    y_prompt = x_prompt
    states = []
    for l in range(DEPTH):
        p = {name: w[l] for name, w in stacked.items()}
        y_prompt, st = trunk_layer(y_prompt, c_ctx[None, :], p, lambda hn: mixers_context(hn, p))
        states.append(st)
    new_na_k = jnp.stack([s[0] for s in states], axis=1)
    new_na_v = jnp.stack([s[1] for s in states], axis=1)
    new_s5 = jnp.stack([s[2] for s in states], axis=1)
    new_gqa_k = jnp.stack([s[3] for s in states], axis=1)
    new_gqa_v = jnp.stack([s[4] for s in states], axis=1)
    new_mla_ckv = jnp.stack([s[5] for s in states], axis=1)
    new_mla_krope = jnp.stack([s[6] for s in states], axis=1)

    y_sample = x_sample
    for l in range(DEPTH):
        p = {name: w[l] for name, w in stacked.items()}
        ctx = (cache_na_k[:, l], cache_na_v[:, l], state_s5[:, l], cache_gqa_k[:, l], cache_gqa_v[:, l],
               cache_mla_ckv[:, l], cache_mla_krope[:, l])
        y_sample, _ = trunk_layer(y_sample, c, p, lambda hn: (mixers_latent(hn, ctx, p), None))

    return (y_prompt, y_sample, new_na_k, new_na_v, new_s5, new_gqa_k, new_gqa_v, new_mla_ckv, new_mla_krope)
```

```python
import functools
import math

import numpy as np
import jax
import jax.numpy as jnp
from jax import lax
from jax.experimental import pallas as pl
from jax.experimental.pallas import tpu as pltpu

F32 = jnp.float32
BF16 = jnp.bfloat16

D_MODEL = 1024
BATCH = 32
SEQ = 256
DEPTH = 2
DEC_BATCH = 4
DEC_SEQ = 1024
PAST_LEN = 512
GRID_W = 64
GRID_ROWS = DEC_SEQ // GRID_W
MIX = D_MODEL // 4
DH = 64
NA_HEADS = MIX // DH
NA_WIN_R = 8
NA_WIN_C = 16
S5_GROUP = 16
S5_GROUPS = MIX // S5_GROUP
S5_N = 64
GQ_HEADS = MIX // DH
GQ_KV = GQ_HEADS // 2
MLA_HEADS = MIX // DH
MLA_NOPE = 64
MLA_ROPE = 32
MLA_QK = MLA_NOPE + MLA_ROPE
MLA_QLORA = (3 * D_MODEL) // 16
MLA_KVLORA = D_MODEL // 8
D_FF = 128 * ((8 * D_MODEL // 3 + 127) // 128)
ROPE_BASE = 10000.0
EPS = 1e-6
NEG = -1e30

LANES = 128
MXU_DIM = 256
VMEM_LIMIT = 48 * 1024 * 1024

Z_NA = 3 * MIX
Z_S5 = MIX
Z_GQ = 2 * MIX
Z_MLA = 3 * LANES
Z_ALL = Z_NA + Z_S5 + Z_GQ + Z_MLA

S5_CHUNK = 8
S5_ROW = S5_CHUNK * MIX
S5_STATE = S5_GROUPS * S5_N

FF_CHUNK = 256
FF_STEPS = D_FF // FF_CHUNK


def _cparams(sem):
    return pltpu.CompilerParams(dimension_semantics=sem, vmem_limit_bytes=VMEM_LIMIT)


def _dot(a, b):
    return jnp.dot(a, b, preferred_element_type=F32)


def _dot_nt(a, b):
    return lax.dot_general(a, b, (((1,), (1,)), ((), ())), preferred_element_type=F32)


def _rms_rows(x, gain, denom=None, valid=None):
    xx = x * x
    if valid is not None and valid != x.shape[-1]:
        lane = lax.broadcasted_iota(jnp.int32, x.shape, 1)
        xx = jnp.where(lane < valid, xx, 0.0)
    denom = denom or (valid or x.shape[-1])
    ss = jnp.sum(xx, axis=-1, keepdims=True)
    return x * lax.rsqrt(ss / denom + EPS) * gain


def _seg_rms(x, seg, denom, gain):
    rows, width = x.shape
    if seg % LANES == 0:
        parts = []
        for s in range(width // seg):
            xs = x[:, s * seg:(s + 1) * seg]
            ss = jnp.sum(xs * xs, axis=-1, keepdims=True)
            parts.append(xs * lax.rsqrt(ss / denom + EPS))
        return jnp.concatenate(parts, axis=-1) * gain
    seg_id = lax.broadcasted_iota(jnp.int32, x.shape, 1) // seg
    xx = x * x
    scale = jnp.zeros_like(x)
    for s in range(width // seg):
        m = seg_id == s
        ss = jnp.sum(jnp.where(m, xx, 0.0), axis=-1, keepdims=True)
        scale = jnp.where(m, lax.rsqrt(ss / denom + EPS), scale)
    return x * scale * gain


def _rope(x, cos, sin_a, sin_b, half):
    tiles = []
    for t in range(x.shape[-1] // LANES):
        xt = x[:, t * LANES:(t + 1) * LANES]
        up = pltpu.roll(xt, LANES - half, axis=1)
        dn = pltpu.roll(xt, half, axis=1)
        tiles.append(xt * cos + up * sin_a + dn * sin_b)
    return tiles[0] if len(tiles) == 1 else jnp.concatenate(tiles, axis=-1)


def _packed_attn(q, parts, nseg, scale, biases=None):
    tq, width = q.shape
    seg_id = lax.broadcasted_iota(jnp.int32, q.shape, 1) // DH
    lhs = jnp.concatenate([jnp.where(seg_id == h, q, 0.0) for h in range(nseg)], axis=0).astype(BF16)
    scores = []
    for i, (kb, _) in enumerate(parts):
        s = _dot_nt(lhs, kb) * scale
        if biases is not None and biases[i] is not None:
            s = s + biases[i]
        scores.append(s)
    m = scores[0].max(axis=-1, keepdims=True)
    for s in scores[1:]:
        m = jnp.maximum(m, s.max(axis=-1, keepdims=True))
    den = None
    pv = None
    for s, (_, vb) in zip(scores, parts):
        p = jnp.exp(s - m)
        ps = p.sum(axis=-1, keepdims=True)
        den = ps if den is None else den + ps
        c = _dot(p.astype(BF16), vb)
        pv = c if pv is None else pv + c
    pv = pv / den
    out = jnp.zeros((tq, width), F32)
    for h in range(nseg):
        out = out + jnp.where(seg_id == h, pv[h * tq:(h + 1) * tq], 0.0)
    return out


def _mla_attn(q, parts, scale):
    tq = q.shape[0]
    vseg = None
    out = jnp.zeros((tq, MIX), F32)
    for h in range(MLA_HEADS):
        qh = q[:, h * LANES:(h + 1) * LANES].astype(BF16)
        scores = [_dot_nt(qh, kb[:, h * LANES:(h + 1) * LANES]) * scale for kb, _ in parts]
        m = scores[0].max(axis=-1, keepdims=True)
        for s in scores[1:]:
            m = jnp.maximum(m, s.max(axis=-1, keepdims=True))
        den = None
        pv = None
        for s, (_, vb) in zip(scores, parts):
            p = jnp.exp(s - m)
            ps = p.sum(axis=-1, keepdims=True)
            den = ps if den is None else den + ps
            vseg = lax.broadcasted_iota(jnp.int32, vb.shape, 1) // DH
            c = _dot(p.astype(BF16), jnp.where(vseg == h, vb, jnp.zeros_like(vb)))
            pv = c if pv is None else pv + c
        out = out + pv / den
    return out


def _mla_qkv(zm, qa_g, kva_g, mqn_g, mkn_g, wuq, wuk, wuv):
    ckv = _rms_rows(zm[:, 0:LANES], kva_g)
    col3 = zm[:, 2 * LANES:3 * LANES]
    lane = lax.broadcasted_iota(jnp.int32, col3.shape, 1)
    kr_placed = jnp.where((lane >= MLA_NOPE) & (lane < MLA_QK), col3, 0.0)
    cq = _rms_rows(zm[:, LANES:3 * LANES], qa_g, valid=MLA_QLORA)
    q = _seg_rms(_dot(cq.astype(BF16), wuq), LANES, MLA_QK, mqn_g)
    k, v = _mla_kv(ckv, kr_placed, mkn_g, wuk, wuv)
    return q, k, v, ckv, col3


def _mla_kv(ckv, kr_placed, mkn_g, wuk, wuv):
    cb = ckv.astype(BF16)
    kf = _dot(cb, wuk) + jnp.concatenate([kr_placed] * MLA_HEADS, axis=-1)
    return _seg_rms(kf, LANES, MLA_QK, mkn_g), _dot(cb, wuv)


def _ada_kernel(c_ref, w_ref, b_ref, o_ref):
    cv = c_ref[...]
    s = cv * jax.nn.sigmoid(cv)
    o_ref[0] = _dot(s.astype(BF16), w_ref[0].astype(BF16)) + b_ref[0]


def _ada_mods(cvec, ada_w, ada_b):
    tn = 1536
    n = ada_w.shape[-1]
    return pl.pallas_call(
        _ada_kernel,
        grid=(DEPTH, n // tn),
        in_specs=[pl.BlockSpec((8, D_MODEL), lambda l, j: (0, 0)),
                  pl.BlockSpec((1, D_MODEL, tn), lambda l, j: (l, 0, j)),
                  pl.BlockSpec((1, 1, tn), lambda l, j: (l, 0, j))],
        out_specs=pl.BlockSpec((1, 8, tn), lambda l, j: (l, 0, j)),
        out_shape=jax.ShapeDtypeStruct((DEPTH, 8, n), F32),
        compiler_params=_cparams(("arbitrary", "arbitrary")),
        name="ada_mods",
    )(cvec, ada_w, ada_b.reshape(DEPTH, 1, n))


def _inproj_kernel(x_ref, mod_ref, g_ref, w_ref, ona_ref, os5_ref, ogq_ref, omla_ref):
    x = x_ref[...]
    y = x * lax.rsqrt(jnp.mean(x * x, axis=-1, keepdims=True) + EPS) * g_ref[...]
    h = y * (1.0 + mod_ref[0, 1:2, :]) + mod_ref[0, 0:1, :]
    z = _dot(h.astype(BF16), w_ref[...])
    ona_ref[...] = z[:, 0:Z_NA]
    os5_ref[...] = z[:, Z_NA:Z_NA + Z_S5]
    ogq_ref[...] = z[:, Z_NA + Z_S5:Z_NA + Z_S5 + Z_GQ]
    omla_ref[...] = z[:, Z_NA + Z_S5 + Z_GQ:Z_ALL]


def _inproj(x, mods, g, w, mod_base, rows_per_mod):
    n = x.shape[0]
    tm = 512
    row = lambda i: (mod_base + (i * tm) // rows_per_mod, 0, 0)
    widths = (Z_NA, Z_S5, Z_GQ, Z_MLA)
    return pl.pallas_call(
        _inproj_kernel,
        grid=(n // tm,),
        in_specs=[pl.BlockSpec((tm, D_MODEL), lambda i: (i, 0)),
                  pl.BlockSpec((1, 6, D_MODEL), row),
                  pl.BlockSpec((1, D_MODEL), lambda i: (0, 0)),
                  pl.BlockSpec((D_MODEL, Z_ALL), lambda i: (0, 0))],
        out_specs=[pl.BlockSpec((tm, wd), lambda i: (i, 0)) for wd in widths],
        out_shape=[jax.ShapeDtypeStruct((n, wd), F32) for wd in widths],
        compiler_params=_cparams(("arbitrary",)),
        name="inproj",
    )(x, mods, g, w)


def _ctx_attn_kernel(zna_ref, zgq_ref, zm_ref, naq_g, nak_g, gqq_g, gqk_g, qa_g, kva_g, mqn_g, mkn_g,
                     wuq_ref, wuk_ref, wuv_ref,
                     ona_ref, ogq_ref, omla_ref, nk_ref, nv_ref, gk_ref, gv_ref, ckv_ref, kr_ref):
    zna = zna_ref[...]
    q = _seg_rms(zna[:, 0:MIX], DH, DH, naq_g[...])
    k = _seg_rms(zna[:, MIX:2 * MIX], DH, DH, nak_g[...])
    v = zna[:, 2 * MIX:3 * MIX]
    nk_ref[...] = k
    nv_ref[...] = v
    ona_ref[...] = _packed_attn(q, [(k.astype(BF16), v.astype(BF16))], NA_HEADS, DH ** -0.5)

    zgq = zgq_ref[...]
    gq = _seg_rms(zgq[:, 0:MIX], DH, DH, gqq_g[...])
    gk = _seg_rms(zgq[:, MIX:MIX + LANES], DH, DH, gqk_g[...])
    gv = zgq[:, MIX + LANES:2 * MIX]
    gk_ref[...] = gk
    gv_ref[...] = gv
    kv = [(gk.astype(BF16), gv.astype(BF16))]
    ogq_ref[...] = jnp.concatenate(
        [_packed_attn(gq[:, r * LANES:(r + 1) * LANES], kv, GQ_KV, DH ** -0.5) for r in range(2)], axis=-1)

    mq, mk, mv, ckv, col3 = _mla_qkv(zm_ref[...], qa_g[...], kva_g[...], mqn_g[...], mkn_g[...],
                                     wuq_ref[...], wuk_ref[...], wuv_ref[...])
    ckv_ref[...] = ckv
    kr_ref[...] = pltpu.roll(col3, LANES - MLA_NOPE, axis=1)[:, 0:MLA_ROPE]
    omla_ref[...] = _mla_attn(mq, [(mk.astype(BF16), mv.astype(BF16))], MLA_QK ** -0.5)


def _full(shape):
    nd = len(shape)
    return pl.BlockSpec(shape, lambda *a: (0,) * nd)


def _ctx_attn(zna, zgq, zm, lw):
    n = zna.shape[0]
    t = SEQ
    rows = lambda wd: pl.BlockSpec((t, wd), lambda b: (b, 0))
    gains = [lw["naq_g"], lw["nak_g"], lw["gqq_g"], lw["gqk_g"], lw["qa_g"], lw["kva_g"], lw["mqn_g"], lw["mkn_g"]]
    weights = [lw["wuq"], lw["wuk"], lw["wuv"]]
    out_w = (MIX, MIX, MIX, MIX, MIX, LANES, LANES, MLA_KVLORA, MLA_ROPE)
    return pl.pallas_call(
        _ctx_attn_kernel,
        grid=(n // t,),
        in_specs=[rows(Z_NA), rows(Z_GQ), rows(Z_MLA)] + [_full(a.shape) for a in gains + weights],
        out_specs=[rows(wd) for wd in out_w],
        out_shape=[jax.ShapeDtypeStruct((n, wd), F32) for wd in out_w],
        compiler_params=_cparams(("arbitrary",)),
        name="ctx_attn",
    )(zna, zgq, zm, *gains, *weights)


def _na_window_start(r):
    return jnp.clip(r - NA_WIN_R // 2, 0, GRID_ROWS - NA_WIN_R)


def _na_lat_kernel(zna_ref, kc_ref, vc_ref, bias_ref, naq_g, nak_g, o_ref, k_scr, v_scr):
    r = pl.program_id(1)

    @pl.when(r == 0)
    def _():
        k_scr[...] = _seg_rms(zna_ref[:, MIX:2 * MIX], DH, DH, nak_g[...]).astype(BF16)
        v_scr[...] = zna_ref[:, 2 * MIX:3 * MIX].astype(BF16)

    q = _seg_rms(zna_ref[pl.ds(pl.multiple_of(r * GRID_W, GRID_W), GRID_W), 0:MIX], DH, DH, naq_g[...])
    start = pl.multiple_of(_na_window_start(r) * GRID_W, GRID_W)
    span = NA_WIN_R * GRID_W
    parts = [(k_scr[pl.ds(start, span), :], v_scr[pl.ds(start, span), :]),
             (kc_ref[0].astype(BF16), vc_ref[0].astype(BF16))]
    o_ref[...] = _packed_attn(q, parts, NA_HEADS, DH ** -0.5, biases=[bias_ref[0], None])


def _na_lat(zna, kc, vc, bias, lw):
    nb = DEC_BATCH
    d0 = lambda b, r: (_na_window_start(r) - r + NA_WIN_R - 1, 0, 0)
    return pl.pallas_call(
        _na_lat_kernel,
        grid=(nb, GRID_ROWS),
        in_specs=[pl.BlockSpec((DEC_SEQ, Z_NA), lambda b, r: (b, 0)),
                  pl.BlockSpec((1, PAST_LEN, MIX), lambda b, r: (b, 0, 0)),
                  pl.BlockSpec((1, PAST_LEN, MIX), lambda b, r: (b, 0, 0)),
                  pl.BlockSpec((1, NA_HEADS * GRID_W, NA_WIN_R * GRID_W), d0),
                  _full(lw["naq_g"].shape), _full(lw["nak_g"].shape)],
        out_specs=pl.BlockSpec((GRID_W, MIX), lambda b, r: (b * GRID_ROWS + r, 0)),
        out_shape=jax.ShapeDtypeStruct((nb * DEC_SEQ, MIX), F32),
        scratch_shapes=[pltpu.VMEM((DEC_SEQ, MIX), BF16), pltpu.VMEM((DEC_SEQ, MIX), BF16)],
        compiler_params=_cparams(("arbitrary", "arbitrary")),
        name="na_latent",
    )(zna, kc, vc, bias, lw["naq_g"], lw["nak_g"])


LAT_QB = 256
LAT_KEYS = DEC_SEQ + PAST_LEN


def _gq_lat_kernel(zgq_ref, kc_ref, vc_ref, cos_ref, sa_ref, sb_ref, gqq_g, gqk_g, o_ref, k_scr, v_scr):
    j = pl.program_id(1)

    @pl.when(j == 0)
    def _():
        k = _seg_rms(zgq_ref[:, MIX:MIX + LANES], DH, DH, gqk_g[...])
        k = _rope(k, cos_ref[...], sa_ref[...], sb_ref[...], DH // 4)
        k_scr[0:DEC_SEQ, :] = k.astype(BF16)
        k_scr[DEC_SEQ:LAT_KEYS, :] = kc_ref[0].astype(BF16)
        v_scr[0:DEC_SEQ, :] = zgq_ref[:, MIX + LANES:2 * MIX].astype(BF16)
        v_scr[DEC_SEQ:LAT_KEYS, :] = vc_ref[0].astype(BF16)

    rows = pl.ds(pl.multiple_of(j * LAT_QB, LAT_QB), LAT_QB)
    q = _seg_rms(zgq_ref[rows, 0:MIX], DH, DH, gqq_g[...])
    q = _rope(q, cos_ref[rows, :], sa_ref[rows, :], sb_ref[rows, :], DH // 4)
    kv = [(k_scr[...], v_scr[...])]
    o_ref[...] = jnp.concatenate(
        [_packed_attn(q[:, r * LANES:(r + 1) * LANES], kv, GQ_KV, DH ** -0.5) for r in range(2)], axis=-1)


def _gq_lat(zgq, kc, vc, tabs, lw):
    nb = DEC_BATCH
    nq = DEC_SEQ // LAT_QB
    return pl.pallas_call(
        _gq_lat_kernel,
        grid=(nb, nq),
        in_specs=[pl.BlockSpec((DEC_SEQ, Z_GQ), lambda b, j: (b, 0)),
                  pl.BlockSpec((1, PAST_LEN, LANES), lambda b, j: (b, 0, 0)),
                  pl.BlockSpec((1, PAST_LEN, LANES), lambda b, j: (b, 0, 0))]
                 + [_full(t.shape) for t in tabs] + [_full(lw["gqq_g"].shape), _full(lw["gqk_g"].shape)],
        out_specs=pl.BlockSpec((LAT_QB, MIX), lambda b, j: (b * nq + j, 0)),
        out_shape=jax.ShapeDtypeStruct((nb * DEC_SEQ, MIX), F32),
        scratch_shapes=[pltpu.VMEM((LAT_KEYS, LANES), BF16), pltpu.VMEM((LAT_KEYS, LANES), BF16)],
        compiler_params=_cparams(("arbitrary", "arbitrary")),
        name="gq_latent",
    )(zgq, kc, vc, *tabs, lw["gqq_g"], lw["gqk_g"])


def _mla_lat_kernel(zm_ref, ckvc_ref, krc_ref, cos_ref, sa_ref, sb_ref, qa_g, kva_g, mqn_g, mkn_g,
                    wuq_ref, wuk_ref, wuv_ref, o_ref, k_scr, v_scr):
    j = pl.program_id(1)
    half = MLA_ROPE // 4

    @pl.when(j == 0)
    def _():
        zm = zm_ref[...]
        ckv = _rms_rows(zm[:, 0:LANES], kva_g[...])
        col3 = zm[:, 2 * LANES:3 * LANES]
        lane = lax.broadcasted_iota(jnp.int32, col3.shape, 1)
        kr_placed = jnp.where((lane >= MLA_NOPE) & (lane < MLA_QK), col3, 0.0)
        k, v = _mla_kv(ckv, kr_placed, mkn_g[...], wuk_ref[...], wuv_ref[...])
        k = _rope(k, cos_ref[...], sa_ref[...], sb_ref[...], half)
        k_scr[0:DEC_SEQ, :] = k.astype(BF16)
        v_scr[0:DEC_SEQ, :] = v.astype(BF16)
        kc, vc = _mla_kv(ckvc_ref[0], krc_ref[0], mkn_g[...], wuk_ref[...], wuv_ref[...])
        k_scr[DEC_SEQ:LAT_KEYS, :] = kc.astype(BF16)
        v_scr[DEC_SEQ:LAT_KEYS, :] = vc.astype(BF16)

    rows = pl.ds(pl.multiple_of(j * LAT_QB, LAT_QB), LAT_QB)
    cq = _rms_rows(zm_ref[rows, LANES:3 * LANES], qa_g[...], valid=MLA_QLORA)
    q = _seg_rms(_dot(cq.astype(BF16), wuq_ref[...]), LANES, MLA_QK, mqn_g[...])
    q = _rope(q, cos_ref[rows, :], sa_ref[rows, :], sb_ref[rows, :], half)
    o_ref[...] = _mla_attn(q, [(k_scr[...], v_scr[...])], MLA_QK ** -0.5)


def _mla_lat(zm, ckvc, krc, tabs, lw):
    nb = DEC_BATCH
    nq = DEC_SEQ // LAT_QB
    small = [lw["qa_g"], lw["kva_g"], lw["mqn_g"], lw["mkn_g"], lw["wuq"], lw["wuk"], lw["wuv"]]
    return pl.pallas_call(
        _mla_lat_kernel,
        grid=(nb, nq),
        in_specs=[pl.BlockSpec((DEC_SEQ, Z_MLA), lambda b, j: (b, 0)),
                  pl.BlockSpec((1, PAST_LEN, LANES), lambda b, j: (b, 0, 0)),
                  pl.BlockSpec((1, PAST_LEN, LANES), lambda b, j: (b, 0, 0))]
                 + [_full(t.shape) for t in tabs] + [_full(a.shape) for a in small],
        out_specs=pl.BlockSpec((LAT_QB, MIX), lambda b, j: (b * nq + j, 0)),
        out_shape=jax.ShapeDtypeStruct((nb * DEC_SEQ, MIX), F32),
        scratch_shapes=[pltpu.VMEM((LAT_KEYS, MLA_HEADS * LANES), BF16), pltpu.VMEM((LAT_KEYS, MIX), BF16)],
        compiler_params=_cparams(("arbitrary", "arbitrary")),
        name="mla_latent",
    )(zm, ckvc, krc, *tabs, *small)


def _mm_kernel(x_ref, w_ref, o_ref):
    o_ref[...] = _dot(x_ref[...].astype(BF16), w_ref[...])


def _s5_in(u2, w1):
    rows = u2.shape[0]
    tm, tn = 512, 1024
    n = w1.shape[1]
    return pl.pallas_call(
        _mm_kernel,
        grid=(n // tn, rows // tm),
        in_specs=[pl.BlockSpec((tm, S5_ROW), lambda j, i: (i, 0)),
                  pl.BlockSpec((S5_ROW, tn), lambda j, i: (0, j))],
        out_specs=pl.BlockSpec((tm, tn), lambda j, i: (i, j)),
        out_shape=jax.ShapeDtypeStruct((rows, n), F32),
        compiler_params=_cparams(("arbitrary", "arbitrary")),
        name="s5_in",
    )(u2, w1)


def _s5_scan_kernel(nb, nch, sfr_ref, sfi_ref, sbr_ref, sbi_ref, a_ref, h0_ref,
                    hfr_ref, hfi_ref, hbr_ref, hbi_ref, fin_ref):
    def run(sr_ref, si_ref, hr_ref, hi_ref, ar, ai, h_re, h_im, order):
        for k in order:
            rows = pl.ds(k, nb, stride=nch)
            hr_ref[rows, :] = h_re
            hi_ref[rows, :] = h_im
            n_re = ar * h_re - ai * h_im + sr_ref[rows, :]
            n_im = ar * h_im + ai * h_re + si_ref[rows, :]
            h_re, h_im = n_re, n_im
        return h_re, h_im

    f_re, f_im = run(sfr_ref, sfi_ref, hfr_ref, hfi_ref, a_ref[0:1, :], a_ref[1:2, :],
                     h0_ref[0], h0_ref[1], range(nch))
    b_re, b_im = run(sbr_ref, sbi_ref, hbr_ref, hbi_ref, a_ref[2:3, :], a_ref[3:4, :],
                     h0_ref[2], h0_ref[3], range(nch - 1, -1, -1))
    fin_ref[0] = f_re
    fin_ref[1] = f_im
    fin_ref[2] = b_re
    fin_ref[3] = b_im


def _s5_scan(ys, a8, h0, nb, nch):
    rows = ys.shape[0]
    tl = LANES
    nt = S5_STATE // tl
    base = S5_ROW // tl
    s_spec = lambda q: pl.BlockSpec((rows, tl), lambda j: (0, base + q * nt + j))
    h_spec = pl.BlockSpec((rows, tl), lambda j: (0, j))
    outs = pl.pallas_call(
        functools.partial(_s5_scan_kernel, nb, nch),
        grid=(nt,),
        in_specs=[s_spec(0), s_spec(1), s_spec(2), s_spec(3),
                  pl.BlockSpec((4, tl), lambda j: (0, j)),
                  pl.BlockSpec((4, nb, tl), lambda j: (0, 0, j))],
        out_specs=[h_spec, h_spec, h_spec, h_spec, pl.BlockSpec((4, nb, tl), lambda j: (0, 0, j))],
        out_shape=[jax.ShapeDtypeStruct((rows, S5_STATE), F32)] * 4
                  + [jax.ShapeDtypeStruct((4, nb, S5_STATE), F32)],
        compiler_params=_cparams(("arbitrary",)),
        name="s5_scan",
    )(ys, ys, ys, ys, a8, h0)
    return outs[:4], outs[4]


def _s5_out_kernel(hfr_ref, hfi_ref, hbr_ref, hbi_ref, w2_ref, yi_ref, u_ref, d_ref, o_ref):
    acc = yi_ref[...] + d_ref[...] * u_ref[...]
    for q, h_ref in enumerate((hfr_ref, hfi_ref, hbr_ref, hbi_ref)):
        acc = acc + _dot(h_ref[...].astype(BF16), w2_ref[q * S5_STATE:(q + 1) * S5_STATE, :])
    o_ref[...] = acc


def _s5_out(hs, w2, ys, u2, d_row):
    rows = u2.shape[0]
    tm, tn = 512, 1024
    return pl.pallas_call(
        _s5_out_kernel,
        grid=(S5_ROW // tn, rows // tm),
        in_specs=[pl.BlockSpec((tm, S5_STATE), lambda j, i: (i, 0))] * 4
                 + [pl.BlockSpec((4 * S5_STATE, tn), lambda j, i: (0, j)),
                    pl.BlockSpec((tm, tn), lambda j, i: (i, j)),
                    pl.BlockSpec((tm, tn), lambda j, i: (i, j)),
                    pl.BlockSpec((1, tn), lambda j, i: (0, j))],
        out_specs=pl.BlockSpec((tm, tn), lambda j, i: (i, j)),
        out_shape=jax.ShapeDtypeStruct((rows, S5_ROW), F32),
        compiler_params=_cparams(("arbitrary", "arbitrary")),
        name="s5_out",
    )(*hs, w2, ys, u2, d_row)


def _s5_mixer(u, lw, h0, nb):
    n = u.shape[0]
    rows = n // S5_CHUNK
    nch = rows // nb
    u2 = u.reshape(rows, S5_ROW)
    ys = _s5_in(u2, lw["s5_w1"])
    hs, fin = _s5_scan(ys, lw["s5_a8"], h0, nb, nch)
    y2 = _s5_out(hs, lw["s5_w2"], ys, u2, lw["s5_d"])
    return y2.reshape(n, MIX), fin


def _outproj_kernel(ona_ref, ys5_ref, ogq_ref, omla_ref, x_ref, mod_ref, w_ref, wglu_ref, bglu_ref, o_ref):
    y = jax.nn.gelu(ys5_ref[...], approximate=True)
    y = y * jax.nn.sigmoid(_dot(y.astype(BF16), wglu_ref[...]) + bglu_ref[...])
    mixed = _dot(ona_ref[...].astype(BF16), w_ref[0:MIX, :])
    mixed = mixed + _dot(y.astype(BF16), w_ref[MIX:2 * MIX, :])
    mixed = mixed + _dot(ogq_ref[...].astype(BF16), w_ref[2 * MIX:3 * MIX, :])
    mixed = mixed + _dot(omla_ref[...].astype(BF16), w_ref[3 * MIX:4 * MIX, :])
    o_ref[...] = x_ref[...] + mod_ref[0, 2:3, :] * mixed


def _outproj(ona, ys5, ogq, omla, x, mods, lw, mod_base, rows_per_mod):
    n = x.shape[0]
    tm = 512
    row = lambda i: (mod_base + (i * tm) // rows_per_mod, 0, 0)
    part = pl.BlockSpec((tm, MIX), lambda i: (i, 0))
    return pl.pallas_call(
        _outproj_kernel,
        grid=(n // tm,),
        in_specs=[part, part, part, part,
                  pl.BlockSpec((tm, D_MODEL), lambda i: (i, 0)),
                  pl.BlockSpec((1, 6, D_MODEL), row),
                  _full(lw["w_out"].shape), _full(lw["w_glu"].shape), _full(lw["b_glu"].shape)],
        out_specs=pl.BlockSpec((tm, D_MODEL), lambda i: (i, 0)),
        out_shape=jax.ShapeDtypeStruct((n, D_MODEL), F32),
        compiler_params=_cparams(("arbitrary",)),
        name="outproj",
    )(ona, ys5, ogq, omla, x, mods, lw["w_out"], lw["w_glu"], lw["b_glu"])


FF_ROWS = 1024


def _ffn_kernel(seq, x_ref, mod_ref, g_ref, wg_ref, wu_ref, cwg_ref, cwu_ref, cbg_ref, cbu_ref, wd_ref,
                o_ref, h_scr):
    j = pl.program_id(1)

    @pl.when(j == 0)
    def _():
        x = x_ref[...]
        y = x * lax.rsqrt(jnp.mean(x * x, axis=-1, keepdims=True) + EPS) * g_ref[...]
        h_scr[...] = (y * (1.0 + mod_ref[0, 4:5, :]) + mod_ref[0, 3:4, :]).astype(BF16)

    pos = lax.broadcasted_iota(jnp.int32, (FF_ROWS, FF_CHUNK), 0) % seq
    first = pos == 0
    last = pos == seq - 1

    def conv(u, cw_ref, cb_ref):
        prev = jnp.where(first, 0.0, pltpu.roll(u, 1, axis=0))
        nxt = jnp.where(last, 0.0, pltpu.roll(u, FF_ROWS - 1, axis=0))
        return cw_ref[0:1, :] * prev + cw_ref[1:2, :] * u + cw_ref[2:3, :] * nxt + cb_ref[...]

    h = h_scr[...]
    gate = conv(_dot(h, wg_ref[...]), cwg_ref, cbg_ref)
    up = conv(_dot(h, wu_ref[...]), cwu_ref, cbu_ref)
    act = gate * jax.nn.sigmoid(gate) * up
    contrib = _dot(act.astype(BF16), wd_ref[...])

    @pl.when(j == 0)
    def _():
        o_ref[...] = contrib

    @pl.when(j > 0)
    def _():
        o_ref[...] += contrib

    @pl.when(j == FF_STEPS - 1)
    def _():
        o_ref[...] = x_ref[...] + mod_ref[0, 5:6, :] * o_ref[...]


def _ffn(x, mods, lw, mod_base, rows_per_mod, seq):
    n = x.shape[0]
    row = lambda i, j: (mod_base + (i * FF_ROWS) // rows_per_mod, 0, 0)
    return pl.pallas_call(
        functools.partial(_ffn_kernel, seq),
        grid=(n // FF_ROWS, FF_STEPS),
        in_specs=[pl.BlockSpec((FF_ROWS, D_MODEL), lambda i, j: (i, 0)),
                  pl.BlockSpec((1, 6, D_MODEL), row),
                  pl.BlockSpec((1, D_MODEL), lambda i, j: (0, 0)),
                  pl.BlockSpec((D_MODEL, FF_CHUNK), lambda i, j: (0, j)),
                  pl.BlockSpec((D_MODEL, FF_CHUNK), lambda i, j: (0, FF_STEPS + j)),
                  pl.BlockSpec((3, FF_CHUNK), lambda i, j: (0, j)),
                  pl.BlockSpec((3, FF_CHUNK), lambda i, j: (0, FF_STEPS + j)),
                  pl.BlockSpec((1, FF_CHUNK), lambda i, j: (0, j)),
                  pl.BlockSpec((1, FF_CHUNK), lambda i, j: (0, FF_STEPS + j)),
                  pl.BlockSpec((FF_CHUNK, D_MODEL), lambda i, j: (j, 0))],
        out_specs=pl.BlockSpec((FF_ROWS, D_MODEL), lambda i, j: (i, 0)),
        out_shape=jax.ShapeDtypeStruct((n, D_MODEL), F32),
        scratch_shapes=[pltpu.VMEM((FF_ROWS, D_MODEL), BF16)],
        compiler_params=_cparams(("arbitrary", "arbitrary")),
        name="conv_ffn",
    )(x, mods, lw["norm2_g"], lw["w_up"], lw["w_up"], lw["conv_w"], lw["conv_w"],
      lw["conv_b"], lw["conv_b"], lw["w_down"])


def _rope_tables():
    pos = np.arange(DEC_SEQ)

    def ang(p, half):
        inv = ROPE_BASE ** (-np.arange(half, dtype=np.float64) / half)
        a = p.astype(np.float64)[:, None] * inv[None, :]
        return np.concatenate([a, a], axis=-1)

    def tables(dim, lanes_before, lanes_after, reps):
        a = np.concatenate([ang(pos // GRID_W, dim // 4), ang(pos % GRID_W, dim // 4)], axis=-1)
        half = dim // 4
        first = (np.arange(dim) % (2 * half)) < half
        cos = np.cos(a)
        sin_a = np.where(first[None, :], -np.sin(a), 0.0)
        sin_b = np.where(first[None, :], 0.0, np.sin(a))

        def place(t, fill):
            t = np.concatenate([np.full((DEC_SEQ, lanes_before), fill), t,
                                np.full((DEC_SEQ, lanes_after), fill)], axis=-1)
            return jnp.asarray(np.tile(t, (1, reps)), F32)

        return place(cos, 1.0), place(sin_a, 0.0), place(sin_b, 0.0)

    gq = tables(DH, 0, 0, LANES // DH)
    mla = tables(MLA_ROPE, MLA_NOPE, LANES - MLA_QK, 1)
    return gq, mla


def _na_bias_tables(rpb):
    qc = np.arange(GRID_W)
    kc = np.arange(GRID_W)
    col_start = np.clip(qc - NA_WIN_C // 2, 0, GRID_W - NA_WIN_C)
    valid = (kc[None, :] >= col_start[:, None]) & (kc[None, :] < col_start[:, None] + NA_WIN_C)
    d_c = np.clip(kc[None, :] - qc[:, None], 1 - NA_WIN_C, NA_WIN_C - 1) + NA_WIN_C - 1
    d_r = np.arange(NA_WIN_R)[:, None] + np.arange(NA_WIN_R)[None, :]
    t = rpb.astype(F32)[:, d_r[:, :, None, None], d_c[None, None, :, :]]
    t = jnp.where(jnp.asarray(valid)[None, None, None], t, NEG)
    t = t.transpose(1, 0, 3, 2, 4)
    return t.reshape(NA_WIN_R, NA_HEADS * GRID_W, NA_WIN_R * GRID_W)


def _s5_matrices(lam_re, lam_im, log_dt, b_re, b_im, c_re, c_im):
    hi = lax.Precision.HIGHEST
    eye = jnp.eye(S5_GROUPS, dtype=F32)
    c = S5_CHUNK
    toe = 0.0
    w_in, w_out, a8 = [], [], []
    for d in range(2):
        dt = jnp.exp(log_dt[d].astype(F32))[:, None]
        lr, li = lam_re[d].astype(F32), lam_im[d].astype(F32)
        mag = jnp.exp(lr * dt)
        a_re, a_im = mag * jnp.cos(li * dt), mag * jnp.sin(li * dt)
        den = lr * lr + li * li
        f_re = ((a_re - 1.0) * lr + a_im * li) / den
        f_im = (a_im * lr - (a_re - 1.0) * li) / den
        br, bi = b_re[d].astype(F32), b_im[d].astype(F32)
        bb_re = f_re[..., None] * br - f_im[..., None] * bi
        bb_im = f_re[..., None] * bi + f_im[..., None] * br
        p_re, p_im = [jnp.ones_like(a_re)], [jnp.zeros_like(a_re)]
        for _ in range(c):
            p_re, p_im = (p_re + [p_re[-1] * a_re - p_im[-1] * a_im],
                          p_im + [p_re[-1] * a_im + p_im[-1] * a_re])
        p_re, p_im = jnp.stack(p_re), jnp.stack(p_im)
        cr, ci = c_re[d].astype(F32), c_im[d].astype(F32)
        q_re = cr[None] * p_re[:, :, None, :] - ci[None] * p_im[:, :, None, :]
        q_im = cr[None] * p_im[:, :, None, :] + ci[None] * p_re[:, :, None, :]
        kern = (jnp.einsum("jgon,gnc->jgoc", q_re, bb_re, precision=hi)
                - jnp.einsum("jgon,gnc->jgoc", q_im, bb_im, precision=hi))
        s_idx, t_idx = np.arange(c)[:, None], np.arange(c)[None, :]
        lag = (t_idx - s_idx) if d == 0 else (s_idx - t_idx)
        kts = jnp.where(jnp.asarray(lag >= 0)[:, :, None, None, None], kern[np.clip(lag, 0, c)], 0.0)
        toe = toe + (kts.transpose(0, 2, 4, 1, 3)[:, :, :, :, None, :]
                     * eye[None, :, None, None, :, None]).reshape(S5_ROW, S5_ROW)
        pw = np.arange(c)[::-1] if d == 0 else np.arange(c)
        w_re = p_re[pw][:, :, :, None] * bb_re[None] - p_im[pw][:, :, :, None] * bb_im[None]
        w_im = p_re[pw][:, :, :, None] * bb_im[None] + p_im[pw][:, :, :, None] * bb_re[None]
        for w in (w_re, w_im):
            w_in.append((w.transpose(0, 1, 3, 2)[:, :, :, None, :] * eye[None, :, None, :, None])
                        .reshape(S5_ROW, S5_STATE))
        pw = np.arange(1, c + 1) if d == 0 else np.arange(c, 0, -1)
        for w in (q_re[pw], -q_im[pw]):
            w_out.append((w.transpose(1, 3, 0, 2)[:, :, :, None, :] * eye[:, None, None, :, None])
                         .reshape(S5_STATE, S5_ROW))
        a8 += [p_re[c].reshape(1, S5_STATE), p_im[c].reshape(1, S5_STATE)]
    w1 = jnp.concatenate([toe] + w_in, axis=1).astype(BF16)
    w2 = jnp.concatenate(w_out, axis=0).astype(BF16)
    return w1, w2, jnp.concatenate(a8, axis=0)


def _layer_weights(l, p):
    w = p["w_in"][l]
    q0 = Z_NA + Z_S5
    gq = lambda h: w[:, q0 + h * DH:q0 + (h + 1) * DH]
    m0 = q0 + 2 * MIX
    w_in = jnp.concatenate(
        [w[:, 0:q0], gq(0), gq(2), gq(1), gq(3), w[:, q0 + MIX:m0],
         w[:, m0 + MLA_QLORA:m0 + MLA_QLORA + MLA_KVLORA], w[:, m0:m0 + MLA_QLORA],
         w[:, m0 + MLA_QLORA + MLA_KVLORA:], jnp.zeros((D_MODEL, Z_ALL - w.shape[1]), w.dtype)],
        axis=1).astype(BF16)
    wo = p["w_out"][l]
    og = lambda h: wo[2 * MIX + h * DH:2 * MIX + (h + 1) * DH]
    w_out = jnp.concatenate([wo[0:2 * MIX], og(0), og(2), og(1), og(3), wo[3 * MIX:]], axis=0).astype(BF16)
    pad_head = lambda g: jnp.tile(jnp.pad(g, (0, LANES - MLA_QK)), MLA_HEADS)[None, :]
    wuq = jnp.pad(p["mla_w_uq"][l].reshape(MLA_QLORA, MLA_HEADS, MLA_QK),
                  ((0, 2 * LANES - MLA_QLORA), (0, 0), (0, LANES - MLA_QK))).reshape(2 * LANES, MLA_HEADS * LANES)
    wukv = p["mla_w_ukv"][l].reshape(MLA_KVLORA, MLA_HEADS, MLA_NOPE + DH)
    wuk = jnp.pad(wukv[:, :, :MLA_NOPE], ((0, 0), (0, 0), (0, LANES - MLA_NOPE))).reshape(MLA_KVLORA, MLA_HEADS * LANES)
    wuv = wukv[:, :, MLA_NOPE:].reshape(MLA_KVLORA, MIX)
    s5_w1, s5_w2, s5_a8 = _s5_matrices(p["s5_lam_re"][l], p["s5_lam_im"][l], p["s5_log_dt"][l],
                                       p["s5_b_re"][l], p["s5_b_im"][l], p["s5_c_re"][l], p["s5_c_im"][l])
    return dict(
        norm1_g=p["norm1_g"][l][None, :], norm2_g=p["norm2_g"][l][None, :],
        w_in=w_in, w_out=w_out,
        naq_g=jnp.tile(p["na_qn"][l], NA_HEADS)[None, :], nak_g=jnp.tile(p["na_kn"][l], NA_HEADS)[None, :],
        gqq_g=jnp.tile(p["gq_qn"][l], GQ_HEADS)[None, :], gqk_g=jnp.tile(p["gq_kn"][l], GQ_KV)[None, :],
        qa_g=jnp.pad(p["mla_qa_g"][l], (0, 2 * LANES - MLA_QLORA))[None, :], kva_g=p["mla_kva_g"][l][None, :],
        mqn_g=pad_head(p["mla_qn"][l]), mkn_g=pad_head(p["mla_kn"][l]),
        wuq=wuq.astype(BF16), wuk=wuk.astype(BF16), wuv=wuv.astype(BF16),
        na_bias=_na_bias_tables(p["na_rpb"][l]),
        s5_w1=s5_w1, s5_w2=s5_w2, s5_a8=s5_a8, s5_d=jnp.tile(p["s5_d"][l], S5_CHUNK)[None, :],
        w_glu=p["s5_w_glu"][l].astype(BF16), b_glu=p["s5_b_glu"][l][None, :],
        w_up=p["ffn_w_up"][l].astype(BF16), conv_w=p["ffn_conv_w"][l], conv_b=p["ffn_conv_b"][l][None, :],
        w_down=p["ffn_w_down"][l].astype(BF16),
    )


def _heads_first(x, nb, t, heads):
    return x.reshape(nb, t, heads, DH).transpose(0, 2, 1, 3)


def _lanes_packed(x):
    b, h, pl_, dh = x.shape
    return x.transpose(0, 2, 1, 3).reshape(b, pl_, h * dh)


def kernel(x_prompt, x_sample, cache_na_k, cache_na_v, state_s5, cache_gqa_k, cache_gqa_v, cache_mla_ckv,
           cache_mla_krope, c, c_ctx, norm1_g, norm2_g, ada_w, ada_b, w_in, na_qn, na_kn, na_rpb, s5_lam_re,
           s5_lam_im, s5_log_dt, s5_b_re, s5_b_im, s5_c_re, s5_c_im, s5_d, s5_w_glu, s5_b_glu, gq_qn, gq_kn,
           mla_qa_g, mla_kva_g, mla_w_uq, mla_w_ukv, mla_qn, mla_kn, w_out, ffn_w_up, ffn_conv_w, ffn_conv_b,
           ffn_w_down):
    p = dict(norm1_g=norm1_g, norm2_g=norm2_g, w_in=w_in, na_qn=na_qn, na_kn=na_kn, na_rpb=na_rpb,
             s5_lam_re=s5_lam_re, s5_lam_im=s5_lam_im, s5_log_dt=s5_log_dt, s5_b_re=s5_b_re, s5_b_im=s5_b_im,
             s5_c_re=s5_c_re, s5_c_im=s5_c_im, s5_d=s5_d, s5_w_glu=s5_w_glu, s5_b_glu=s5_b_glu,
             gq_qn=gq_qn, gq_kn=gq_kn, mla_qa_g=mla_qa_g, mla_kva_g=mla_kva_g, mla_w_uq=mla_w_uq,
             mla_w_ukv=mla_w_ukv, mla_qn=mla_qn, mla_kn=mla_kn, w_out=w_out, ffn_w_up=ffn_w_up,
             ffn_conv_w=ffn_conv_w, ffn_conv_b=ffn_conv_b, ffn_w_down=ffn_w_down)
    nb, nd = BATCH, DEC_BATCH
    cvec = jnp.concatenate([c_ctx[None, :], c, jnp.zeros((8 - 1 - nd, D_MODEL), F32)], axis=0)
    mods_all = _ada_mods(cvec, ada_w, ada_b).reshape(DEPTH, 8, 6, D_MODEL)
    gq_tabs, mla_tabs = _rope_tables()

    xc = x_prompt.reshape(nb * SEQ, D_MODEL)
    xl = x_sample.reshape(nd * DEC_SEQ, D_MODEL)
    zero_state = jnp.zeros((4, nb, S5_STATE), F32)
    caches = []
    for l in range(DEPTH):
        lw = _layer_weights(l, p)
        mods = mods_all[l]

        zna, zs5, zgq, zm = _inproj(xc, mods, lw["norm1_g"], lw["w_in"], 0, nb * SEQ)
        ona, ogq, omla, nk, nv, gk, gv, ckv, kr = _ctx_attn(zna, zgq, zm, lw)
        ys5, fin = _s5_mixer(zs5, lw, zero_state, nb)
        xc = _outproj(ona, ys5, ogq, omla, xc, mods, lw, 0, nb * SEQ)
        xc = _ffn(xc, mods, lw, 0, nb * SEQ, SEQ)
        caches.append((_heads_first(nk, nb, SEQ, NA_HEADS), _heads_first(nv, nb, SEQ, NA_HEADS),
                       fin.reshape(2, 2, nb, S5_GROUPS, S5_N).transpose(2, 0, 1, 3, 4),
                       _heads_first(gk, nb, SEQ, GQ_KV), _heads_first(gv, nb, SEQ, GQ_KV),
                       ckv.reshape(nb, SEQ, MLA_KVLORA), kr.reshape(nb, SEQ, MLA_ROPE)))

        zna, zs5, zgq, zm = _inproj(xl, mods, lw["norm1_g"], lw["w_in"], 1, DEC_SEQ)
        ona = _na_lat(zna, _lanes_packed(cache_na_k[:, l]), _lanes_packed(cache_na_v[:, l]), lw["na_bias"], lw)
        ogq = _gq_lat(zgq, _lanes_packed(cache_gqa_k[:, l]), _lanes_packed(cache_gqa_v[:, l]), gq_tabs, lw)
        krc = jnp.pad(cache_mla_krope[:, l], ((0, 0), (0, 0), (MLA_NOPE, LANES - MLA_QK)))
        omla = _mla_lat(zm, cache_mla_ckv[:, l], krc, mla_tabs, lw)
        h0 = state_s5[:, l].astype(F32).reshape(nd, 4, S5_STATE).transpose(1, 0, 2)
        ys5, _ = _s5_mixer(zs5, lw, h0, nd)
        xl = _outproj(ona, ys5, ogq, omla, xl, mods, lw, 1, DEC_SEQ)
        xl = _ffn(xl, mods, lw, 1, DEC_SEQ, DEC_SEQ)

    stack = lambda i: jnp.stack([cl[i] for cl in caches], axis=1)
    return (xc.reshape(nb, SEQ, D_MODEL), xl.reshape(nd, DEC_SEQ, D_MODEL),
            stack(0), stack(1), stack(2), stack(3), stack(4), stack(5), stack(6))
```

```python
import functools
import math

import numpy as np
import jax
import jax.numpy as jnp
from jax import lax
from jax.experimental import pallas as pl
from jax.experimental.pallas import tpu as pltpu

F32 = jnp.float32
BF16 = jnp.bfloat16

D_MODEL = 1024
BATCH = 32
SEQ = 256
DEPTH = 2
DEC_BATCH = 4
DEC_SEQ = 1024
PAST_LEN = 512
GRID_W = 64
GRID_ROWS = DEC_SEQ // GRID_W
MIX = D_MODEL // 4
DH = 64
NA_HEADS = MIX // DH
NA_WIN_R = 8
NA_WIN_C = 16
S5_GROUP = 16
S5_GROUPS = MIX // S5_GROUP
S5_N = 64
GQ_HEADS = MIX // DH
GQ_KV = GQ_HEADS // 2
MLA_HEADS = MIX // DH
MLA_NOPE = 64
MLA_ROPE = 32
MLA_QK = MLA_NOPE + MLA_ROPE
MLA_QLORA = (3 * D_MODEL) // 16
MLA_KVLORA = D_MODEL // 8
D_FF = 128 * ((8 * D_MODEL // 3 + 127) // 128)
ROPE_BASE = 10000.0
EPS = 1e-6
NEG = -1e30

LANES = 128
MXU_DIM = 256
VMEM_LIMIT = 48 * 1024 * 1024

Z_NA = 3 * MIX
Z_S5 = MIX
Z_GQ = 2 * MIX
Z_MLA = 3 * LANES
Z_ALL = Z_NA + Z_S5 + Z_GQ + Z_MLA

S5_CHUNK = 16
S5_ROW = S5_CHUNK * S5_GROUP

FF_CHUNK = 256
FF_STEPS = D_FF // FF_CHUNK


def _cparams(sem):
    return pltpu.CompilerParams(dimension_semantics=sem, vmem_limit_bytes=VMEM_LIMIT)


def _dot(a, b):
    return jnp.dot(a, b, preferred_element_type=F32)


def _dot_nt(a, b):
    return lax.dot_general(a, b, (((1,), (1,)), ((), ())), preferred_element_type=F32)


def _rms_rows(x, gain, denom=None, valid=None):
    xx = x * x
    if valid is not None and valid != x.shape[-1]:
        lane = lax.broadcasted_iota(jnp.int32, x.shape, 1)
        xx = jnp.where(lane < valid, xx, 0.0)
    denom = denom or (valid or x.shape[-1])
    ss = jnp.sum(xx, axis=-1, keepdims=True)
    return x * lax.rsqrt(ss / denom + EPS) * gain


def _seg_rms(x, seg, denom, gain):
    rows, width = x.shape
    if seg % LANES == 0:
        parts = []
        for s in range(width // seg):
            xs = x[:, s * seg:(s + 1) * seg]
            ss = jnp.sum(xs * xs, axis=-1, keepdims=True)
            parts.append(xs * lax.rsqrt(ss / denom + EPS))
        return jnp.concatenate(parts, axis=-1) * gain
    seg_id = lax.broadcasted_iota(jnp.int32, x.shape, 1) // seg
    xx = x * x
    scale = jnp.zeros_like(x)
    for s in range(width // seg):
        m = seg_id == s
        ss = jnp.sum(jnp.where(m, xx, 0.0), axis=-1, keepdims=True)
        scale = jnp.where(m, lax.rsqrt(ss / denom + EPS), scale)
    return x * scale * gain


def _rope(x, cos, sin_a, sin_b, half):
    tiles = []
    for t in range(x.shape[-1] // LANES):
        xt = x[:, t * LANES:(t + 1) * LANES]
        up = pltpu.roll(xt, LANES - half, axis=1)
        dn = pltpu.roll(xt, half, axis=1)
        tiles.append(xt * cos + up * sin_a + dn * sin_b)
    return tiles[0] if len(tiles) == 1 else jnp.concatenate(tiles, axis=-1)


def _packed_attn(q, parts, nseg, scale, biases=None):
    tq, width = q.shape
    seg_id = lax.broadcasted_iota(jnp.int32, q.shape, 1) // DH
    lhs = jnp.concatenate([jnp.where(seg_id == h, q, 0.0) for h in range(nseg)], axis=0).astype(BF16)
    scores = []
    for i, (kb, _) in enumerate(parts):
        s = _dot_nt(lhs, kb) * scale
        if biases is not None and biases[i] is not None:
            s = s + biases[i]
        scores.append(s)
    m = scores[0].max(axis=-1, keepdims=True)
    for s in scores[1:]:
        m = jnp.maximum(m, s.max(axis=-1, keepdims=True))
    den = None
    pv = None
    for s, (_, vb) in zip(scores, parts):
        p = jnp.exp(s - m)
        ps = p.sum(axis=-1, keepdims=True)
        den = ps if den is None else den + ps
        c = _dot(p.astype(BF16), vb)
        pv = c if pv is None else pv + c
    pv = pv / den
    out = jnp.zeros((tq, width), F32)
    for h in range(nseg):
        out = out + jnp.where(seg_id == h, pv[h * tq:(h + 1) * tq], 0.0)
    return out


def _mla_attn(q, parts, scale):
    tq = q.shape[0]
    vseg = None
    out = jnp.zeros((tq, MIX), F32)
    for h in range(MLA_HEADS):
        qh = q[:, h * LANES:(h + 1) * LANES].astype(BF16)
        scores = [_dot_nt(qh, kb[:, h * LANES:(h + 1) * LANES]) * scale for kb, _ in parts]
        m = scores[0].max(axis=-1, keepdims=True)
        for s in scores[1:]:
            m = jnp.maximum(m, s.max(axis=-1, keepdims=True))
        den = None
        pv = None
        for s, (_, vb) in zip(scores, parts):
            p = jnp.exp(s - m)
            ps = p.sum(axis=-1, keepdims=True)
            den = ps if den is None else den + ps
            vseg = lax.broadcasted_iota(jnp.int32, vb.shape, 1) // DH
            c = _dot(p.astype(BF16), jnp.where(vseg == h, vb, jnp.zeros_like(vb)))
            pv = c if pv is None else pv + c
        out = out + pv / den
    return out


def _mla_qkv(zm, qa_g, kva_g, mqn_g, mkn_g, wuq, wuk, wuv):
    ckv = _rms_rows(zm[:, 0:LANES], kva_g)
    col3 = zm[:, 2 * LANES:3 * LANES]
    lane = lax.broadcasted_iota(jnp.int32, col3.shape, 1)
    kr_placed = jnp.where((lane >= MLA_NOPE) & (lane < MLA_QK), col3, 0.0)
    cq = _rms_rows(zm[:, LANES:3 * LANES], qa_g, valid=MLA_QLORA)
    q = _seg_rms(_dot(cq.astype(BF16), wuq), LANES, MLA_QK, mqn_g)
    k, v = _mla_kv(ckv, kr_placed, mkn_g, wuk, wuv)
    return q, k, v, ckv, col3


def _mla_kv(ckv, kr_placed, mkn_g, wuk, wuv):
    cb = ckv.astype(BF16)
    kf = _dot(cb, wuk) + jnp.concatenate([kr_placed] * MLA_HEADS, axis=-1)
    return _seg_rms(kf, LANES, MLA_QK, mkn_g), _dot(cb, wuv)


def _ada_kernel(c_ref, w_ref, b_ref, o_ref):
    cv = c_ref[...]
    s = cv * jax.nn.sigmoid(cv)
    o_ref[0] = _dot(s.astype(BF16), w_ref[0].astype(BF16)) + b_ref[0]


def _ada_mods(cvec, ada_w, ada_b):
    tn = 1536
    n = ada_w.shape[-1]
    return pl.pallas_call(
        _ada_kernel,
        grid=(DEPTH, n // tn),
        in_specs=[pl.BlockSpec((8, D_MODEL), lambda l, j: (0, 0)),
                  pl.BlockSpec((1, D_MODEL, tn), lambda l, j: (l, 0, j)),
                  pl.BlockSpec((1, 1, tn), lambda l, j: (l, 0, j))],
        out_specs=pl.BlockSpec((1, 8, tn), lambda l, j: (l, 0, j)),
        out_shape=jax.ShapeDtypeStruct((DEPTH, 8, n), F32),
        compiler_params=_cparams(("arbitrary", "arbitrary")),
        name="ada_mods",
    )(cvec, ada_w, ada_b.reshape(DEPTH, 1, n))


def _inproj_kernel(x_ref, mod_ref, g_ref, w_ref, ona_ref, os5_ref, ogq_ref, omla_ref):
    x = x_ref[...]
    y = x * lax.rsqrt(jnp.mean(x * x, axis=-1, keepdims=True) + EPS) * g_ref[...]
    h = y * (1.0 + mod_ref[0, 1:2, :]) + mod_ref[0, 0:1, :]
    z = _dot(h.astype(BF16), w_ref[...])
    ona_ref[...] = z[:, 0:Z_NA]
    os5_ref[...] = z[:, Z_NA:Z_NA + Z_S5]
    ogq_ref[...] = z[:, Z_NA + Z_S5:Z_NA + Z_S5 + Z_GQ]
    omla_ref[...] = z[:, Z_NA + Z_S5 + Z_GQ:Z_ALL]


def _inproj(x, mods, g, w, mod_base, rows_per_mod):
    n = x.shape[0]
    tm = 512
    row = lambda i: (mod_base + (i * tm) // rows_per_mod, 0, 0)
    widths = (Z_NA, Z_S5, Z_GQ, Z_MLA)
    return pl.pallas_call(
        _inproj_kernel,
        grid=(n // tm,),
        in_specs=[pl.BlockSpec((tm, D_MODEL), lambda i: (i, 0)),
                  pl.BlockSpec((1, 6, D_MODEL), row),
                  pl.BlockSpec((1, D_MODEL), lambda i: (0, 0)),
                  pl.BlockSpec((D_MODEL, Z_ALL), lambda i: (0, 0))],
        out_specs=[pl.BlockSpec((tm, wd), lambda i: (i, 0)) for wd in widths],
        out_shape=[jax.ShapeDtypeStruct((n, wd), F32) for wd in widths],
        compiler_params=_cparams(("arbitrary",)),
        name="inproj",
    )(x, mods, g, w)


def _ctx_attn_kernel(zna_ref, zgq_ref, zm_ref, naq_g, nak_g, gqq_g, gqk_g, qa_g, kva_g, mqn_g, mkn_g,
                     wuq_ref, wuk_ref, wuv_ref,
                     ona_ref, ogq_ref, omla_ref, nk_ref, nv_ref, gk_ref, gv_ref, ckv_ref, kr_ref):
    zna = zna_ref[...]
    q = _seg_rms(zna[:, 0:MIX], DH, DH, naq_g[...])
    k = _seg_rms(zna[:, MIX:2 * MIX], DH, DH, nak_g[...])
    v = zna[:, 2 * MIX:3 * MIX]
    nk_ref[...] = k
    nv_ref[...] = v
    ona_ref[...] = _packed_attn(q, [(k.astype(BF16), v.astype(BF16))], NA_HEADS, DH ** -0.5)

    zgq = zgq_ref[...]
    gq = _seg_rms(zgq[:, 0:MIX], DH, DH, gqq_g[...])
    gk = _seg_rms(zgq[:, MIX:MIX + LANES], DH, DH, gqk_g[...])
    gv = zgq[:, MIX + LANES:2 * MIX]
    gk_ref[...] = gk
    gv_ref[...] = gv
    kv = [(gk.astype(BF16), gv.astype(BF16))]
    ogq_ref[...] = jnp.concatenate(
        [_packed_attn(gq[:, r * LANES:(r + 1) * LANES], kv, GQ_KV, DH ** -0.5) for r in range(2)], axis=-1)

    mq, mk, mv, ckv, col3 = _mla_qkv(zm_ref[...], qa_g[...], kva_g[...], mqn_g[...], mkn_g[...],
                                     wuq_ref[...], wuk_ref[...], wuv_ref[...])
    ckv_ref[...] = ckv
    kr_ref[...] = pltpu.roll(col3, LANES - MLA_NOPE, axis=1)[:, 0:MLA_ROPE]
    omla_ref[...] = _mla_attn(mq, [(mk.astype(BF16), mv.astype(BF16))], MLA_QK ** -0.5)


def _full(shape):
    nd = len(shape)
    return pl.BlockSpec(shape, lambda *a: (0,) * nd)


def _ctx_attn(zna, zgq, zm, lw):
    n = zna.shape[0]
    t = SEQ
    rows = lambda wd: pl.BlockSpec((t, wd), lambda b: (b, 0))
    gains = [lw["naq_g"], lw["nak_g"], lw["gqq_g"], lw["gqk_g"], lw["qa_g"], lw["kva_g"], lw["mqn_g"], lw["mkn_g"]]
    weights = [lw["wuq"], lw["wuk"], lw["wuv"]]
    out_w = (MIX, MIX, MIX, MIX, MIX, LANES, LANES, MLA_KVLORA, MLA_ROPE)
    return pl.pallas_call(
        _ctx_attn_kernel,
        grid=(n // t,),
        in_specs=[rows(Z_NA), rows(Z_GQ), rows(Z_MLA)] + [_full(a.shape) for a in gains + weights],
        out_specs=[rows(wd) for wd in out_w],
        out_shape=[jax.ShapeDtypeStruct((n, wd), F32) for wd in out_w],
        compiler_params=_cparams(("arbitrary",)),
        name="ctx_attn",
    )(zna, zgq, zm, *gains, *weights)


def _na_window_start(r):
    return jnp.clip(r - NA_WIN_R // 2, 0, GRID_ROWS - NA_WIN_R)


def _na_lat_kernel(zna_ref, kc_ref, vc_ref, bias_ref, naq_g, nak_g, o_ref, k_scr, v_scr):
    r = pl.program_id(1)

    @pl.when(r == 0)
    def _():
        k_scr[...] = _seg_rms(zna_ref[:, MIX:2 * MIX], DH, DH, nak_g[...]).astype(BF16)
        v_scr[...] = zna_ref[:, 2 * MIX:3 * MIX].astype(BF16)

    q = _seg_rms(zna_ref[pl.ds(pl.multiple_of(r * GRID_W, GRID_W), GRID_W), 0:MIX], DH, DH, naq_g[...])
    start = pl.multiple_of(_na_window_start(r) * GRID_W, GRID_W)
    span = NA_WIN_R * GRID_W
    parts = [(k_scr[pl.ds(start, span), :], v_scr[pl.ds(start, span), :]),
             (kc_ref[0].astype(BF16), vc_ref[0].astype(BF16))]
    d0 = _na_window_start(r) - r + NA_WIN_R - 1
    bias = jnp.concatenate([bias_ref[d0 + 2 * i] for i in range(NA_WIN_R // 2)], axis=-1)
    o_ref[...] = _packed_attn(q, parts, NA_HEADS, DH ** -0.5, biases=[bias, None])


def _na_lat(zna, kc, vc, bias, lw):
    nb = DEC_BATCH
    return pl.pallas_call(
        _na_lat_kernel,
        grid=(nb, GRID_ROWS),
        in_specs=[pl.BlockSpec((DEC_SEQ, Z_NA), lambda b, r: (b, 0)),
                  pl.BlockSpec((1, PAST_LEN, MIX), lambda b, r: (b, 0, 0)),
                  pl.BlockSpec((1, PAST_LEN, MIX), lambda b, r: (b, 0, 0)),
                  _full(bias.shape),
                  _full(lw["naq_g"].shape), _full(lw["nak_g"].shape)],
        out_specs=pl.BlockSpec((GRID_W, MIX), lambda b, r: (b * GRID_ROWS + r, 0)),
        out_shape=jax.ShapeDtypeStruct((nb * DEC_SEQ, MIX), F32),
        scratch_shapes=[pltpu.VMEM((DEC_SEQ, MIX), BF16), pltpu.VMEM((DEC_SEQ, MIX), BF16)],
        compiler_params=_cparams(("arbitrary", "arbitrary")),
        name="na_latent",
    )(zna, kc, vc, bias, lw["naq_g"], lw["nak_g"])


LAT_QB = 256
LAT_KEYS = DEC_SEQ + PAST_LEN


def _gq_lat_kernel(zgq_ref, kc_ref, vc_ref, cos_ref, sa_ref, sb_ref, gqq_g, gqk_g, o_ref, k_scr, v_scr):
    j = pl.program_id(1)

    @pl.when(j == 0)
    def _():
        k = _seg_rms(zgq_ref[:, MIX:MIX + LANES], DH, DH, gqk_g[...])
        k = _rope(k, cos_ref[...], sa_ref[...], sb_ref[...], DH // 4)
        k_scr[0:DEC_SEQ, :] = k.astype(BF16)
        k_scr[DEC_SEQ:LAT_KEYS, :] = kc_ref[0].astype(BF16)
        v_scr[0:DEC_SEQ, :] = zgq_ref[:, MIX + LANES:2 * MIX].astype(BF16)
        v_scr[DEC_SEQ:LAT_KEYS, :] = vc_ref[0].astype(BF16)

    rows = pl.ds(pl.multiple_of(j * LAT_QB, LAT_QB), LAT_QB)
    q = _seg_rms(zgq_ref[rows, 0:MIX], DH, DH, gqq_g[...])
    q = _rope(q, cos_ref[rows, :], sa_ref[rows, :], sb_ref[rows, :], DH // 4)
    kv = [(k_scr[...], v_scr[...])]
    o_ref[...] = jnp.concatenate(
        [_packed_attn(q[:, r * LANES:(r + 1) * LANES], kv, GQ_KV, DH ** -0.5) for r in range(2)], axis=-1)


def _gq_lat(zgq, kc, vc, tabs, lw):
    nb = DEC_BATCH
    nq = DEC_SEQ // LAT_QB
    return pl.pallas_call(
        _gq_lat_kernel,
        grid=(nb, nq),
        in_specs=[pl.BlockSpec((DEC_SEQ, Z_GQ), lambda b, j: (b, 0)),
                  pl.BlockSpec((1, PAST_LEN, LANES), lambda b, j: (b, 0, 0)),
                  pl.BlockSpec((1, PAST_LEN, LANES), lambda b, j: (b, 0, 0))]
                 + [_full(t.shape) for t in tabs] + [_full(lw["gqq_g"].shape), _full(lw["gqk_g"].shape)],
        out_specs=pl.BlockSpec((LAT_QB, MIX), lambda b, j: (b * nq + j, 0)),
        out_shape=jax.ShapeDtypeStruct((nb * DEC_SEQ, MIX), F32),
        scratch_shapes=[pltpu.VMEM((LAT_KEYS, LANES), BF16), pltpu.VMEM((LAT_KEYS, LANES), BF16)],
        compiler_params=_cparams(("arbitrary", "arbitrary")),
        name="gq_latent",
    )(zgq, kc, vc, *tabs, lw["gqq_g"], lw["gqk_g"])


def _mla_lat_kernel(zm_ref, ckvc_ref, krc_ref, cos_ref, sa_ref, sb_ref, qa_g, kva_g, mqn_g, mkn_g,
                    wuq_ref, wuk_ref, wuv_ref, o_ref, k_scr, v_scr):
    j = pl.program_id(1)
    half = MLA_ROPE // 4

    @pl.when(j == 0)
    def _():
        zm = zm_ref[...]
        ckv = _rms_rows(zm[:, 0:LANES], kva_g[...])
        col3 = zm[:, 2 * LANES:3 * LANES]
        lane = lax.broadcasted_iota(jnp.int32, col3.shape, 1)
        kr_placed = jnp.where((lane >= MLA_NOPE) & (lane < MLA_QK), col3, 0.0)
        k, v = _mla_kv(ckv, kr_placed, mkn_g[...], wuk_ref[...], wuv_ref[...])
        k = _rope(k, cos_ref[...], sa_ref[...], sb_ref[...], half)
        k_scr[0:DEC_SEQ, :] = k.astype(BF16)
        v_scr[0:DEC_SEQ, :] = v.astype(BF16)
        kc, vc = _mla_kv(ckvc_ref[0], krc_ref[0], mkn_g[...], wuk_ref[...], wuv_ref[...])
        k_scr[DEC_SEQ:LAT_KEYS, :] = kc.astype(BF16)
        v_scr[DEC_SEQ:LAT_KEYS, :] = vc.astype(BF16)

    rows = pl.ds(pl.multiple_of(j * LAT_QB, LAT_QB), LAT_QB)
    cq = _rms_rows(zm_ref[rows, LANES:3 * LANES], qa_g[...], valid=MLA_QLORA)
    q = _seg_rms(_dot(cq.astype(BF16), wuq_ref[...]), LANES, MLA_QK, mqn_g[...])
    q = _rope(q, cos_ref[rows, :], sa_ref[rows, :], sb_ref[rows, :], half)
    o_ref[...] = _mla_attn(q, [(k_scr[...], v_scr[...])], MLA_QK ** -0.5)


def _mla_lat(zm, ckvc, krc, tabs, lw):
    nb = DEC_BATCH
    nq = DEC_SEQ // LAT_QB
    small = [lw["qa_g"], lw["kva_g"], lw["mqn_g"], lw["mkn_g"], lw["wuq"], lw["wuk"], lw["wuv"]]
    return pl.pallas_call(
        _mla_lat_kernel,
        grid=(nb, nq),
        in_specs=[pl.BlockSpec((DEC_SEQ, Z_MLA), lambda b, j: (b, 0)),
                  pl.BlockSpec((1, PAST_LEN, LANES), lambda b, j: (b, 0, 0)),
                  pl.BlockSpec((1, PAST_LEN, LANES), lambda b, j: (b, 0, 0))]
                 + [_full(t.shape) for t in tabs] + [_full(a.shape) for a in small],
        out_specs=pl.BlockSpec((LAT_QB, MIX), lambda b, j: (b * nq + j, 0)),
        out_shape=jax.ShapeDtypeStruct((nb * DEC_SEQ, MIX), F32),
        scratch_shapes=[pltpu.VMEM((LAT_KEYS, MLA_HEADS * LANES), BF16), pltpu.VMEM((LAT_KEYS, MIX), BF16)],
        compiler_params=_cparams(("arbitrary", "arbitrary")),
        name="mla_latent",
    )(zm, ckvc, krc, *tabs, *small)


def _s5_kernel(nb, nch, u_ref, m_ref, ws_ref, wc_ref, a_ref, d_ref, h0_ref, y_ref, fin_ref,
               sf_scr, sb_scr, hf_scr, hb_scr):
    x = u_ref[0]
    xb = x.astype(BF16)
    s = _dot(xb, ws_ref[0])
    sf_scr[...] = s[:, 0:LANES]
    sb_scr[...] = s[:, LANES:2 * LANES]
    a = a_ref[0]
    h_f = h0_ref[0][:, 0:LANES]
    h_b = h0_ref[0][:, LANES:2 * LANES]
    for k in range(nch):
        rows_f = pl.ds(k, nb, stride=nch)
        rows_b = pl.ds(nch - 1 - k, nb, stride=nch)
        hf_scr[rows_f, :] = h_f
        hb_scr[rows_b, :] = h_b
        h_f = a[0:1] * h_f + a[1:2] * pltpu.roll(h_f, S5_N, axis=1) + sf_scr[rows_f, :]
        h_b = a[2:3] * h_b + a[3:4] * pltpu.roll(h_b, S5_N, axis=1) + sb_scr[rows_b, :]
    fin_ref[0] = jnp.concatenate([h_f, h_b], axis=-1)
    hp = jnp.concatenate([hf_scr[...], hb_scr[...]], axis=-1).astype(BF16)
    y_ref[0] = _dot(xb, m_ref[0]) + _dot(hp, wc_ref[0]) + d_ref[0] * x


def _s5_mixer(u, lw, h0, nb):
    n = u.shape[0]
    rows = n // S5_CHUNK
    nch = rows // nb
    g, c = S5_GROUPS, S5_GROUP
    ug = u.reshape(rows, S5_CHUNK, g, c).transpose(2, 0, 1, 3).reshape(g, rows, S5_ROW)
    blk = lambda *shape: pl.BlockSpec((1,) + shape, lambda i: (i, 0, 0))
    yg, fin = pl.pallas_call(
        functools.partial(_s5_kernel, nb, nch),
        grid=(g,),
        in_specs=[blk(rows, S5_ROW), blk(S5_ROW, S5_ROW), blk(S5_ROW, S5_ROW), blk(S5_ROW, S5_ROW),
                  blk(4, LANES), blk(1, S5_ROW), blk(nb, S5_ROW)],
        out_specs=[blk(rows, S5_ROW), blk(nb, S5_ROW)],
        out_shape=[jax.ShapeDtypeStruct((g, rows, S5_ROW), F32), jax.ShapeDtypeStruct((g, nb, S5_ROW), F32)],
        scratch_shapes=[pltpu.VMEM((rows, LANES), F32)] * 4,
        compiler_params=_cparams(("arbitrary",)),
        name="s5_mixer",
    )(ug, lw["s5_m"], lw["s5_ws"], lw["s5_wc"], lw["s5_a"], lw["s5_d"], h0)
    y = yg.reshape(g, rows, S5_CHUNK, c).transpose(1, 2, 0, 3).reshape(n, MIX)
    return y, fin


def _outproj_kernel(ona_ref, ys5_ref, ogq_ref, omla_ref, x_ref, mod_ref, w_ref, wglu_ref, bglu_ref, o_ref):
    y = jax.nn.gelu(ys5_ref[...], approximate=True)
    y = y * jax.nn.sigmoid(_dot(y.astype(BF16), wglu_ref[...]) + bglu_ref[...])
    mixed = _dot(ona_ref[...].astype(BF16), w_ref[0:MIX, :])
    mixed = mixed + _dot(y.astype(BF16), w_ref[MIX:2 * MIX, :])
    mixed = mixed + _dot(ogq_ref[...].astype(BF16), w_ref[2 * MIX:3 * MIX, :])
    mixed = mixed + _dot(omla_ref[...].astype(BF16), w_ref[3 * MIX:4 * MIX, :])
    o_ref[...] = x_ref[...] + mod_ref[0, 2:3, :] * mixed


def _outproj(ona, ys5, ogq, omla, x, mods, lw, mod_base, rows_per_mod):
    n = x.shape[0]
    tm = 512
    row = lambda i: (mod_base + (i * tm) // rows_per_mod, 0, 0)
    part = pl.BlockSpec((tm, MIX), lambda i: (i, 0))
    return pl.pallas_call(
        _outproj_kernel,
        grid=(n // tm,),
        in_specs=[part, part, part, part,
                  pl.BlockSpec((tm, D_MODEL), lambda i: (i, 0)),
                  pl.BlockSpec((1, 6, D_MODEL), row),
                  _full(lw["w_out"].shape), _full(lw["w_glu"].shape), _full(lw["b_glu"].shape)],
        out_specs=pl.BlockSpec((tm, D_MODEL), lambda i: (i, 0)),
        out_shape=jax.ShapeDtypeStruct((n, D_MODEL), F32),
        compiler_params=_cparams(("arbitrary",)),
        name="outproj",
    )(ona, ys5, ogq, omla, x, mods, lw["w_out"], lw["w_glu"], lw["b_glu"])


FF_ROWS = 1024


def _ffn_kernel(seq, x_ref, mod_ref, g_ref, wg_ref, wu_ref, cwg_ref, cwu_ref, cbg_ref, cbu_ref, wd_ref,
                o_ref, h_scr):
    j = pl.program_id(1)

    @pl.when(j == 0)
    def _():
        x = x_ref[...]
        y = x * lax.rsqrt(jnp.mean(x * x, axis=-1, keepdims=True) + EPS) * g_ref[...]
        h_scr[...] = (y * (1.0 + mod_ref[0, 4:5, :]) + mod_ref[0, 3:4, :]).astype(BF16)

    pos = lax.broadcasted_iota(jnp.int32, (FF_ROWS, FF_CHUNK), 0) % seq
    first = pos == 0
    last = pos == seq - 1

    def conv(u, cw_ref, cb_ref):
        prev = jnp.where(first, 0.0, pltpu.roll(u, 1, axis=0))
        nxt = jnp.where(last, 0.0, pltpu.roll(u, FF_ROWS - 1, axis=0))
        return cw_ref[0:1, :] * prev + cw_ref[1:2, :] * u + cw_ref[2:3, :] * nxt + cb_ref[...]

    h = h_scr[...]
    gate = conv(_dot(h, wg_ref[...]), cwg_ref, cbg_ref)
    up = conv(_dot(h, wu_ref[...]), cwu_ref, cbu_ref)
    act = gate * jax.nn.sigmoid(gate) * up
    contrib = _dot(act.astype(BF16), wd_ref[...])

    @pl.when(j == 0)
    def _():
        o_ref[...] = contrib

    @pl.when(j > 0)
    def _():
        o_ref[...] += contrib

    @pl.when(j == FF_STEPS - 1)
    def _():
        o_ref[...] = x_ref[...] + mod_ref[0, 5:6, :] * o_ref[...]


def _ffn(x, mods, lw, mod_base, rows_per_mod, seq):
    n = x.shape[0]
    row = lambda i, j: (mod_base + (i * FF_ROWS) // rows_per_mod, 0, 0)
    return pl.pallas_call(
        functools.partial(_ffn_kernel, seq),
        grid=(n // FF_ROWS, FF_STEPS),
        in_specs=[pl.BlockSpec((FF_ROWS, D_MODEL), lambda i, j: (i, 0)),
                  pl.BlockSpec((1, 6, D_MODEL), row),
                  pl.BlockSpec((1, D_MODEL), lambda i, j: (0, 0)),
                  pl.BlockSpec((D_MODEL, FF_CHUNK), lambda i, j: (0, j)),
                  pl.BlockSpec((D_MODEL, FF_CHUNK), lambda i, j: (0, FF_STEPS + j)),
                  pl.BlockSpec((3, FF_CHUNK), lambda i, j: (0, j)),
                  pl.BlockSpec((3, FF_CHUNK), lambda i, j: (0, FF_STEPS + j)),
                  pl.BlockSpec((1, FF_CHUNK), lambda i, j: (0, j)),
                  pl.BlockSpec((1, FF_CHUNK), lambda i, j: (0, FF_STEPS + j)),
                  pl.BlockSpec((FF_CHUNK, D_MODEL), lambda i, j: (j, 0))],
        out_specs=pl.BlockSpec((FF_ROWS, D_MODEL), lambda i, j: (i, 0)),
        out_shape=jax.ShapeDtypeStruct((n, D_MODEL), F32),
        scratch_shapes=[pltpu.VMEM((FF_ROWS, D_MODEL), BF16)],
        compiler_params=_cparams(("arbitrary", "arbitrary")),
        name="conv_ffn",
    )(x, mods, lw["norm2_g"], lw["w_up"], lw["w_up"], lw["conv_w"], lw["conv_w"],
      lw["conv_b"], lw["conv_b"], lw["w_down"])


def _rope_tables():
    pos = np.arange(DEC_SEQ)

    def ang(p, half):
        inv = ROPE_BASE ** (-np.arange(half, dtype=np.float64) / half)
        a = p.astype(np.float64)[:, None] * inv[None, :]
        return np.concatenate([a, a], axis=-1)

    def tables(dim, lanes_before, lanes_after, reps):
        a = np.concatenate([ang(pos // GRID_W, dim // 4), ang(pos % GRID_W, dim // 4)], axis=-1)
        half = dim // 4
        first = (np.arange(dim) % (2 * half)) < half
        cos = np.cos(a)
        sin_a = np.where(first[None, :], -np.sin(a), 0.0)
        sin_b = np.where(first[None, :], 0.0, np.sin(a))

        def place(t, fill):
            t = np.concatenate([np.full((DEC_SEQ, lanes_before), fill), t,
                                np.full((DEC_SEQ, lanes_after), fill)], axis=-1)
            return jnp.asarray(np.tile(t, (1, reps)), F32)

        return place(cos, 1.0), place(sin_a, 0.0), place(sin_b, 0.0)

    gq = tables(DH, 0, 0, LANES // DH)
    mla = tables(MLA_ROPE, MLA_NOPE, LANES - MLA_QK, 1)
    return gq, mla


def _na_bias_tables(rpb):
    qc = np.arange(GRID_W)[:, None]
    kc = np.arange(GRID_W)[None, :]
    col_start = np.clip(qc - NA_WIN_C // 2, 0, GRID_W - NA_WIN_C)
    valid = (kc >= col_start) & (kc < col_start + NA_WIN_C)
    d_c = np.clip(kc - qc, 1 - NA_WIN_C, NA_WIN_C - 1) + NA_WIN_C - 1
    nrel = 2 * NA_WIN_C - 1
    onehot = jnp.asarray((d_c.reshape(-1)[None, :] == np.arange(nrel)[:, None]).astype(np.float32))
    t = jnp.dot(rpb.astype(F32).reshape(-1, nrel), onehot, precision=lax.Precision.HIGHEST)
    t = t.reshape(NA_HEADS, 2 * NA_WIN_R - 1, GRID_W, GRID_W)
    t = jnp.where(jnp.asarray(valid)[None, None], t, NEG)
    t = t.transpose(1, 0, 2, 3).reshape(2 * NA_WIN_R - 1, NA_HEADS * GRID_W, GRID_W)
    return jnp.concatenate([t[:-1], t[1:]], axis=-1)


def _s5_matrices(lam_re, lam_im, log_dt, b_re, b_im, c_re, c_im):
    hi = lax.Precision.HIGHEST
    c = S5_CHUNK
    s_idx, t_idx, j_idx = np.arange(c)[:, None, None], np.arange(c)[None, :, None], np.arange(c)[None, None, :]
    m = 0.0
    ws, wc, av = [], [], []
    for d in range(2):
        dt = jnp.exp(log_dt[d].astype(F32))[:, None]
        lr, li = lam_re[d].astype(F32), lam_im[d].astype(F32)
        mag = jnp.exp(lr * dt)
        a_re, a_im = mag * jnp.cos(li * dt), mag * jnp.sin(li * dt)
        den = lr * lr + li * li
        f_re = ((a_re - 1.0) * lr + a_im * li) / den
        f_im = (a_im * lr - (a_re - 1.0) * li) / den
        br, bi = b_re[d].astype(F32), b_im[d].astype(F32)
        bb_re = f_re[..., None] * br - f_im[..., None] * bi
        bb_im = f_re[..., None] * bi + f_im[..., None] * br
        p_re, p_im = [jnp.ones_like(a_re)], [jnp.zeros_like(a_re)]
        for _ in range(c):
            p_re, p_im = (p_re + [p_re[-1] * a_re - p_im[-1] * a_im],
                          p_im + [p_re[-1] * a_im + p_im[-1] * a_re])
        p_re, p_im = jnp.stack(p_re), jnp.stack(p_im)
        cr, ci = c_re[d].astype(F32), c_im[d].astype(F32)
        q_re = cr[None] * p_re[:, :, None, :] - ci[None] * p_im[:, :, None, :]
        q_im = cr[None] * p_im[:, :, None, :] + ci[None] * p_re[:, :, None, :]
        kern = (jnp.einsum("jgon,gnc->jgoc", q_re[:c], bb_re, precision=hi)
                - jnp.einsum("jgon,gnc->jgoc", q_im[:c], bb_im, precision=hi))
        lag = (t_idx - s_idx) if d == 0 else (s_idx - t_idx)
        sel = jnp.asarray((lag == j_idx).astype(np.float32))
        m = m + jnp.einsum("stj,jgoc->gscto", sel, kern, precision=hi).reshape(S5_GROUPS, S5_ROW, S5_ROW)
        pw = np.arange(c)[::-1] if d == 0 else np.arange(c)
        w_re = p_re[pw][:, :, :, None] * bb_re[None] - p_im[pw][:, :, :, None] * bb_im[None]
        w_im = p_re[pw][:, :, :, None] * bb_im[None] + p_im[pw][:, :, :, None] * bb_re[None]
        ws += [w.transpose(1, 0, 3, 2).reshape(S5_GROUPS, S5_ROW, S5_N) for w in (w_re, w_im)]
        pw = np.arange(1, c + 1) if d == 0 else np.arange(c, 0, -1)
        wc += [w.transpose(1, 3, 0, 2).reshape(S5_GROUPS, S5_N, S5_ROW) for w in (q_re[pw], -q_im[pw])]
        av += [jnp.concatenate([p_re[c], p_re[c]], axis=-1), jnp.concatenate([-p_im[c], p_im[c]], axis=-1)]
    return (m.astype(BF16), jnp.concatenate(ws, axis=-1).astype(BF16), jnp.concatenate(wc, axis=1).astype(BF16),
            jnp.stack(av, axis=1))


def _layer_weights(l, p):
    w = p["w_in"][l]
    q0 = Z_NA + Z_S5
    gq = lambda h: w[:, q0 + h * DH:q0 + (h + 1) * DH]
    m0 = q0 + 2 * MIX
    w_in = jnp.concatenate(
        [w[:, 0:q0], gq(0), gq(2), gq(1), gq(3), w[:, q0 + MIX:m0],
         w[:, m0 + MLA_QLORA:m0 + MLA_QLORA + MLA_KVLORA], w[:, m0:m0 + MLA_QLORA],
         w[:, m0 + MLA_QLORA + MLA_KVLORA:], jnp.zeros((D_MODEL, Z_ALL - w.shape[1]), w.dtype)],
        axis=1).astype(BF16)
    wo = p["w_out"][l]
    og = lambda h: wo[2 * MIX + h * DH:2 * MIX + (h + 1) * DH]
    w_out = jnp.concatenate([wo[0:2 * MIX], og(0), og(2), og(1), og(3), wo[3 * MIX:]], axis=0).astype(BF16)
    pad_head = lambda g: jnp.tile(jnp.pad(g, (0, LANES - MLA_QK)), MLA_HEADS)[None, :]
    wuq = jnp.pad(p["mla_w_uq"][l].reshape(MLA_QLORA, MLA_HEADS, MLA_QK),
                  ((0, 2 * LANES - MLA_QLORA), (0, 0), (0, LANES - MLA_QK))).reshape(2 * LANES, MLA_HEADS * LANES)
    wukv = p["mla_w_ukv"][l].reshape(MLA_KVLORA, MLA_HEADS, MLA_NOPE + DH)
    wuk = jnp.pad(wukv[:, :, :MLA_NOPE], ((0, 0), (0, 0), (0, LANES - MLA_NOPE))).reshape(MLA_KVLORA, MLA_HEADS * LANES)
    wuv = wukv[:, :, MLA_NOPE:].reshape(MLA_KVLORA, MIX)
    s5_m, s5_ws, s5_wc, s5_a = _s5_matrices(p["s5_lam_re"][l], p["s5_lam_im"][l], p["s5_log_dt"][l],
                                       p["s5_b_re"][l], p["s5_b_im"][l], p["s5_c_re"][l], p["s5_c_im"][l])
    return dict(
        norm1_g=p["norm1_g"][l][None, :], norm2_g=p["norm2_g"][l][None, :],
        w_in=w_in, w_out=w_out,
        naq_g=jnp.tile(p["na_qn"][l], NA_HEADS)[None, :], nak_g=jnp.tile(p["na_kn"][l], NA_HEADS)[None, :],
        gqq_g=jnp.tile(p["gq_qn"][l], GQ_HEADS)[None, :], gqk_g=jnp.tile(p["gq_kn"][l], GQ_KV)[None, :],
        qa_g=jnp.pad(p["mla_qa_g"][l], (0, 2 * LANES - MLA_QLORA))[None, :], kva_g=p["mla_kva_g"][l][None, :],
        mqn_g=pad_head(p["mla_qn"][l]), mkn_g=pad_head(p["mla_kn"][l]),
        wuq=wuq.astype(BF16), wuk=wuk.astype(BF16), wuv=wuv.astype(BF16),
        na_bias=_na_bias_tables(p["na_rpb"][l]),
        s5_m=s5_m, s5_ws=s5_ws, s5_wc=s5_wc, s5_a=s5_a,
        s5_d=jnp.tile(p["s5_d"][l].reshape(S5_GROUPS, 1, S5_GROUP), (1, 1, S5_CHUNK)),
        w_glu=p["s5_w_glu"][l].astype(BF16), b_glu=p["s5_b_glu"][l][None, :],
        w_up=p["ffn_w_up"][l].astype(BF16), conv_w=p["ffn_conv_w"][l], conv_b=p["ffn_conv_b"][l][None, :],
        w_down=p["ffn_w_down"][l].astype(BF16),
    )


def _heads_first(x, nb, t, heads):
    return x.reshape(nb, t, heads, DH).transpose(0, 2, 1, 3)


def _lanes_packed(x):
    b, h, pl_, dh = x.shape
    return x.transpose(0, 2, 1, 3).reshape(b, pl_, h * dh)


def kernel(x_prompt, x_sample, cache_na_k, cache_na_v, state_s5, cache_gqa_k, cache_gqa_v, cache_mla_ckv,
           cache_mla_krope, c, c_ctx, norm1_g, norm2_g, ada_w, ada_b, w_in, na_qn, na_kn, na_rpb, s5_lam_re,
           s5_lam_im, s5_log_dt, s5_b_re, s5_b_im, s5_c_re, s5_c_im, s5_d, s5_w_glu, s5_b_glu, gq_qn, gq_kn,
           mla_qa_g, mla_kva_g, mla_w_uq, mla_w_ukv, mla_qn, mla_kn, w_out, ffn_w_up, ffn_conv_w, ffn_conv_b,
           ffn_w_down):
    p = dict(norm1_g=norm1_g, norm2_g=norm2_g, w_in=w_in, na_qn=na_qn, na_kn=na_kn, na_rpb=na_rpb,
             s5_lam_re=s5_lam_re, s5_lam_im=s5_lam_im, s5_log_dt=s5_log_dt, s5_b_re=s5_b_re, s5_b_im=s5_b_im,
             s5_c_re=s5_c_re, s5_c_im=s5_c_im, s5_d=s5_d, s5_w_glu=s5_w_glu, s5_b_glu=s5_b_glu,
             gq_qn=gq_qn, gq_kn=gq_kn, mla_qa_g=mla_qa_g, mla_kva_g=mla_kva_g, mla_w_uq=mla_w_uq,
             mla_w_ukv=mla_w_ukv, mla_qn=mla_qn, mla_kn=mla_kn, w_out=w_out, ffn_w_up=ffn_w_up,
             ffn_conv_w=ffn_conv_w, ffn_conv_b=ffn_conv_b, ffn_w_down=ffn_w_down)
    nb, nd = BATCH, DEC_BATCH
    cvec = jnp.concatenate([c_ctx[None, :], c, jnp.zeros((8 - 1 - nd, D_MODEL), F32)], axis=0)
    mods_all = _ada_mods(cvec, ada_w, ada_b).reshape(DEPTH, 8, 6, D_MODEL)
    gq_tabs, mla_tabs = _rope_tables()

    xc = x_prompt.reshape(nb * SEQ, D_MODEL)
    xl = x_sample.reshape(nd * DEC_SEQ, D_MODEL)
    zero_state = jnp.zeros((S5_GROUPS, nb, 4 * S5_N), F32)
    caches = []
    for l in range(DEPTH):
        lw = _layer_weights(l, p)
        mods = mods_all[l]

        zna, zs5, zgq, zm = _inproj(xc, mods, lw["norm1_g"], lw["w_in"], 0, nb * SEQ)
        ona, ogq, omla, nk, nv, gk, gv, ckv, kr = _ctx_attn(zna, zgq, zm, lw)
        ys5, fin = _s5_mixer(zs5, lw, zero_state, nb)
        xc = _outproj(ona, ys5, ogq, omla, xc, mods, lw, 0, nb * SEQ)
        xc = _ffn(xc, mods, lw, 0, nb * SEQ, SEQ)
        caches.append((_heads_first(nk, nb, SEQ, NA_HEADS), _heads_first(nv, nb, SEQ, NA_HEADS),
                       fin.reshape(S5_GROUPS, nb, 2, 2, S5_N).transpose(1, 2, 3, 0, 4),
                       _heads_first(gk, nb, SEQ, GQ_KV), _heads_first(gv, nb, SEQ, GQ_KV),
                       ckv.reshape(nb, SEQ, MLA_KVLORA), kr.reshape(nb, SEQ, MLA_ROPE)))

        zna, zs5, zgq, zm = _inproj(xl, mods, lw["norm1_g"], lw["w_in"], 1, DEC_SEQ)
        ona = _na_lat(zna, _lanes_packed(cache_na_k[:, l]), _lanes_packed(cache_na_v[:, l]), lw["na_bias"], lw)
        ogq = _gq_lat(zgq, _lanes_packed(cache_gqa_k[:, l]), _lanes_packed(cache_gqa_v[:, l]), gq_tabs, lw)
        krc = jnp.pad(cache_mla_krope[:, l], ((0, 0), (0, 0), (MLA_NOPE, LANES - MLA_QK)))
        omla = _mla_lat(zm, cache_mla_ckv[:, l], krc, mla_tabs, lw)
        h0 = state_s5[:, l].astype(F32).transpose(3, 0, 1, 2, 4).reshape(S5_GROUPS, nd, 4 * S5_N)
        ys5, _ = _s5_mixer(zs5, lw, h0, nd)
        xl = _outproj(ona, ys5, ogq, omla, xl, mods, lw, 1, DEC_SEQ)
        xl = _ffn(xl, mods, lw, 1, DEC_SEQ, DEC_SEQ)

    stack = lambda i: jnp.stack([cl[i] for cl in caches], axis=1)
    return (xc.reshape(nb, SEQ, D_MODEL), xl.reshape(nd, DEC_SEQ, D_MODEL),
            stack(0), stack(1), stack(2), stack(3), stack(4), stack(5), stack(6))
```

```python
import functools
import math

import numpy as np
import jax
import jax.numpy as jnp
from jax import lax
from jax.experimental import pallas as pl
from jax.experimental.pallas import tpu as pltpu

F32 = jnp.float32
BF16 = jnp.bfloat16

D_MODEL = 1024
BATCH = 32
SEQ = 256
DEPTH = 2
DEC_BATCH = 4
DEC_SEQ = 1024
PAST_LEN = 512
GRID_W = 64
GRID_ROWS = DEC_SEQ // GRID_W
MIX = D_MODEL // 4
DH = 64
NA_HEADS = MIX // DH
NA_WIN_R = 8
NA_WIN_C = 16
S5_GROUP = 16
S5_GROUPS = MIX // S5_GROUP
S5_N = 64
GQ_HEADS = MIX // DH
GQ_KV = GQ_HEADS // 2
MLA_HEADS = MIX // DH
MLA_NOPE = 64
MLA_ROPE = 32
MLA_QK = MLA_NOPE + MLA_ROPE
MLA_QLORA = (3 * D_MODEL) // 16
MLA_KVLORA = D_MODEL // 8
D_FF = 128 * ((8 * D_MODEL // 3 + 127) // 128)
ROPE_BASE = 10000.0
EPS = 1e-6
NEG = -1e30

LANES = 128
MXU_DIM = 256
VMEM_LIMIT = 48 * 1024 * 1024

Z_NA = 3 * MIX
Z_S5 = MIX
Z_GQ = 2 * MIX
Z_MLA = 3 * LANES
Z_ALL = Z_NA + Z_S5 + Z_GQ + Z_MLA

S5_CHUNK = 16
S5_ROW = S5_CHUNK * S5_GROUP

FF_CHUNK = 256
FF_STEPS = D_FF // FF_CHUNK


def _cparams(sem):
    return pltpu.CompilerParams(dimension_semantics=sem, vmem_limit_bytes=VMEM_LIMIT)


def _dot(a, b):
    return jnp.dot(a, b, preferred_element_type=F32)


def _dot_nt(a, b):
    return lax.dot_general(a, b, (((1,), (1,)), ((), ())), preferred_element_type=F32)


def _rms_rows(x, gain, denom=None, valid=None):
    xx = x * x
    if valid is not None and valid != x.shape[-1]:
        lane = lax.broadcasted_iota(jnp.int32, x.shape, 1)
        xx = jnp.where(lane < valid, xx, 0.0)
    denom = denom or (valid or x.shape[-1])
    ss = jnp.sum(xx, axis=-1, keepdims=True)
    return x * lax.rsqrt(ss / denom + EPS) * gain


def _seg_rms(x, seg, denom, gain):
    rows, width = x.shape
    if seg % LANES == 0:
        parts = []
        for s in range(width // seg):
            xs = x[:, s * seg:(s + 1) * seg]
            ss = jnp.sum(xs * xs, axis=-1, keepdims=True)
            parts.append(xs * lax.rsqrt(ss / denom + EPS))
        return jnp.concatenate(parts, axis=-1) * gain
    seg_id = lax.broadcasted_iota(jnp.int32, x.shape, 1) // seg
    xx = x * x
    scale = jnp.zeros_like(x)
    for s in range(width // seg):
        m = seg_id == s
        ss = jnp.sum(jnp.where(m, xx, 0.0), axis=-1, keepdims=True)
        scale = jnp.where(m, lax.rsqrt(ss / denom + EPS), scale)
    return x * scale * gain


def _rope(x, cos, sin_a, sin_b, half):
    tiles = []
    for t in range(x.shape[-1] // LANES):
        xt = x[:, t * LANES:(t + 1) * LANES]
        up = pltpu.roll(xt, LANES - half, axis=1)
        dn = pltpu.roll(xt, half, axis=1)
        tiles.append(xt * cos + up * sin_a + dn * sin_b)
    return tiles[0] if len(tiles) == 1 else jnp.concatenate(tiles, axis=-1)


def _packed_attn(q, parts, nseg, scale, biases=None):
    tq, width = q.shape
    seg_id = lax.broadcasted_iota(jnp.int32, q.shape, 1) // DH
    lhs = jnp.concatenate([jnp.where(seg_id == h, q, 0.0) for h in range(nseg)], axis=0).astype(BF16)
    scores = []
    for i, (kb, _) in enumerate(parts):
        s = _dot_nt(lhs, kb) * scale
        if biases is not None and biases[i] is not None:
            s = s + biases[i]
        scores.append(s)
    m = scores[0].max(axis=-1, keepdims=True)
    for s in scores[1:]:
        m = jnp.maximum(m, s.max(axis=-1, keepdims=True))
    den = None
    pv = None
    for s, (_, vb) in zip(scores, parts):
        p = jnp.exp(s - m)
        ps = p.sum(axis=-1, keepdims=True)
        den = ps if den is None else den + ps
        c = _dot(p.astype(BF16), vb)
        pv = c if pv is None else pv + c
    pv = pv / den
    out = jnp.zeros((tq, width), F32)
    for h in range(nseg):
        out = out + jnp.where(seg_id == h, pv[h * tq:(h + 1) * tq], 0.0)
    return out


def _mla_attn(q, parts, scale):
    tq = q.shape[0]
    vseg = None
    out = jnp.zeros((tq, MIX), F32)
    for h in range(MLA_HEADS):
        qh = q[:, h * LANES:(h + 1) * LANES].astype(BF16)
        scores = [_dot_nt(qh, kb[:, h * LANES:(h + 1) * LANES]) * scale for kb, _ in parts]
        m = scores[0].max(axis=-1, keepdims=True)
        for s in scores[1:]:
            m = jnp.maximum(m, s.max(axis=-1, keepdims=True))
        den = None
        pv = None
        for s, (_, vb) in zip(scores, parts):
            p = jnp.exp(s - m)
            ps = p.sum(axis=-1, keepdims=True)
            den = ps if den is None else den + ps
            vseg = lax.broadcasted_iota(jnp.int32, vb.shape, 1) // DH
            c = _dot(p.astype(BF16), jnp.where(vseg == h, vb, jnp.zeros_like(vb)))
            pv = c if pv is None else pv + c
        out = out + pv / den
    return out


def _mla_qkv(zm, qa_g, kva_g, mqn_g, mkn_g, wuq, wuk, wuv):
    ckv = _rms_rows(zm[:, 0:LANES], kva_g)
    col3 = zm[:, 2 * LANES:3 * LANES]
    lane = lax.broadcasted_iota(jnp.int32, col3.shape, 1)
    kr_placed = jnp.where((lane >= MLA_NOPE) & (lane < MLA_QK), col3, 0.0)
    cq = _rms_rows(zm[:, LANES:3 * LANES], qa_g, valid=MLA_QLORA)
    q = _seg_rms(_dot(cq.astype(BF16), wuq), LANES, MLA_QK, mqn_g)
    k, v = _mla_kv(ckv, kr_placed, mkn_g, wuk, wuv)
    return q, k, v, ckv, col3


def _mla_kv(ckv, kr_placed, mkn_g, wuk, wuv):
    cb = ckv.astype(BF16)
    kf = _dot(cb, wuk) + jnp.concatenate([kr_placed] * MLA_HEADS, axis=-1)
    return _seg_rms(kf, LANES, MLA_QK, mkn_g), _dot(cb, wuv)


def _ada_kernel(c_ref, w_ref, b_ref, o_ref):
    cv = c_ref[...]
    s = cv * jax.nn.sigmoid(cv)
    o_ref[0] = _dot(s.astype(BF16), w_ref[0].astype(BF16)) + b_ref[0]


def _ada_mods(cvec, ada_w, ada_b):
    tn = 1536
    n = ada_w.shape[-1]
    return pl.pallas_call(
        _ada_kernel,
        grid=(DEPTH, n // tn),
        in_specs=[pl.BlockSpec((8, D_MODEL), lambda l, j: (0, 0)),
                  pl.BlockSpec((1, D_MODEL, tn), lambda l, j: (l, 0, j)),
                  pl.BlockSpec((1, 1, tn), lambda l, j: (l, 0, j))],
        out_specs=pl.BlockSpec((1, 8, tn), lambda l, j: (l, 0, j)),
        out_shape=jax.ShapeDtypeStruct((DEPTH, 8, n), F32),
        compiler_params=_cparams(("arbitrary", "arbitrary")),
        name="ada_mods",
    )(cvec, ada_w, ada_b.reshape(DEPTH, 1, n))


def _inproj_kernel(x_ref, mod_ref, g_ref, w_ref, ona_ref, os5_ref, ogq_ref, omla_ref):
    x = x_ref[...]
    y = x * lax.rsqrt(jnp.mean(x * x, axis=-1, keepdims=True) + EPS) * g_ref[...]
    h = y * (1.0 + mod_ref[0, 1:2, :]) + mod_ref[0, 0:1, :]
    z = _dot(h.astype(BF16), w_ref[...])
    ona_ref[...] = z[:, 0:Z_NA]
    os5_ref[...] = z[:, Z_NA:Z_NA + Z_S5]
    ogq_ref[...] = z[:, Z_NA + Z_S5:Z_NA + Z_S5 + Z_GQ]
    omla_ref[...] = z[:, Z_NA + Z_S5 + Z_GQ:Z_ALL]


def _inproj(x, mods, g, w, mod_base, rows_per_mod):
    n = x.shape[0]
    tm = 512
    row = lambda i: (mod_base + (i * tm) // rows_per_mod, 0, 0)
    widths = (Z_NA, Z_S5, Z_GQ, Z_MLA)
    return pl.pallas_call(
        _inproj_kernel,
        grid=(n // tm,),
        in_specs=[pl.BlockSpec((tm, D_MODEL), lambda i: (i, 0)),
                  pl.BlockSpec((1, 6, D_MODEL), row),
                  pl.BlockSpec((1, D_MODEL), lambda i: (0, 0)),
                  pl.BlockSpec((D_MODEL, Z_ALL), lambda i: (0, 0))],
        out_specs=[pl.BlockSpec((tm, wd), lambda i: (i, 0)) for wd in widths],
        out_shape=[jax.ShapeDtypeStruct((n, wd), F32) for wd in widths],
        compiler_params=_cparams(("arbitrary",)),
        name="inproj",
    )(x, mods, g, w)


def _ctx_attn_kernel(zna_ref, zgq_ref, zm_ref, naq_g, nak_g, gqq_g, gqk_g, qa_g, kva_g, mqn_g, mkn_g,
                     wuq_ref, wuk_ref, wuv_ref,
                     ona_ref, ogq_ref, omla_ref, nk_ref, nv_ref, gk_ref, gv_ref, ckv_ref, kr_ref):
    zna = zna_ref[...]
    q = _seg_rms(zna[:, 0:MIX], DH, DH, naq_g[...])
    k = _seg_rms(zna[:, MIX:2 * MIX], DH, DH, nak_g[...])
    v = zna[:, 2 * MIX:3 * MIX]
    nk_ref[...] = k
    nv_ref[...] = v
    ona_ref[...] = _packed_attn(q, [(k.astype(BF16), v.astype(BF16))], NA_HEADS, DH ** -0.5)

    zgq = zgq_ref[...]
    gq = _seg_rms(zgq[:, 0:MIX], DH, DH, gqq_g[...])
    gk = _seg_rms(zgq[:, MIX:MIX + LANES], DH, DH, gqk_g[...])
    gv = zgq[:, MIX + LANES:2 * MIX]
    gk_ref[...] = gk
    gv_ref[...] = gv
    kv = [(gk.astype(BF16), gv.astype(BF16))]
    ogq_ref[...] = jnp.concatenate(
        [_packed_attn(gq[:, r * LANES:(r + 1) * LANES], kv, GQ_KV, DH ** -0.5) for r in range(2)], axis=-1)

    mq, mk, mv, ckv, col3 = _mla_qkv(zm_ref[...], qa_g[...], kva_g[...], mqn_g[...], mkn_g[...],
                                     wuq_ref[...], wuk_ref[...], wuv_ref[...])
    ckv_ref[...] = ckv
    kr_ref[...] = pltpu.roll(col3, LANES - MLA_NOPE, axis=1)[:, 0:MLA_ROPE]
    omla_ref[...] = _mla_attn(mq, [(mk.astype(BF16), mv.astype(BF16))], MLA_QK ** -0.5)


def _full(shape):
    nd = len(shape)
    return pl.BlockSpec(shape, lambda *a: (0,) * nd)


def _ctx_attn(zna, zgq, zm, lw):
    n = zna.shape[0]
    t = SEQ
    rows = lambda wd: pl.BlockSpec((t, wd), lambda b: (b, 0))
    gains = [lw["naq_g"], lw["nak_g"], lw["gqq_g"], lw["gqk_g"], lw["qa_g"], lw["kva_g"], lw["mqn_g"], lw["mkn_g"]]
    weights = [lw["wuq"], lw["wuk"], lw["wuv"]]
    out_w = (MIX, MIX, MIX, MIX, MIX, LANES, LANES, MLA_KVLORA, MLA_ROPE)
    return pl.pallas_call(
        _ctx_attn_kernel,
        grid=(n // t,),
        in_specs=[rows(Z_NA), rows(Z_GQ), rows(Z_MLA)] + [_full(a.shape) for a in gains + weights],
        out_specs=[rows(wd) for wd in out_w],
        out_shape=[jax.ShapeDtypeStruct((n, wd), F32) for wd in out_w],
        compiler_params=_cparams(("arbitrary",)),
        name="ctx_attn",
    )(zna, zgq, zm, *gains, *weights)


def _na_window_start(r):
    return jnp.clip(r - NA_WIN_R // 2, 0, GRID_ROWS - NA_WIN_R)


def _na_lat_kernel(zna_ref, kc_ref, vc_ref, bias_ref, naq_g, nak_g, o_ref, k_scr, v_scr):
    r = pl.program_id(1)

    @pl.when(r == 0)
    def _():
        k_scr[...] = _seg_rms(zna_ref[:, MIX:2 * MIX], DH, DH, nak_g[...]).astype(BF16)
        v_scr[...] = zna_ref[:, 2 * MIX:3 * MIX].astype(BF16)

    q = _seg_rms(zna_ref[pl.ds(pl.multiple_of(r * GRID_W, GRID_W), GRID_W), 0:MIX], DH, DH, naq_g[...])
    start = pl.multiple_of(_na_window_start(r) * GRID_W, GRID_W)
    span = NA_WIN_R * GRID_W
    parts = [(k_scr[pl.ds(start, span), :], v_scr[pl.ds(start, span), :]),
             (kc_ref[0].astype(BF16), vc_ref[0].astype(BF16))]
    d0 = _na_window_start(r) - r + NA_WIN_R - 1
    bias = jnp.concatenate([bias_ref[d0 + 2 * i] for i in range(NA_WIN_R // 2)], axis=-1)
    o_ref[...] = _packed_attn(q, parts, NA_HEADS, DH ** -0.5, biases=[bias, None])


def _na_lat(zna, kc, vc, bias, lw):
    nb = DEC_BATCH
    return pl.pallas_call(
        _na_lat_kernel,
        grid=(nb, GRID_ROWS),
        in_specs=[pl.BlockSpec((DEC_SEQ, Z_NA), lambda b, r: (b, 0)),
                  pl.BlockSpec((1, PAST_LEN, MIX), lambda b, r: (b, 0, 0)),
                  pl.BlockSpec((1, PAST_LEN, MIX), lambda b, r: (b, 0, 0)),
                  _full(bias.shape),
                  _full(lw["naq_g"].shape), _full(lw["nak_g"].shape)],
        out_specs=pl.BlockSpec((GRID_W, MIX), lambda b, r: (b * GRID_ROWS + r, 0)),
        out_shape=jax.ShapeDtypeStruct((nb * DEC_SEQ, MIX), F32),
        scratch_shapes=[pltpu.VMEM((DEC_SEQ, MIX), BF16), pltpu.VMEM((DEC_SEQ, MIX), BF16)],
        compiler_params=_cparams(("arbitrary", "arbitrary")),
        name="na_latent",
    )(zna, kc, vc, bias, lw["naq_g"], lw["nak_g"])


LAT_QB = 256
LAT_KEYS = DEC_SEQ + PAST_LEN


def _gq_lat_kernel(zgq_ref, kc_ref, vc_ref, cos_ref, sa_ref, sb_ref, gqq_g, gqk_g, o_ref, k_scr, v_scr):
    j = pl.program_id(1)

    @pl.when(j == 0)
    def _():
        k = _seg_rms(zgq_ref[:, MIX:MIX + LANES], DH, DH, gqk_g[...])
        k = _rope(k, cos_ref[...], sa_ref[...], sb_ref[...], DH // 4)
        k_scr[0:DEC_SEQ, :] = k.astype(BF16)
        k_scr[DEC_SEQ:LAT_KEYS, :] = kc_ref[0].astype(BF16)
        v_scr[0:DEC_SEQ, :] = zgq_ref[:, MIX + LANES:2 * MIX].astype(BF16)
        v_scr[DEC_SEQ:LAT_KEYS, :] = vc_ref[0].astype(BF16)

    rows = pl.ds(pl.multiple_of(j * LAT_QB, LAT_QB), LAT_QB)
    q = _seg_rms(zgq_ref[rows, 0:MIX], DH, DH, gqq_g[...])
    q = _rope(q, cos_ref[rows, :], sa_ref[rows, :], sb_ref[rows, :], DH // 4)
    kv = [(k_scr[...], v_scr[...])]
    o_ref[...] = jnp.concatenate(
        [_packed_attn(q[:, r * LANES:(r + 1) * LANES], kv, GQ_KV, DH ** -0.5) for r in range(2)], axis=-1)


def _gq_lat(zgq, kc, vc, tabs, lw):
    nb = DEC_BATCH
    nq = DEC_SEQ // LAT_QB
    return pl.pallas_call(
        _gq_lat_kernel,
        grid=(nb, nq),
        in_specs=[pl.BlockSpec((DEC_SEQ, Z_GQ), lambda b, j: (b, 0)),
                  pl.BlockSpec((1, PAST_LEN, LANES), lambda b, j: (b, 0, 0)),
                  pl.BlockSpec((1, PAST_LEN, LANES), lambda b, j: (b, 0, 0))]
                 + [_full(t.shape) for t in tabs] + [_full(lw["gqq_g"].shape), _full(lw["gqk_g"].shape)],
        out_specs=pl.BlockSpec((LAT_QB, MIX), lambda b, j: (b * nq + j, 0)),
        out_shape=jax.ShapeDtypeStruct((nb * DEC_SEQ, MIX), F32),
        scratch_shapes=[pltpu.VMEM((LAT_KEYS, LANES), BF16), pltpu.VMEM((LAT_KEYS, LANES), BF16)],
        compiler_params=_cparams(("arbitrary", "arbitrary")),
        name="gq_latent",
    )(zgq, kc, vc, *tabs, lw["gqq_g"], lw["gqk_g"])


def _mla_lat_kernel(zm_ref, ckvc_ref, krc_ref, cos_ref, sa_ref, sb_ref, qa_g, kva_g, mqn_g, mkn_g,
                    wuq_ref, wuk_ref, wuv_ref, o_ref, k_scr, v_scr):
    j = pl.program_id(1)
    half = MLA_ROPE // 4

    @pl.when(j == 0)
    def _():
        zm = zm_ref[...]
        ckv = _rms_rows(zm[:, 0:LANES], kva_g[...])
        col3 = zm[:, 2 * LANES:3 * LANES]
        lane = lax.broadcasted_iota(jnp.int32, col3.shape, 1)
        kr_placed = jnp.where((lane >= MLA_NOPE) & (lane < MLA_QK), col3, 0.0)
        k, v = _mla_kv(ckv, kr_placed, mkn_g[...], wuk_ref[...], wuv_ref[...])
        k = _rope(k, cos_ref[...], sa_ref[...], sb_ref[...], half)
        k_scr[0:DEC_SEQ, :] = k.astype(BF16)
        v_scr[0:DEC_SEQ, :] = v.astype(BF16)
        kc, vc = _mla_kv(ckvc_ref[0], krc_ref[0], mkn_g[...], wuk_ref[...], wuv_ref[...])
        k_scr[DEC_SEQ:LAT_KEYS, :] = kc.astype(BF16)
        v_scr[DEC_SEQ:LAT_KEYS, :] = vc.astype(BF16)

    rows = pl.ds(pl.multiple_of(j * LAT_QB, LAT_QB), LAT_QB)
    cq = _rms_rows(zm_ref[rows, LANES:3 * LANES], qa_g[...], valid=MLA_QLORA)
    q = _seg_rms(_dot(cq.astype(BF16), wuq_ref[...]), LANES, MLA_QK, mqn_g[...])
    q = _rope(q, cos_ref[rows, :], sa_ref[rows, :], sb_ref[rows, :], half)
    o_ref[...] = _mla_attn(q, [(k_scr[...], v_scr[...])], MLA_QK ** -0.5)


def _mla_lat(zm, ckvc, krc, tabs, lw):
    nb = DEC_BATCH
    nq = DEC_SEQ // LAT_QB
    small = [lw["qa_g"], lw["kva_g"], lw["mqn_g"], lw["mkn_g"], lw["wuq"], lw["wuk"], lw["wuv"]]
    return pl.pallas_call(
        _mla_lat_kernel,
        grid=(nb, nq),
        in_specs=[pl.BlockSpec((DEC_SEQ, Z_MLA), lambda b, j: (b, 0)),
                  pl.BlockSpec((1, PAST_LEN, LANES), lambda b, j: (b, 0, 0)),
                  pl.BlockSpec((1, PAST_LEN, LANES), lambda b, j: (b, 0, 0))]
                 + [_full(t.shape) for t in tabs] + [_full(a.shape) for a in small],
        out_specs=pl.BlockSpec((LAT_QB, MIX), lambda b, j: (b * nq + j, 0)),
        out_shape=jax.ShapeDtypeStruct((nb * DEC_SEQ, MIX), F32),
        scratch_shapes=[pltpu.VMEM((LAT_KEYS, MLA_HEADS * LANES), BF16), pltpu.VMEM((LAT_KEYS, MIX), BF16)],
        compiler_params=_cparams(("arbitrary", "arbitrary")),
        name="mla_latent",
    )(zm, ckvc, krc, *tabs, *small)


def _s5_kernel(nb, nch, u_ref, ws_ref, wct_ref, kt_ref, kr_ref, a_ref, d_ref, h0_ref, y_ref, fin_ref,
               sf_scr, sb_scr, sfs_scr, sbs_scr, hf_scr, hb_scr):
    x = u_ref[0]
    xb = x.astype(BF16)
    s = _dot(xb, ws_ref[0])
    sf_scr[...] = s[:, 0:LANES]
    sb_scr[...] = s[:, LANES:2 * LANES]
    sfs_scr[...] = pltpu.roll(s[:, 0:LANES], S5_N, axis=1)
    sbs_scr[...] = pltpu.roll(s[:, LANES:2 * LANES], S5_N, axis=1)
    a = a_ref[0]
    h_f = h0_ref[0][:, 0:LANES]
    h_b = h0_ref[0][:, LANES:2 * LANES]
    g_f = pltpu.roll(h_f, S5_N, axis=1)
    g_b = pltpu.roll(h_b, S5_N, axis=1)
    for k in range(nch):
        rows_f = pl.ds(k, nb, stride=nch)
        rows_b = pl.ds(nch - 1 - k, nb, stride=nch)
        hf_scr[rows_f, :] = h_f
        hb_scr[rows_b, :] = h_b
        h_f, g_f = (a[0:1] * h_f + a[1:2] * g_f + sf_scr[rows_f, :],
                    a[0:1] * g_f + a[2:3] * h_f + sfs_scr[rows_f, :])
        h_b, g_b = (a[3:4] * h_b + a[4:5] * g_b + sb_scr[rows_b, :],
                    a[3:4] * g_b + a[5:6] * h_b + sbs_scr[rows_b, :])
    fin_ref[0] = jnp.concatenate([h_f, h_b], axis=-1)
    kt, kr = kt_ref[0], kr_ref[0]
    lane = lax.broadcasted_iota(jnp.int32, kt.shape, 1)
    blocks = []
    for s_pos in range(S5_CHUNK):
        sh_f = S5_GROUP * s_pos
        sh_b = S5_GROUP * (S5_CHUNK - 1 - s_pos)
        f = jnp.where(lane >= sh_f, pltpu.roll(kt, sh_f, axis=1), 0.0) if sh_f else kt
        b = jnp.where(lane < S5_ROW - sh_b, pltpu.roll(kr, S5_ROW - sh_b, axis=1), 0.0) if sh_b else kr
        blocks.append(f + b)
    m = jnp.concatenate(blocks, axis=0).astype(BF16)
    hp = jnp.concatenate([hf_scr[...], hb_scr[...]], axis=-1).astype(BF16)
    y_ref[0] = _dot(xb, m) + _dot_nt(hp, wct_ref[0]) + d_ref[0] * x


def _s5_mixer(u, lw, h0, nb):
    n = u.shape[0]
    rows = n // S5_CHUNK
    nch = rows // nb
    g, c = S5_GROUPS, S5_GROUP
    ug = u.reshape(rows, S5_CHUNK, g, c).transpose(2, 0, 1, 3).reshape(g, rows, S5_ROW)
    blk = lambda *shape: pl.BlockSpec((1,) + shape, lambda i: (i, 0, 0))
    yg, fin = pl.pallas_call(
        functools.partial(_s5_kernel, nb, nch),
        grid=(g,),
        in_specs=[blk(rows, S5_ROW), blk(S5_ROW, S5_ROW), blk(S5_ROW, S5_ROW),
                  blk(S5_GROUP, S5_ROW), blk(S5_GROUP, S5_ROW), blk(6, LANES), blk(1, S5_ROW), blk(nb, S5_ROW)],
        out_specs=[blk(rows, S5_ROW), blk(nb, S5_ROW)],
        out_shape=[jax.ShapeDtypeStruct((g, rows, S5_ROW), F32), jax.ShapeDtypeStruct((g, nb, S5_ROW), F32)],
        scratch_shapes=[pltpu.VMEM((rows, LANES), F32)] * 6,
        compiler_params=_cparams(("arbitrary",)),
        name="s5_mixer",
    )(ug, lw["s5_ws"], lw["s5_wct"], lw["s5_kt"], lw["s5_kr"], lw["s5_a"], lw["s5_d"], h0)
    y = yg.reshape(g, rows, S5_CHUNK, c).transpose(1, 2, 0, 3).reshape(n, MIX)
    return y, fin


def _outproj_kernel(ona_ref, ys5_ref, ogq_ref, omla_ref, x_ref, mod_ref, w_ref, wglu_ref, bglu_ref, o_ref):
    y = jax.nn.gelu(ys5_ref[...], approximate=True)
    y = y * jax.nn.sigmoid(_dot(y.astype(BF16), wglu_ref[...]) + bglu_ref[...])
    mixed = _dot(ona_ref[...].astype(BF16), w_ref[0:MIX, :])
    mixed = mixed + _dot(y.astype(BF16), w_ref[MIX:2 * MIX, :])
    mixed = mixed + _dot(ogq_ref[...].astype(BF16), w_ref[2 * MIX:3 * MIX, :])
    mixed = mixed + _dot(omla_ref[...].astype(BF16), w_ref[3 * MIX:4 * MIX, :])
    o_ref[...] = x_ref[...] + mod_ref[0, 2:3, :] * mixed


def _outproj(ona, ys5, ogq, omla, x, mods, lw, mod_base, rows_per_mod):
    n = x.shape[0]
    tm = 512
    row = lambda i: (mod_base + (i * tm) // rows_per_mod, 0, 0)
    part = pl.BlockSpec((tm, MIX), lambda i: (i, 0))
    return pl.pallas_call(
        _outproj_kernel,
        grid=(n // tm,),
        in_specs=[part, part, part, part,
                  pl.BlockSpec((tm, D_MODEL), lambda i: (i, 0)),
                  pl.BlockSpec((1, 6, D_MODEL), row),
                  _full(lw["w_out"].shape), _full(lw["w_glu"].shape), _full(lw["b_glu"].shape)],
        out_specs=pl.BlockSpec((tm, D_MODEL), lambda i: (i, 0)),
        out_shape=jax.ShapeDtypeStruct((n, D_MODEL), F32),
        compiler_params=_cparams(("arbitrary",)),
        name="outproj",
    )(ona, ys5, ogq, omla, x, mods, lw["w_out"], lw["w_glu"], lw["b_glu"])


FF_ROWS = 1024


FFN_VMEM_LIMIT = 58 * 1024 * 1024


def _ffn_kernel(seq, x_ref, mod_ref, g_ref, wup_ref, cw_ref, cb_ref, wd_ref, o_ref, h_scr, ug_scr, uu_scr, act_scr):
    x = x_ref[...]
    y = x * lax.rsqrt(jnp.mean(x * x, axis=-1, keepdims=True) + EPS) * g_ref[...]
    h_scr[...] = (y * (1.0 + mod_ref[0, 4:5, :]) + mod_ref[0, 3:4, :]).astype(BF16)

    pos = lax.broadcasted_iota(jnp.int32, (FF_ROWS, FF_CHUNK), 0) % seq
    first = pos == 0
    last = pos == seq - 1

    def conv(u, cw, cb):
        prev = jnp.where(first, 0.0, pltpu.roll(u, 1, axis=0))
        nxt = jnp.where(last, 0.0, pltpu.roll(u, FF_ROWS - 1, axis=0))
        return cw[0:1, :] * prev + cw[1:2, :] * u + cw[2:3, :] * nxt + cb

    def up_proj(j, slot):
        h = h_scr[...]
        ug_scr[slot] = _dot(h, wup_ref[j])
        uu_scr[slot] = _dot(h, wup_ref[FF_STEPS + j])

    def gate_act(j, slot):
        gate = conv(ug_scr[slot], cw_ref[j], cb_ref[j])
        up = conv(uu_scr[slot], cw_ref[FF_STEPS + j], cb_ref[FF_STEPS + j])
        act_scr[slot] = (gate * jax.nn.sigmoid(gate) * up).astype(BF16)

    def down_proj(j, slot):
        return _dot(act_scr[slot], wd_ref[j])

    last_j = FF_STEPS - 1
    up_proj(0, 0)
    up_proj(1, 1)
    gate_act(0, 0)
    up_proj(2, 0)
    gate_act(1, 1)
    o_ref[...] = down_proj(0, 0)

    def body(j, carry):
        slot = j % 2
        up_proj(j + 2, slot)
        gate_act(j + 1, 1 - slot)
        o_ref[...] += down_proj(j, slot)
        return carry

    lax.fori_loop(1, last_j - 1, body, 0)
    gate_act(last_j, last_j % 2)
    o_ref[...] += down_proj(last_j - 1, (last_j - 1) % 2)
    o_ref[...] = x_ref[...] + mod_ref[0, 5:6, :] * (o_ref[...] + down_proj(last_j, last_j % 2))


def _ffn(x, mods, lw, mod_base, rows_per_mod, seq):
    n = x.shape[0]
    row = lambda i: (mod_base + (i * FF_ROWS) // rows_per_mod, 0, 0)
    resident = lambda a: pl.BlockSpec(a.shape, lambda i: (0,) * a.ndim, pipeline_mode=pl.Buffered(1))
    return pl.pallas_call(
        functools.partial(_ffn_kernel, seq),
        grid=(n // FF_ROWS,),
        in_specs=[pl.BlockSpec((FF_ROWS, D_MODEL), lambda i: (i, 0)),
                  pl.BlockSpec((1, 6, D_MODEL), row),
                  pl.BlockSpec((1, D_MODEL), lambda i: (0, 0)),
                  resident(lw["w_up"]), resident(lw["conv_w"]), resident(lw["conv_b"]), resident(lw["w_down"])],
        out_specs=pl.BlockSpec((FF_ROWS, D_MODEL), lambda i: (i, 0)),
        out_shape=jax.ShapeDtypeStruct((n, D_MODEL), F32),
        scratch_shapes=[pltpu.VMEM((FF_ROWS, D_MODEL), BF16),
                        pltpu.VMEM((2, FF_ROWS, FF_CHUNK), F32), pltpu.VMEM((2, FF_ROWS, FF_CHUNK), F32),
                        pltpu.VMEM((2, FF_ROWS, FF_CHUNK), BF16)],
        compiler_params=pltpu.CompilerParams(dimension_semantics=("arbitrary",), vmem_limit_bytes=FFN_VMEM_LIMIT),
        name="conv_ffn",
    )(x, mods, lw["norm2_g"], lw["w_up"], lw["conv_w"], lw["conv_b"], lw["w_down"])


def _rope_tables():
    pos = np.arange(DEC_SEQ)

    def ang(p, half):
        inv = ROPE_BASE ** (-np.arange(half, dtype=np.float64) / half)
        a = p.astype(np.float64)[:, None] * inv[None, :]
        return np.concatenate([a, a], axis=-1)

    def tables(dim, lanes_before, lanes_after, reps):
        a = np.concatenate([ang(pos // GRID_W, dim // 4), ang(pos % GRID_W, dim // 4)], axis=-1)
        half = dim // 4
        first = (np.arange(dim) % (2 * half)) < half
        cos = np.cos(a)
        sin_a = np.where(first[None, :], -np.sin(a), 0.0)
        sin_b = np.where(first[None, :], 0.0, np.sin(a))

        def place(t, fill):
            t = np.concatenate([np.full((DEC_SEQ, lanes_before), fill), t,
                                np.full((DEC_SEQ, lanes_after), fill)], axis=-1)
            return jnp.asarray(np.tile(t, (1, reps)), F32)

        return place(cos, 1.0), place(sin_a, 0.0), place(sin_b, 0.0)

    gq = tables(DH, 0, 0, LANES // DH)
    mla = tables(MLA_ROPE, MLA_NOPE, LANES - MLA_QK, 1)
    return gq, mla


def _na_bias_tables(rpb):
    qc = np.arange(GRID_W)[:, None]
    kc = np.arange(GRID_W)[None, :]
    col_start = np.clip(qc - NA_WIN_C // 2, 0, GRID_W - NA_WIN_C)
    valid = (kc >= col_start) & (kc < col_start + NA_WIN_C)
    d_c = np.clip(kc - qc, 1 - NA_WIN_C, NA_WIN_C - 1) + NA_WIN_C - 1
    nrel = 2 * NA_WIN_C - 1
    onehot = jnp.asarray((d_c.reshape(-1)[None, :] == np.arange(nrel)[:, None]).astype(np.float32))
    t = jnp.dot(rpb.astype(F32).reshape(-1, nrel), onehot, precision=lax.Precision.HIGHEST)
    t = t.reshape(NA_HEADS, 2 * NA_WIN_R - 1, GRID_W, GRID_W)
    t = jnp.where(jnp.asarray(valid)[None, None], t, NEG)
    t = t.transpose(1, 0, 2, 3).reshape(2 * NA_WIN_R - 1, NA_HEADS * GRID_W, GRID_W)
    return jnp.concatenate([t[:-1], t[1:]], axis=-1)


def _s5_matrices(lam_re, lam_im, log_dt, b_re, b_im, c_re, c_im):
    c = S5_CHUNK
    cat = lambda parts: jnp.concatenate(parts, axis=-1)
    hi = lax.Precision.HIGHEST
    pw, kern, bre, bim, cre, cim = [], [], [], [], [], []
    for d in range(2):
        dt = jnp.exp(log_dt[d].astype(F32))[:, None]
        lr, li = lam_re[d].astype(F32), lam_im[d].astype(F32)
        mag = jnp.exp(lr * dt)
        a_re, a_im = mag * jnp.cos(li * dt), mag * jnp.sin(li * dt)
        den = lr * lr + li * li
        f_re = ((a_re - 1.0) * lr + a_im * li) / den
        f_im = (a_im * lr - (a_re - 1.0) * li) / den
        br, bi = b_re[d].astype(F32), b_im[d].astype(F32)
        bre.append((f_re[..., None] * br - f_im[..., None] * bi).transpose(0, 2, 1))
        bim.append((f_re[..., None] * bi + f_im[..., None] * br).transpose(0, 2, 1))
        cre.append(c_re[d].astype(F32))
        cim.append(c_im[d].astype(F32))
        p_re, p_im = [jnp.ones_like(a_re)], [jnp.zeros_like(a_re)]
        for _ in range(c):
            p_re, p_im = (p_re + [p_re[-1] * a_re - p_im[-1] * a_im], p_im + [p_re[-1] * a_im + p_im[-1] * a_re])
        p_re, p_im = jnp.stack(p_re, axis=1), jnp.stack(p_im, axis=1)
        pw.append((p_re, p_im))
        q_re = cre[-1][:, None] * p_re[:, :c, None, :] - cim[-1][:, None] * p_im[:, :c, None, :]
        q_im = cre[-1][:, None] * p_im[:, :c, None, :] + cim[-1][:, None] * p_re[:, :c, None, :]
        kern.append(jnp.einsum("gjon,gcn->gcjo", q_re, bre[-1], precision=hi)
                    - jnp.einsum("gjon,gcn->gcjo", q_im, bim[-1], precision=hi))
    up, dn = np.arange(c), np.arange(c)[::-1]
    (pfr, pfi), (pbr, pbi) = pw
    outer = lambda x, y: (x[:, :, None, :] * y[:, None, :, :])
    e_fr, e_fi, e_br, e_bi = pfr[:, dn], pfi[:, dn], pbr[:, up], pbi[:, up]
    ws = (outer(cat([e_fr, e_fr, e_br, e_br]), cat([bre[0], bim[0], bre[1], bim[1]]))
          + outer(cat([-e_fi, e_fi, -e_bi, e_bi]), cat([bim[0], bre[0], bim[1], bre[1]])))
    e_fr, e_fi, e_br, e_bi = pfr[:, up + 1], pfi[:, up + 1], pbr[:, dn + 1], pbi[:, dn + 1]
    c_a, c_b = cat([cre[0], cre[0], cre[1], cre[1]]), cat([cim[0], cim[0], cim[1], cim[1]])
    wct = outer(cat([e_fr, -e_fi, e_br, -e_bi]), c_a) + outer(cat([-e_fi, -e_fr, -e_bi, -e_br]), c_b)
    rows = lambda w: w.reshape(S5_GROUPS, S5_ROW, 4 * S5_N).astype(BF16)
    kt = kern[0].reshape(S5_GROUPS, S5_GROUP, S5_ROW)
    kr = kern[1][:, :, ::-1, :].reshape(S5_GROUPS, S5_GROUP, S5_ROW)
    av = []
    for p_re, p_im in ((pfr[:, c], pfi[:, c]), (pbr[:, c], pbi[:, c])):
        av += [cat([p_re, p_re]), cat([-p_im, p_im]), cat([p_im, -p_im])]
    return rows(ws), rows(wct), kt, kr, jnp.stack(av, axis=1)


def _layer_weights(l, p):
    w = p["w_in"][l]
    q0 = Z_NA + Z_S5
    gq = lambda h: w[:, q0 + h * DH:q0 + (h + 1) * DH]
    m0 = q0 + 2 * MIX
    w_in = jnp.concatenate(
        [w[:, 0:q0], gq(0), gq(2), gq(1), gq(3), w[:, q0 + MIX:m0],
         w[:, m0 + MLA_QLORA:m0 + MLA_QLORA + MLA_KVLORA], w[:, m0:m0 + MLA_QLORA],
         w[:, m0 + MLA_QLORA + MLA_KVLORA:], jnp.zeros((D_MODEL, Z_ALL - w.shape[1]), w.dtype)],
        axis=1).astype(BF16)
    wo = p["w_out"][l]
    og = lambda h: wo[2 * MIX + h * DH:2 * MIX + (h + 1) * DH]
    w_out = jnp.concatenate([wo[0:2 * MIX], og(0), og(2), og(1), og(3), wo[3 * MIX:]], axis=0).astype(BF16)
    pad_head = lambda g: jnp.tile(jnp.pad(g, (0, LANES - MLA_QK)), MLA_HEADS)[None, :]
    wuq = jnp.pad(p["mla_w_uq"][l].reshape(MLA_QLORA, MLA_HEADS, MLA_QK),
                  ((0, 2 * LANES - MLA_QLORA), (0, 0), (0, LANES - MLA_QK))).reshape(2 * LANES, MLA_HEADS * LANES)
    wukv = p["mla_w_ukv"][l].reshape(MLA_KVLORA, MLA_HEADS, MLA_NOPE + DH)
    wuk = jnp.pad(wukv[:, :, :MLA_NOPE], ((0, 0), (0, 0), (0, LANES - MLA_NOPE))).reshape(MLA_KVLORA, MLA_HEADS * LANES)
    wuv = wukv[:, :, MLA_NOPE:].reshape(MLA_KVLORA, MIX)
    s5_ws, s5_wct, s5_kt, s5_kr, s5_a = _s5_matrices(p["s5_lam_re"][l], p["s5_lam_im"][l], p["s5_log_dt"][l],
                                       p["s5_b_re"][l], p["s5_b_im"][l], p["s5_c_re"][l], p["s5_c_im"][l])
    return dict(
        norm1_g=p["norm1_g"][l][None, :], norm2_g=p["norm2_g"][l][None, :],
        w_in=w_in, w_out=w_out,
        naq_g=jnp.tile(p["na_qn"][l], NA_HEADS)[None, :], nak_g=jnp.tile(p["na_kn"][l], NA_HEADS)[None, :],
        gqq_g=jnp.tile(p["gq_qn"][l], GQ_HEADS)[None, :], gqk_g=jnp.tile(p["gq_kn"][l], GQ_KV)[None, :],
        qa_g=jnp.pad(p["mla_qa_g"][l], (0, 2 * LANES - MLA_QLORA))[None, :], kva_g=p["mla_kva_g"][l][None, :],
        mqn_g=pad_head(p["mla_qn"][l]), mkn_g=pad_head(p["mla_kn"][l]),
        wuq=wuq.astype(BF16), wuk=wuk.astype(BF16), wuv=wuv.astype(BF16),
        na_bias=_na_bias_tables(p["na_rpb"][l]),
        s5_ws=s5_ws, s5_wct=s5_wct, s5_kt=s5_kt, s5_kr=s5_kr, s5_a=s5_a,
        s5_d=jnp.tile(p["s5_d"][l].reshape(S5_GROUPS, 1, S5_GROUP), (1, 1, S5_CHUNK)),
        w_glu=p["s5_w_glu"][l].astype(BF16), b_glu=p["s5_b_glu"][l][None, :],
        w_up=p["ffn_w_up"][l].astype(BF16).reshape(D_MODEL, 2 * FF_STEPS, FF_CHUNK).transpose(1, 0, 2),
        conv_w=p["ffn_conv_w"][l].reshape(3, 2 * FF_STEPS, FF_CHUNK).transpose(1, 0, 2),
        conv_b=p["ffn_conv_b"][l].reshape(2 * FF_STEPS, 1, FF_CHUNK),
        w_down=p["ffn_w_down"][l].astype(BF16).reshape(FF_STEPS, FF_CHUNK, D_MODEL),
    )


def _heads_first(x, nb, t, heads):
    return x.reshape(nb, t, heads, DH).transpose(0, 2, 1, 3)


def _lanes_packed(x):
    b, h, pl_, dh = x.shape
    return x.transpose(0, 2, 1, 3).reshape(b, pl_, h * dh)


def kernel(x_prompt, x_sample, cache_na_k, cache_na_v, state_s5, cache_gqa_k, cache_gqa_v, cache_mla_ckv,
           cache_mla_krope, c, c_ctx, norm1_g, norm2_g, ada_w, ada_b, w_in, na_qn, na_kn, na_rpb, s5_lam_re,
           s5_lam_im, s5_log_dt, s5_b_re, s5_b_im, s5_c_re, s5_c_im, s5_d, s5_w_glu, s5_b_glu, gq_qn, gq_kn,
           mla_qa_g, mla_kva_g, mla_w_uq, mla_w_ukv, mla_qn, mla_kn, w_out, ffn_w_up, ffn_conv_w, ffn_conv_b,
           ffn_w_down):
    p = dict(norm1_g=norm1_g, norm2_g=norm2_g, w_in=w_in, na_qn=na_qn, na_kn=na_kn, na_rpb=na_rpb,
             s5_lam_re=s5_lam_re, s5_lam_im=s5_lam_im, s5_log_dt=s5_log_dt, s5_b_re=s5_b_re, s5_b_im=s5_b_im,
             s5_c_re=s5_c_re, s5_c_im=s5_c_im, s5_d=s5_d, s5_w_glu=s5_w_glu, s5_b_glu=s5_b_glu,
             gq_qn=gq_qn, gq_kn=gq_kn, mla_qa_g=mla_qa_g, mla_kva_g=mla_kva_g, mla_w_uq=mla_w_uq,
             mla_w_ukv=mla_w_ukv, mla_qn=mla_qn, mla_kn=mla_kn, w_out=w_out, ffn_w_up=ffn_w_up,
             ffn_conv_w=ffn_conv_w, ffn_conv_b=ffn_conv_b, ffn_w_down=ffn_w_down)
    nb, nd = BATCH, DEC_BATCH
    cvec = jnp.concatenate([c_ctx[None, :], c, jnp.zeros((8 - 1 - nd, D_MODEL), F32)], axis=0)
    mods_all = _ada_mods(cvec, ada_w, ada_b).reshape(DEPTH, 8, 6, D_MODEL)
    gq_tabs, mla_tabs = _rope_tables()

    xc = x_prompt.reshape(nb * SEQ, D_MODEL)
    xl = x_sample.reshape(nd * DEC_SEQ, D_MODEL)
    zero_state = jnp.zeros((S5_GROUPS, nb, 4 * S5_N), F32)
    caches = []
    for l in range(DEPTH):
        lw = _layer_weights(l, p)
        mods = mods_all[l]

        zna, zs5, zgq, zm = _inproj(xc, mods, lw["norm1_g"], lw["w_in"], 0, nb * SEQ)
        ona, ogq, omla, nk, nv, gk, gv, ckv, kr = _ctx_attn(zna, zgq, zm, lw)
        ys5, fin = _s5_mixer(zs5, lw, zero_state, nb)
        xc = _outproj(ona, ys5, ogq, omla, xc, mods, lw, 0, nb * SEQ)
        xc = _ffn(xc, mods, lw, 0, nb * SEQ, SEQ)
        caches.append((_heads_first(nk, nb, SEQ, NA_HEADS), _heads_first(nv, nb, SEQ, NA_HEADS),
                       fin.reshape(S5_GROUPS, nb, 2, 2, S5_N).transpose(1, 2, 3, 0, 4),
                       _heads_first(gk, nb, SEQ, GQ_KV), _heads_first(gv, nb, SEQ, GQ_KV),
                       ckv.reshape(nb, SEQ, MLA_KVLORA), kr.reshape(nb, SEQ, MLA_ROPE)))

        zna, zs5, zgq, zm = _inproj(xl, mods, lw["norm1_g"], lw["w_in"], 1, DEC_SEQ)
        ona = _na_lat(zna, _lanes_packed(cache_na_k[:, l]), _lanes_packed(cache_na_v[:, l]), lw["na_bias"], lw)
        ogq = _gq_lat(zgq, _lanes_packed(cache_gqa_k[:, l]), _lanes_packed(cache_gqa_v[:, l]), gq_tabs, lw)
        krc = jnp.pad(cache_mla_krope[:, l], ((0, 0), (0, 0), (MLA_NOPE, LANES - MLA_QK)))
        omla = _mla_lat(zm, cache_mla_ckv[:, l], krc, mla_tabs, lw)
        h0 = state_s5[:, l].astype(F32).transpose(3, 0, 1, 2, 4).reshape(S5_GROUPS, nd, 4 * S5_N)
        ys5, _ = _s5_mixer(zs5, lw, h0, nd)
        xl = _outproj(ona, ys5, ogq, omla, xl, mods, lw, 1, DEC_SEQ)
        xl = _ffn(xl, mods, lw, 1, DEC_SEQ, DEC_SEQ)

    stack = lambda i: jnp.stack([cl[i] for cl in caches], axis=1)
    return (xc.reshape(nb, SEQ, D_MODEL), xl.reshape(nd, DEC_SEQ, D_MODEL),
            stack(0), stack(1), stack(2), stack(3), stack(4), stack(5), stack(6))
```

```python
import functools

import numpy as np
import jax
import jax.numpy as jnp
from jax import lax
from jax.experimental import pallas as pl
from jax.experimental.pallas import tpu as pltpu

F32 = jnp.float32
BF16 = jnp.bfloat16

D_MODEL = 1024
BATCH = 32
SEQ = 256
DEPTH = 2
DEC_BATCH = 4
DEC_SEQ = 1024
PAST_LEN = 512
GRID_W = 64
GRID_ROWS = DEC_SEQ // GRID_W
MIX = D_MODEL // 4
DH = 64
NA_HEADS = MIX // DH
NA_WIN_R = 8
NA_WIN_C = 16
S5_GROUP = 16
S5_GROUPS = MIX // S5_GROUP
S5_N = 64
GQ_HEADS = MIX // DH
GQ_KV = GQ_HEADS // 2
MLA_HEADS = MIX // DH
MLA_NOPE = 64
MLA_ROPE = 32
MLA_QK = MLA_NOPE + MLA_ROPE
MLA_QLORA = (3 * D_MODEL) // 16
MLA_KVLORA = D_MODEL // 8
D_FF = 128 * ((8 * D_MODEL // 3 + 127) // 128)
ROPE_BASE = 10000.0
EPS = 1e-6
NEG = -1e30

LANES = 128
MXU_DIM = 256
VMEM_LIMIT = 48 * 1024 * 1024

Z_NA = 3 * MIX
Z_S5 = MIX
Z_GQ = 2 * MIX
Z_MLA = 3 * LANES
Z_ALL = Z_NA + Z_S5 + Z_GQ + Z_MLA

S5_CHUNK = 16
S5_ROW = S5_CHUNK * S5_GROUP
S5_CTX_BLOCK = 8

FF_CHUNK = 256
FF_STEPS = D_FF // FF_CHUNK


def _cparams(sem):
    return pltpu.CompilerParams(dimension_semantics=sem, vmem_limit_bytes=VMEM_LIMIT)


def _dot(a, b):
    return jnp.dot(a, b, preferred_element_type=F32)


def _dot_nt(a, b):
    return lax.dot_general(a, b, (((1,), (1,)), ((), ())), preferred_element_type=F32)


def _dot_nt_f32(a, b):
    def split(x):
        hi = x.astype(BF16)
        r1 = x - hi.astype(F32)
        mid = r1.astype(BF16)
        return hi, mid, (r1 - mid.astype(F32)).astype(BF16)

    pa, pb = split(a), split(b)
    acc = None
    for i in range(3):
        for j in range(3 - i):
            term = _dot_nt(pa[i], pb[j])
            acc = term if acc is None else acc + term
    return acc


def _rms_rows(x, gain, denom=None, valid=None):
    xx = x * x
    if valid is not None and valid != x.shape[-1]:
        lane = lax.broadcasted_iota(jnp.int32, x.shape, 1)
        xx = jnp.where(lane < valid, xx, 0.0)
    denom = denom or (valid or x.shape[-1])
    ss = jnp.sum(xx, axis=-1, keepdims=True)
    return x * lax.rsqrt(ss / denom + EPS) * gain


def _seg_rms(x, seg, denom, gain):
    rows, width = x.shape
    if seg % LANES == 0:
        parts = []
        for s in range(width // seg):
            xs = x[:, s * seg:(s + 1) * seg]
            ss = jnp.sum(xs * xs, axis=-1, keepdims=True)
            parts.append(xs * lax.rsqrt(ss / denom + EPS))
        return jnp.concatenate(parts, axis=-1) * gain
    seg_id = lax.broadcasted_iota(jnp.int32, x.shape, 1) // seg
    xx = x * x
    scale = jnp.zeros_like(x)
    for s in range(width // seg):
        m = seg_id == s
        ss = jnp.sum(jnp.where(m, xx, 0.0), axis=-1, keepdims=True)
        scale = jnp.where(m, lax.rsqrt(ss / denom + EPS), scale)
    return x * scale * gain


def _rope(x, cos, sin_a, sin_b, half):
    tiles = []
    for t in range(x.shape[-1] // LANES):
        xt = x[:, t * LANES:(t + 1) * LANES]
        up = pltpu.roll(xt, LANES - half, axis=1)
        dn = pltpu.roll(xt, half, axis=1)
        tiles.append(xt * cos + up * sin_a + dn * sin_b)
    return tiles[0] if len(tiles) == 1 else jnp.concatenate(tiles, axis=-1)


def _packed_attn(q, parts, nseg, scale, biases=None):
    tq, width = q.shape
    seg_id = lax.broadcasted_iota(jnp.int32, q.shape, 1) // DH
    lhs = jnp.concatenate([jnp.where(seg_id == h, q, 0.0) for h in range(nseg)], axis=0).astype(BF16)
    scores = []
    for i, (kb, _) in enumerate(parts):
        s = _dot_nt(lhs, kb) * scale
        if biases is not None and biases[i] is not None:
            s = s + biases[i]
        scores.append(s)
    m = scores[0].max(axis=-1, keepdims=True)
    for s in scores[1:]:
        m = jnp.maximum(m, s.max(axis=-1, keepdims=True))
    den = None
    pv = None
    for s, (_, vb) in zip(scores, parts):
        p = jnp.exp(s - m)
        ps = p.sum(axis=-1, keepdims=True)
        den = ps if den is None else den + ps
        c = _dot(p.astype(BF16), vb)
        pv = c if pv is None else pv + c
    pv = pv / den
    out = jnp.zeros((tq, width), F32)
    for h in range(nseg):
        out = out + jnp.where(seg_id == h, pv[h * tq:(h + 1) * tq], 0.0)
    return out


def _mla_attn(q, parts, scale):
    tq = q.shape[0]
    vseg = None
    out = jnp.zeros((tq, MIX), F32)
    for h in range(MLA_HEADS):
        qh = q[:, h * LANES:(h + 1) * LANES].astype(BF16)
        scores = [_dot_nt(qh, kb[:, h * LANES:(h + 1) * LANES]) * scale for kb, _ in parts]
        m = scores[0].max(axis=-1, keepdims=True)
        for s in scores[1:]:
            m = jnp.maximum(m, s.max(axis=-1, keepdims=True))
        den = None
        pv = None
        for s, (_, vb) in zip(scores, parts):
            p = jnp.exp(s - m)
            ps = p.sum(axis=-1, keepdims=True)
            den = ps if den is None else den + ps
            vseg = lax.broadcasted_iota(jnp.int32, vb.shape, 1) // DH
            c = _dot(p.astype(BF16), jnp.where(vseg == h, vb, jnp.zeros_like(vb)))
            pv = c if pv is None else pv + c
        out = out + pv / den
    return out


def _mla_qkv(zm, qa_g, kva_g, mqn_g, mkn_g, wuq, wuk, wuv):
    ckv = _rms_rows(zm[:, 0:LANES], kva_g)
    col3 = zm[:, 2 * LANES:3 * LANES]
    lane = lax.broadcasted_iota(jnp.int32, col3.shape, 1)
    kr_placed = jnp.where((lane >= MLA_NOPE) & (lane < MLA_QK), col3, 0.0)
    cq = _rms_rows(zm[:, LANES:3 * LANES], qa_g, valid=MLA_QLORA)
    q = _seg_rms(_dot(cq.astype(BF16), wuq), LANES, MLA_QK, mqn_g)
    k, v = _mla_kv(ckv, kr_placed, mkn_g, wuk, wuv)
    return q, k, v, ckv, col3


def _mla_kv(ckv, kr_placed, mkn_g, wuk, wuv):
    cb = ckv.astype(BF16)
    kf = _dot(cb, wuk) + jnp.concatenate([kr_placed] * MLA_HEADS, axis=-1)
    return _seg_rms(kf, LANES, MLA_QK, mkn_g), _dot(cb, wuv)


def _ada_kernel(c_ref, w_ref, b_ref, o_ref):
    cv = c_ref[...]
    s = cv * jax.nn.sigmoid(cv)
    o_ref[0] = _dot(s.astype(BF16), w_ref[0].astype(BF16)) + b_ref[0]


def _ada_mods(cvec, ada_w, ada_b):
    tn = 1536
    n = ada_w.shape[-1]
    return pl.pallas_call(
        _ada_kernel,
        grid=(DEPTH, n // tn),
        in_specs=[pl.BlockSpec((8, D_MODEL), lambda l, j: (0, 0)),
                  pl.BlockSpec((1, D_MODEL, tn), lambda l, j: (l, 0, j)),
                  pl.BlockSpec((1, 1, tn), lambda l, j: (l, 0, j))],
        out_specs=pl.BlockSpec((1, 8, tn), lambda l, j: (l, 0, j)),
        out_shape=jax.ShapeDtypeStruct((DEPTH, 8, n), F32),
        compiler_params=_cparams(("arbitrary", "arbitrary")),
        name="ada_mods",
    )(cvec, ada_w, ada_b.reshape(DEPTH, 1, n))


def _inproj_kernel(x_ref, mod_ref, g_ref, w_ref, ona_ref, os5_ref, ogq_ref, omla_ref):
    x = x_ref[...]
    y = x * lax.rsqrt(jnp.mean(x * x, axis=-1, keepdims=True) + EPS) * g_ref[...]
    h = y * (1.0 + mod_ref[0, 1:2, :]) + mod_ref[0, 0:1, :]
    z = _dot(h.astype(BF16), w_ref[...])
    ona_ref[...] = z[:, 0:Z_NA]
    os5_ref[...] = z[:, Z_NA:Z_NA + Z_S5]
    ogq_ref[...] = z[:, Z_NA + Z_S5:Z_NA + Z_S5 + Z_GQ]
    omla_ref[...] = z[:, Z_NA + Z_S5 + Z_GQ:Z_ALL]


def _inproj(x, mods, g, w, mod_base, rows_per_mod):
    n = x.shape[0]
    tm = 512
    row = lambda i: (mod_base + (i * tm) // rows_per_mod, 0, 0)
    widths = (Z_NA, Z_S5, Z_GQ, Z_MLA)
    return pl.pallas_call(
        _inproj_kernel,
        grid=(n // tm,),
        in_specs=[pl.BlockSpec((tm, D_MODEL), lambda i: (i, 0)),
                  pl.BlockSpec((1, 6, D_MODEL), row),
                  pl.BlockSpec((1, D_MODEL), lambda i: (0, 0)),
                  pl.BlockSpec((D_MODEL, Z_ALL), lambda i: (0, 0))],
        out_specs=[pl.BlockSpec((tm, wd), lambda i: (i, 0)) for wd in widths],
        out_shape=[jax.ShapeDtypeStruct((n, wd), F32) for wd in widths],
        compiler_params=_cparams(("arbitrary",)),
        name="inproj",
    )(x, mods, g, w)


def _store_heads(ref, x, heads):
    for h in range(heads):
        ref[0, 0, h] = x[:, h * DH:(h + 1) * DH]


def _ctx_attn_kernel(zna_ref, zgq_ref, zm_ref, naq_g, nak_g, gqq_g, gqk_g, qa_g, kva_g, mqn_g, mkn_g,
                     wuq_ref, wuk_ref, wuv_ref, *refs):
    ona_ref, ogq_ref, omla_ref, nk_ref, nv_ref, gk_ref, gv_ref, ckv_ref, kr_ref = refs[-9:]
    zna = zna_ref[...]
    q = _seg_rms(zna[:, 0:MIX], DH, DH, naq_g[...])
    k = _seg_rms(zna[:, MIX:2 * MIX], DH, DH, nak_g[...])
    v = zna[:, 2 * MIX:3 * MIX]
    _store_heads(nk_ref, k, NA_HEADS)
    _store_heads(nv_ref, v, NA_HEADS)
    ona_ref[...] = _packed_attn(q, [(k.astype(BF16), v.astype(BF16))], NA_HEADS, DH ** -0.5)

    zgq = zgq_ref[...]
    gq = _seg_rms(zgq[:, 0:MIX], DH, DH, gqq_g[...])
    gk = _seg_rms(zgq[:, MIX:MIX + LANES], DH, DH, gqk_g[...])
    gv = zgq[:, MIX + LANES:2 * MIX]
    _store_heads(gk_ref, gk, GQ_KV)
    _store_heads(gv_ref, gv, GQ_KV)
    kv = [(gk.astype(BF16), gv.astype(BF16))]
    ogq_ref[...] = jnp.concatenate(
        [_packed_attn(gq[:, r * LANES:(r + 1) * LANES], kv, GQ_KV, DH ** -0.5) for r in range(2)], axis=-1)

    mq, mk, mv, ckv, col3 = _mla_qkv(zm_ref[...], qa_g[...], kva_g[...], mqn_g[...], mkn_g[...],
                                     wuq_ref[...], wuk_ref[...], wuv_ref[...])
    ckv_ref[0, 0] = ckv
    kr_ref[0, 0] = pltpu.roll(col3, LANES - MLA_NOPE, axis=1)[:, 0:MLA_ROPE]
    omla_ref[...] = _mla_attn(mq, [(mk.astype(BF16), mv.astype(BF16))], MLA_QK ** -0.5)


def _full(shape):
    nd = len(shape)
    return pl.BlockSpec(shape, lambda *a: (0,) * nd)


CACHE_SHAPES = ((NA_HEADS, SEQ, DH), (NA_HEADS, SEQ, DH), (GQ_KV, SEQ, DH), (GQ_KV, SEQ, DH),
                (SEQ, MLA_KVLORA), (SEQ, MLA_ROPE))


def _ctx_attn(zna, zgq, zm, lw, layer, caches):
    n = zna.shape[0]
    t = SEQ
    rows = lambda wd: pl.BlockSpec((t, wd), lambda b: (b, 0))
    cache_spec = lambda shp: pl.BlockSpec((1, 1) + shp, lambda b: (b, layer) + (0,) * len(shp))
    gains = [lw["naq_g"], lw["nak_g"], lw["gqq_g"], lw["gqk_g"], lw["qa_g"], lw["kva_g"], lw["mqn_g"], lw["mkn_g"]]
    weights = [lw["wuq"], lw["wuk"], lw["wuv"]]
    n_in = 3 + len(gains) + len(weights)
    outs = pl.pallas_call(
        _ctx_attn_kernel,
        grid=(n // t,),
        in_specs=[rows(Z_NA), rows(Z_GQ), rows(Z_MLA)] + [_full(a.shape) for a in gains + weights]
                 + [pl.BlockSpec(memory_space=pl.ANY)] * len(caches),
        out_specs=[rows(MIX)] * 3 + [cache_spec(shp) for shp in CACHE_SHAPES],
        out_shape=[jax.ShapeDtypeStruct((n, MIX), F32)] * 3
                  + [jax.ShapeDtypeStruct((n // t, DEPTH) + shp, F32) for shp in CACHE_SHAPES],
        input_output_aliases={n_in + i: 3 + i for i in range(len(caches))},
        compiler_params=_cparams(("arbitrary",)),
        name="ctx_attn",
    )(zna, zgq, zm, *gains, *weights, *caches)
    return outs[:3], list(outs[3:])


def _na_window_start(r):
    return jnp.clip(r - NA_WIN_R // 2, 0, GRID_ROWS - NA_WIN_R)


def _pack_heads(ref):
    return jnp.concatenate([ref[0, 0, h] for h in range(ref.shape[2])], axis=-1)


def _na_lat_kernel(zna_ref, kc_ref, vc_ref, bias_ref, naq_g, nak_g, o_ref, k_scr, v_scr, kc_scr, vc_scr):
    r = pl.program_id(1)

    @pl.when(r == 0)
    def _():
        k_scr[...] = _seg_rms(zna_ref[:, MIX:2 * MIX], DH, DH, nak_g[...]).astype(BF16)
        v_scr[...] = zna_ref[:, 2 * MIX:3 * MIX].astype(BF16)
        kc_scr[...] = _pack_heads(kc_ref).astype(BF16)
        vc_scr[...] = _pack_heads(vc_ref).astype(BF16)

    q = _seg_rms(zna_ref[pl.ds(pl.multiple_of(r * GRID_W, GRID_W), GRID_W), 0:MIX], DH, DH, naq_g[...])
    start = pl.multiple_of(_na_window_start(r) * GRID_W, GRID_W)
    span = NA_WIN_R * GRID_W
    parts = [(k_scr[pl.ds(start, span), :], v_scr[pl.ds(start, span), :]),
             (kc_scr[...], vc_scr[...])]
    d0 = _na_window_start(r) - r + NA_WIN_R - 1
    bias = jnp.concatenate([bias_ref[d0 + 2 * i] for i in range(NA_WIN_R // 2)], axis=-1)
    o_ref[...] = _packed_attn(q, parts, NA_HEADS, DH ** -0.5, biases=[bias, None])


def _na_lat(zna, kc, vc, layer, bias, lw):
    nb = DEC_BATCH
    cache = pl.BlockSpec((1, 1, NA_HEADS, PAST_LEN, DH), lambda b, r: (b, layer, 0, 0, 0))
    return pl.pallas_call(
        _na_lat_kernel,
        grid=(nb, GRID_ROWS),
        in_specs=[pl.BlockSpec((DEC_SEQ, Z_NA), lambda b, r: (b, 0)),
                  cache, cache,
                  _full(bias.shape),
                  _full(lw["naq_g"].shape), _full(lw["nak_g"].shape)],
        out_specs=pl.BlockSpec((GRID_W, MIX), lambda b, r: (b * GRID_ROWS + r, 0)),
        out_shape=jax.ShapeDtypeStruct((nb * DEC_SEQ, MIX), F32),
        scratch_shapes=[pltpu.VMEM((DEC_SEQ, MIX), BF16), pltpu.VMEM((DEC_SEQ, MIX), BF16),
                        pltpu.VMEM((PAST_LEN, MIX), BF16), pltpu.VMEM((PAST_LEN, MIX), BF16)],
        compiler_params=_cparams(("arbitrary", "arbitrary")),
        name="na_latent",
    )(zna, kc, vc, bias, lw["naq_g"], lw["nak_g"])


LAT_QB = 256
LAT_KEYS = DEC_SEQ + PAST_LEN


def _gq_lat_kernel(zgq_ref, kc_ref, vc_ref, cos_ref, sa_ref, sb_ref, gqq_g, gqk_g, o_ref, k_scr, v_scr):
    j = pl.program_id(1)

    @pl.when(j == 0)
    def _():
        k = _seg_rms(zgq_ref[:, MIX:MIX + LANES], DH, DH, gqk_g[...])
        k = _rope(k, cos_ref[...], sa_ref[...], sb_ref[...], DH // 4)
        k_scr[0:DEC_SEQ, :] = k.astype(BF16)
        k_scr[DEC_SEQ:LAT_KEYS, :] = _pack_heads(kc_ref).astype(BF16)
        v_scr[0:DEC_SEQ, :] = zgq_ref[:, MIX + LANES:2 * MIX].astype(BF16)
        v_scr[DEC_SEQ:LAT_KEYS, :] = _pack_heads(vc_ref).astype(BF16)

    rows = pl.ds(pl.multiple_of(j * LAT_QB, LAT_QB), LAT_QB)
    q = _seg_rms(zgq_ref[rows, 0:MIX], DH, DH, gqq_g[...])
    q = _rope(q, cos_ref[rows, :], sa_ref[rows, :], sb_ref[rows, :], DH // 4)
    kv = [(k_scr[...], v_scr[...])]
    o_ref[...] = jnp.concatenate(
        [_packed_attn(q[:, r * LANES:(r + 1) * LANES], kv, GQ_KV, DH ** -0.5) for r in range(2)], axis=-1)


def _gq_lat(zgq, kc, vc, layer, tabs, lw):
    nb = DEC_BATCH
    nq = DEC_SEQ // LAT_QB
    cache = pl.BlockSpec((1, 1, GQ_KV, PAST_LEN, DH), lambda b, j: (b, layer, 0, 0, 0))
    return pl.pallas_call(
        _gq_lat_kernel,
        grid=(nb, nq),
        in_specs=[pl.BlockSpec((DEC_SEQ, Z_GQ), lambda b, j: (b, 0)), cache, cache]
                 + [_full(t.shape) for t in tabs] + [_full(lw["gqq_g"].shape), _full(lw["gqk_g"].shape)],
        out_specs=pl.BlockSpec((LAT_QB, MIX), lambda b, j: (b * nq + j, 0)),
        out_shape=jax.ShapeDtypeStruct((nb * DEC_SEQ, MIX), F32),
        scratch_shapes=[pltpu.VMEM((LAT_KEYS, LANES), BF16), pltpu.VMEM((LAT_KEYS, LANES), BF16)],
        compiler_params=_cparams(("arbitrary", "arbitrary")),
        name="gq_latent",
    )(zgq, kc, vc, *tabs, lw["gqq_g"], lw["gqk_g"])


def _mla_lat_kernel(zm_ref, ckvc_ref, krc_ref, cos_ref, sa_ref, sb_ref, qa_g, kva_g, mqn_g, mkn_g,
                    wuq_ref, wuk_ref, wuv_ref, o_ref, k_scr, v_scr):
    j = pl.program_id(1)
    half = MLA_ROPE // 4

    @pl.when(j == 0)
    def _():
        zm = zm_ref[...]
        ckv = _rms_rows(zm[:, 0:LANES], kva_g[...])
        col3 = zm[:, 2 * LANES:3 * LANES]
        lane = lax.broadcasted_iota(jnp.int32, col3.shape, 1)
        kr_placed = jnp.where((lane >= MLA_NOPE) & (lane < MLA_QK), col3, 0.0)
        k, v = _mla_kv(ckv, kr_placed, mkn_g[...], wuk_ref[...], wuv_ref[...])
        k = _rope(k, cos_ref[...], sa_ref[...], sb_ref[...], half)
        k_scr[0:DEC_SEQ, :] = k.astype(BF16)
        v_scr[0:DEC_SEQ, :] = v.astype(BF16)
        krc = jnp.concatenate([jnp.zeros((PAST_LEN, MLA_NOPE), F32), krc_ref[0, 0],
                               jnp.zeros((PAST_LEN, LANES - MLA_QK), F32)], axis=-1)
        kc, vc = _mla_kv(ckvc_ref[0, 0], krc, mkn_g[...], wuk_ref[...], wuv_ref[...])
        k_scr[DEC_SEQ:LAT_KEYS, :] = kc.astype(BF16)
        v_scr[DEC_SEQ:LAT_KEYS, :] = vc.astype(BF16)

    rows = pl.ds(pl.multiple_of(j * LAT_QB, LAT_QB), LAT_QB)
    cq = _rms_rows(zm_ref[rows, LANES:3 * LANES], qa_g[...], valid=MLA_QLORA)
    q = _seg_rms(_dot(cq.astype(BF16), wuq_ref[...]), LANES, MLA_QK, mqn_g[...])
    q = _rope(q, cos_ref[rows, :], sa_ref[rows, :], sb_ref[rows, :], half)
    o_ref[...] = _mla_attn(q, [(k_scr[...], v_scr[...])], MLA_QK ** -0.5)


def _mla_lat(zm, ckvc, krc, layer, tabs, lw):
    nb = DEC_BATCH
    nq = DEC_SEQ // LAT_QB
    small = [lw["qa_g"], lw["kva_g"], lw["mqn_g"], lw["mkn_g"], lw["wuq"], lw["wuk"], lw["wuv"]]
    return pl.pallas_call(
        _mla_lat_kernel,
        grid=(nb, nq),
        in_specs=[pl.BlockSpec((DEC_SEQ, Z_MLA), lambda b, j: (b, 0)),
                  pl.BlockSpec((1, 1, PAST_LEN, MLA_KVLORA), lambda b, j: (b, layer, 0, 0)),
                  pl.BlockSpec((1, 1, PAST_LEN, MLA_ROPE), lambda b, j: (b, layer, 0, 0))]
                 + [_full(t.shape) for t in tabs] + [_full(a.shape) for a in small],
        out_specs=pl.BlockSpec((LAT_QB, MIX), lambda b, j: (b * nq + j, 0)),
        out_shape=jax.ShapeDtypeStruct((nb * DEC_SEQ, MIX), F32),
        scratch_shapes=[pltpu.VMEM((LAT_KEYS, MLA_HEADS * LANES), BF16), pltpu.VMEM((LAT_KEYS, MIX), BF16)],
        compiler_params=_cparams(("arbitrary", "arbitrary")),
        name="mla_latent",
    )(zm, ckvc, krc, *tabs, *small)


def _s5_prep_kernel(bb_ref, qj_ref, m_ref):
    bb, qj = bb_ref[0], qj_ref[0]
    kt = _dot_nt_f32(bb[:, 0:LANES], qj[:, 0:LANES])
    kr = _dot_nt_f32(bb[:, LANES:2 * LANES], qj[:, LANES:2 * LANES])
    lane = lax.broadcasted_iota(jnp.int32, kt.shape, 1)
    blocks = []
    for s_pos in range(S5_CHUNK):
        sh_f = S5_GROUP * s_pos
        sh_b = S5_GROUP * (S5_CHUNK - 1 - s_pos)
        f = jnp.where(lane >= sh_f, pltpu.roll(kt, sh_f, axis=1), 0.0) if sh_f else kt
        b = jnp.where(lane < S5_ROW - sh_b, pltpu.roll(kr, S5_ROW - sh_b, axis=1), 0.0) if sh_b else kr
        blocks.append(f + b)
    m_ref[0] = jnp.concatenate(blocks, axis=0).astype(BF16)


def _s5_prep(lw):
    blk = lambda *shape: pl.BlockSpec((1,) + shape, lambda i: (i, 0, 0))
    return pl.pallas_call(
        _s5_prep_kernel,
        grid=(S5_GROUPS,),
        in_specs=[blk(S5_GROUP, S5_ROW), blk(S5_ROW, S5_ROW)],
        out_specs=blk(S5_ROW, S5_ROW),
        out_shape=jax.ShapeDtypeStruct((S5_GROUPS, S5_ROW, S5_ROW), BF16),
        compiler_params=_cparams(("arbitrary",)),
        name="s5_prep",
    )(lw["s5_bb"], lw["s5_qj"])


S5_TILE_BLOCKS = LANES // S5_GROUP


def _s5_kernel(nb, nch, ua_ref, ub_ref, m_ref, ws_ref, wct_ref, a_ref, d_ref, h0_ref, ya_ref, yb_ref, fin_ref,
               ug_scr, yg_scr, sf_scr, sb_scr, sfs_scr, sbs_scr, hf_scr, hb_scr):
    rows = nb * nch
    per = S5_TILE_BLOCKS
    lane_blk = lax.broadcasted_iota(jnp.int32, (rows, LANES), 1) // S5_GROUP
    u_refs = (ua_ref, ub_ref)
    y_refs = (ya_ref, yb_ref)

    def gather_blocks(pieces, src_blk):
        acc = None
        for i, (piece, sb) in enumerate(zip(pieces, src_blk)):
            shift = ((i - sb) * S5_GROUP) % LANES
            if shift:
                piece = pltpu.roll(piece, shift, axis=1)
            acc = piece if acc is None else jnp.where(lane_blk == i, piece, acc)
        return acc

    for g in range(S5_GROUPS):
        tiles = []
        for t in range(S5_CHUNK // per):
            pieces = [u_refs[g // per][pl.ds(t * per + i, rows, stride=S5_CHUNK), :] for i in range(per)]
            tiles.append(gather_blocks(pieces, [g % per] * per))
        ug_scr[g] = jnp.concatenate(tiles, axis=-1)

    def group(g, carry):
        x = ug_scr[g]
        xb = x.astype(BF16)
        s = _dot(xb, ws_ref[g])
        sf_scr[...] = s[:, 0:LANES]
        sb_scr[...] = s[:, LANES:2 * LANES]
        sfs_scr[...] = pltpu.roll(s[:, 0:LANES], S5_N, axis=1)
        sbs_scr[...] = pltpu.roll(s[:, LANES:2 * LANES], S5_N, axis=1)
        a = a_ref[g]
        h0 = h0_ref[g]
        h_f = h0[:, 0:LANES]
        h_b = h0[:, LANES:2 * LANES]
        g_f = pltpu.roll(h_f, S5_N, axis=1)
        g_b = pltpu.roll(h_b, S5_N, axis=1)
        for k in range(nch):
            rows_f = pl.ds(k, nb, stride=nch)
            rows_b = pl.ds(nch - 1 - k, nb, stride=nch)
            hf_scr[rows_f, :] = h_f
            hb_scr[rows_b, :] = h_b
            h_f, g_f = (a[0:1] * h_f + a[1:2] * g_f + sf_scr[rows_f, :],
                        a[0:1] * g_f + a[2:3] * h_f + sfs_scr[rows_f, :])
            h_b, g_b = (a[3:4] * h_b + a[4:5] * g_b + sb_scr[rows_b, :],
                        a[3:4] * g_b + a[5:6] * h_b + sbs_scr[rows_b, :])
        fin_ref[g] = jnp.concatenate([h_f, h_b], axis=-1)
        hp = jnp.concatenate([hf_scr[...], hb_scr[...]], axis=-1).astype(BF16)
        yg_scr[g] = _dot(xb, m_ref[g]) + _dot_nt(hp, wct_ref[g]) + d_ref[g] * x
        return carry

    lax.fori_loop(0, S5_GROUPS, group, 0)

    for s_pos in range(S5_CHUNK):
        t = s_pos // per
        for half in range(S5_GROUPS // per):
            pieces = [yg_scr[half * per + i, :, t * LANES:(t + 1) * LANES] for i in range(per)]
            y_refs[half][pl.ds(s_pos, rows, stride=S5_CHUNK), :] = gather_blocks(pieces, [s_pos % per] * per)


def _s5_mixer(u, m, lw, h0, nb, nb_step):
    n = u.shape[0]
    nch = n // nb // S5_CHUNK
    tok = nb_step * nch * S5_CHUNK
    rows = nb_step * nch
    g = S5_GROUPS
    res = lambda a: pl.BlockSpec(a.shape, lambda i: (0, 0, 0))
    state = pl.BlockSpec((g, nb_step, S5_ROW), lambda i: (0, i, 0))
    half = lambda j: pl.BlockSpec((tok, LANES), lambda i: (i, j))
    ya, yb, fin = pl.pallas_call(
        functools.partial(_s5_kernel, nb_step, nch),
        grid=(nb // nb_step,),
        in_specs=[half(0), half(1), res(m), res(lw["s5_ws"]), res(lw["s5_wct"]), res(lw["s5_a"]), res(lw["s5_d"]), state],
        out_specs=[half(0), half(0), state],
        out_shape=[jax.ShapeDtypeStruct((n, LANES), F32), jax.ShapeDtypeStruct((n, LANES), F32),
                   jax.ShapeDtypeStruct((g, nb, S5_ROW), F32)],
        scratch_shapes=[pltpu.VMEM((g, rows, S5_ROW), F32), pltpu.VMEM((g, rows, S5_ROW), F32)]
                       + [pltpu.VMEM((rows, LANES), F32)] * 6,
        compiler_params=_cparams(("arbitrary",)),
        name="s5_mixer",
    )(u, u, m, lw["s5_ws"], lw["s5_wct"], lw["s5_a"], lw["s5_d"], h0)
    return ya, yb, fin


def _outproj_kernel(ona_ref, ys5a_ref, ys5b_ref, ogq_ref, omla_ref, x_ref, mod_ref, w_ref, wglu_ref, bglu_ref,
                    o_ref):
    y = jax.nn.gelu(jnp.concatenate([ys5a_ref[...], ys5b_ref[...]], axis=-1), approximate=True)
    y = y * jax.nn.sigmoid(_dot(y.astype(BF16), wglu_ref[...]) + bglu_ref[...])
    mixed = _dot(ona_ref[...].astype(BF16), w_ref[0:MIX, :])
    mixed = mixed + _dot(y.astype(BF16), w_ref[MIX:2 * MIX, :])
    mixed = mixed + _dot(ogq_ref[...].astype(BF16), w_ref[2 * MIX:3 * MIX, :])
    mixed = mixed + _dot(omla_ref[...].astype(BF16), w_ref[3 * MIX:4 * MIX, :])
    o_ref[...] = x_ref[...] + mod_ref[0, 2:3, :] * mixed


def _outproj(ona, ys5a, ys5b, ogq, omla, x, mods, lw, mod_base, rows_per_mod):
    n = x.shape[0]
    tm = 512
    row = lambda i: (mod_base + (i * tm) // rows_per_mod, 0, 0)
    part = pl.BlockSpec((tm, MIX), lambda i: (i, 0))
    half = pl.BlockSpec((tm, LANES), lambda i: (i, 0))
    return pl.pallas_call(
        _outproj_kernel,
        grid=(n // tm,),
        in_specs=[part, half, half, part, part,
                  pl.BlockSpec((tm, D_MODEL), lambda i: (i, 0)),
                  pl.BlockSpec((1, 6, D_MODEL), row),
                  _full(lw["w_out"].shape), _full(lw["w_glu"].shape), _full(lw["b_glu"].shape)],
        out_specs=pl.BlockSpec((tm, D_MODEL), lambda i: (i, 0)),
        out_shape=jax.ShapeDtypeStruct((n, D_MODEL), F32),
        compiler_params=_cparams(("arbitrary",)),
        name="outproj",
    )(ona, ys5a, ys5b, ogq, omla, x, mods, lw["w_out"], lw["w_glu"], lw["b_glu"])


FF_ROWS = 1024
FFN_VMEM_LIMIT = 58 * 1024 * 1024


def _ffn_kernel(seq, x_ref, mod_ref, g_ref, wup_ref, cw_ref, cb_ref, wd_ref, o_ref, h_scr, ug_scr, uu_scr, act_scr):
    x = x_ref[...]
    y = x * lax.rsqrt(jnp.mean(x * x, axis=-1, keepdims=True) + EPS) * g_ref[...]
    h_scr[...] = (y * (1.0 + mod_ref[0, 4:5, :]) + mod_ref[0, 3:4, :]).astype(BF16)

    pos = lax.broadcasted_iota(jnp.int32, (FF_ROWS, FF_CHUNK), 0) % seq
    first = pos == 0
    last = pos == seq - 1

    def conv(u, cw, cb):
        prev = jnp.where(first, 0.0, pltpu.roll(u, 1, axis=0))
        nxt = jnp.where(last, 0.0, pltpu.roll(u, FF_ROWS - 1, axis=0))
        return cw[0:1, :] * prev + cw[1:2, :] * u + cw[2:3, :] * nxt + cb

    def cols(j, base=0):
        return pl.ds(pl.multiple_of(base + j * FF_CHUNK, FF_CHUNK), FF_CHUNK)

    def up_proj(j, slot):
        h = h_scr[...]
        ug_scr[slot] = _dot(h, wup_ref[:, cols(j)])
        uu_scr[slot] = _dot(h, wup_ref[:, cols(j, D_FF)])

    def gate_act(j, slot):
        gate = conv(ug_scr[slot], cw_ref[:, cols(j)], cb_ref[:, cols(j)])
        up = conv(uu_scr[slot], cw_ref[:, cols(j, D_FF)], cb_ref[:, cols(j, D_FF)])
        act_scr[slot] = (gate * jax.nn.sigmoid(gate) * up).astype(BF16)

    def down_proj(j, slot):
        return _dot(act_scr[slot], wd_ref[cols(j), :])

    last_j = FF_STEPS - 1
    up_proj(0, 0)
    up_proj(1, 1)
    gate_act(0, 0)
    up_proj(2, 0)
    gate_act(1, 1)
    o_ref[...] = down_proj(0, 0)

    def body(j, carry):
        slot = j % 2
        up_proj(j + 2, slot)
        gate_act(j + 1, 1 - slot)
        o_ref[...] += down_proj(j, slot)
        return carry

    lax.fori_loop(1, last_j - 1, body, 0)
    gate_act(last_j, last_j % 2)
    o_ref[...] += down_proj(last_j - 1, (last_j - 1) % 2)
    o_ref[...] = x_ref[...] + mod_ref[0, 5:6, :] * (o_ref[...] + down_proj(last_j, last_j % 2))


def _ffn(x, mods, lw, mod_base, rows_per_mod, seq):
    n = x.shape[0]
    row = lambda i: (mod_base + (i * FF_ROWS) // rows_per_mod, 0, 0)
    resident = lambda a: pl.BlockSpec(a.shape, lambda i: (0,) * a.ndim, pipeline_mode=pl.Buffered(1))
    return pl.pallas_call(
        functools.partial(_ffn_kernel, seq),
        grid=(n // FF_ROWS,),
        in_specs=[pl.BlockSpec((FF_ROWS, D_MODEL), lambda i: (i, 0)),
                  pl.BlockSpec((1, 6, D_MODEL), row),
                  pl.BlockSpec((1, D_MODEL), lambda i: (0, 0)),
                  resident(lw["w_up"]), resident(lw["conv_w"]), resident(lw["conv_b"]), resident(lw["w_down"])],
        out_specs=pl.BlockSpec((FF_ROWS, D_MODEL), lambda i: (i, 0)),
        out_shape=jax.ShapeDtypeStruct((n, D_MODEL), F32),
        scratch_shapes=[pltpu.VMEM((FF_ROWS, D_MODEL), BF16),
                        pltpu.VMEM((2, FF_ROWS, FF_CHUNK), F32), pltpu.VMEM((2, FF_ROWS, FF_CHUNK), F32),
                        pltpu.VMEM((2, FF_ROWS, FF_CHUNK), BF16)],
        compiler_params=pltpu.CompilerParams(dimension_semantics=("arbitrary",), vmem_limit_bytes=FFN_VMEM_LIMIT),
        name="conv_ffn",
    )(x, mods, lw["norm2_g"], lw["w_up"], lw["conv_w"], lw["conv_b"], lw["w_down"])


def _rope_tables():
    pos = np.arange(DEC_SEQ)

    def ang(p, half):
        inv = ROPE_BASE ** (-np.arange(half, dtype=np.float64) / half)
        a = p.astype(np.float64)[:, None] * inv[None, :]
        return np.concatenate([a, a], axis=-1)

    def tables(dim, lanes_before, lanes_after, reps):
        a = np.concatenate([ang(pos // GRID_W, dim // 4), ang(pos % GRID_W, dim // 4)], axis=-1)
        half = dim // 4
        first = (np.arange(dim) % (2 * half)) < half
        cos = np.cos(a)
        sin_a = np.where(first[None, :], -np.sin(a), 0.0)
        sin_b = np.where(first[None, :], 0.0, np.sin(a))

        def place(t, fill):
            t = np.concatenate([np.full((DEC_SEQ, lanes_before), fill), t,
                                np.full((DEC_SEQ, lanes_after), fill)], axis=-1)
            return jnp.asarray(np.tile(t, (1, reps)), F32)

        return place(cos, 1.0), place(sin_a, 0.0), place(sin_b, 0.0)

    gq = tables(DH, 0, 0, LANES // DH)
    mla = tables(MLA_ROPE, MLA_NOPE, LANES - MLA_QK, 1)
    return gq, mla


def _na_bias_tables(rpb):
    qc = np.arange(GRID_W)[:, None]
    kc = np.arange(GRID_W)[None, :]
    col_start = np.clip(qc - NA_WIN_C // 2, 0, GRID_W - NA_WIN_C)
    valid = (kc >= col_start) & (kc < col_start + NA_WIN_C)
    d_c = np.clip(kc - qc, 1 - NA_WIN_C, NA_WIN_C - 1) + NA_WIN_C - 1
    nrel = 2 * NA_WIN_C - 1
    onehot = jnp.asarray((d_c.reshape(-1)[None, :] == np.arange(nrel)[:, None]).astype(np.float32))
    t = jnp.dot(rpb.astype(F32).reshape(-1, nrel), onehot, precision=lax.Precision.HIGHEST)
    t = t.reshape(NA_HEADS, 2 * NA_WIN_R - 1, GRID_W, GRID_W)
    t = jnp.where(jnp.asarray(valid)[None, None], t, NEG)
    t = t.transpose(1, 0, 2, 3).reshape(2 * NA_WIN_R - 1, NA_HEADS * GRID_W, GRID_W)
    return jnp.concatenate([t[:-1], t[1:]], axis=-1)


def _s5_matrices(lam_re, lam_im, log_dt, b_re, b_im, c_re, c_im):
    c = S5_CHUNK
    cat = lambda parts: jnp.concatenate(parts, axis=-1)
    pw, bre, bim, cre, cim = [], [], [], [], []
    for d in range(2):
        dt = jnp.exp(log_dt[d].astype(F32))[:, None]
        lr, li = lam_re[d].astype(F32), lam_im[d].astype(F32)
        mag = jnp.exp(lr * dt)
        a_re, a_im = mag * jnp.cos(li * dt), mag * jnp.sin(li * dt)
        den = lr * lr + li * li
        f_re = ((a_re - 1.0) * lr + a_im * li) / den
        f_im = (a_im * lr - (a_re - 1.0) * li) / den
        br, bi = b_re[d].astype(F32), b_im[d].astype(F32)
        bre.append((f_re[..., None] * br - f_im[..., None] * bi).transpose(0, 2, 1))
        bim.append((f_re[..., None] * bi + f_im[..., None] * br).transpose(0, 2, 1))
        cre.append(c_re[d].astype(F32))
        cim.append(c_im[d].astype(F32))
        p_re, p_im = [jnp.ones_like(a_re)], [jnp.zeros_like(a_re)]
        for _ in range(c):
            p_re, p_im = (p_re + [p_re[-1] * a_re - p_im[-1] * a_im], p_im + [p_re[-1] * a_im + p_im[-1] * a_re])
        p_re, p_im = jnp.stack(p_re, axis=1), jnp.stack(p_im, axis=1)
        pw.append((p_re, p_im))
    up, dn = np.arange(c), np.arange(c)[::-1]
    (pfr, pfi), (pbr, pbi) = pw
    outer = lambda x, y: (x[:, :, None, :] * y[:, None, :, :])
    e_fr, e_fi, e_br, e_bi = pfr[:, dn], pfi[:, dn], pbr[:, up], pbi[:, up]
    bb = cat([bre[0], bim[0], bre[1], bim[1]])
    ws = (outer(cat([e_fr, e_fr, e_br, e_br]), bb)
          + outer(cat([-e_fi, e_fi, -e_bi, e_bi]), cat([bim[0], bre[0], bim[1], bre[1]])))
    e_fr, e_fi, e_br, e_bi = pfr[:, up + 1], pfi[:, up + 1], pbr[:, dn + 1], pbi[:, dn + 1]
    c_a, c_b = cat([cre[0], cre[0], cre[1], cre[1]]), cat([cim[0], cim[0], cim[1], cim[1]])
    wct = outer(cat([e_fr, -e_fi, e_br, -e_bi]), c_a) + outer(cat([-e_fi, -e_fr, -e_bi, -e_br]), c_b)
    e_fr, e_fi, e_br, e_bi = pfr[:, up], pfi[:, up], pbr[:, dn], pbi[:, dn]
    qj = outer(cat([e_fr, -e_fi, e_br, -e_bi]), c_a) + outer(cat([-e_fi, -e_fr, -e_bi, -e_br]), c_b)
    rows = lambda w: w.reshape(S5_GROUPS, S5_ROW, 4 * S5_N).astype(BF16)
    av = []
    for p_re, p_im in ((pfr[:, c], pfi[:, c]), (pbr[:, c], pbi[:, c])):
        av += [cat([p_re, p_re]), cat([-p_im, p_im]), cat([p_im, -p_im])]
    return rows(ws), rows(wct), bb, qj.reshape(S5_GROUPS, S5_ROW, 4 * S5_N), jnp.stack(av, axis=1)


def _layer_weights(l, p):
    w = p["w_in"][l]
    q0 = Z_NA + Z_S5
    gq = lambda h: w[:, q0 + h * DH:q0 + (h + 1) * DH]
    m0 = q0 + 2 * MIX
    w_in = jnp.concatenate(
        [w[:, 0:q0], gq(0), gq(2), gq(1), gq(3), w[:, q0 + MIX:m0],
         w[:, m0 + MLA_QLORA:m0 + MLA_QLORA + MLA_KVLORA], w[:, m0:m0 + MLA_QLORA],
         w[:, m0 + MLA_QLORA + MLA_KVLORA:], jnp.zeros((D_MODEL, Z_ALL - w.shape[1]), w.dtype)],
        axis=1).astype(BF16)
    wo = p["w_out"][l]
    og = lambda h: wo[2 * MIX + h * DH:2 * MIX + (h + 1) * DH]
    w_out = jnp.concatenate([wo[0:2 * MIX], og(0), og(2), og(1), og(3), wo[3 * MIX:]], axis=0).astype(BF16)
    pad_head = lambda g: jnp.tile(jnp.pad(g, (0, LANES - MLA_QK)), MLA_HEADS)[None, :]
    wuq = jnp.pad(p["mla_w_uq"][l].reshape(MLA_QLORA, MLA_HEADS, MLA_QK),
                  ((0, 2 * LANES - MLA_QLORA), (0, 0), (0, LANES - MLA_QK))).reshape(2 * LANES, MLA_HEADS * LANES)
    wukv = p["mla_w_ukv"][l].reshape(MLA_KVLORA, MLA_HEADS, MLA_NOPE + DH)
    wuk = jnp.pad(wukv[:, :, :MLA_NOPE], ((0, 0), (0, 0), (0, LANES - MLA_NOPE))).reshape(MLA_KVLORA, MLA_HEADS * LANES)
    wuv = wukv[:, :, MLA_NOPE:].reshape(MLA_KVLORA, MIX)
    s5_ws, s5_wct, s5_bb, s5_qj, s5_a = _s5_matrices(p["s5_lam_re"][l], p["s5_lam_im"][l], p["s5_log_dt"][l],
                                                     p["s5_b_re"][l], p["s5_b_im"][l], p["s5_c_re"][l], p["s5_c_im"][l])
    return dict(
        norm1_g=p["norm1_g"][l][None, :], norm2_g=p["norm2_g"][l][None, :],
        w_in=w_in, w_out=w_out,
        naq_g=jnp.tile(p["na_qn"][l], NA_HEADS)[None, :], nak_g=jnp.tile(p["na_kn"][l], NA_HEADS)[None, :],
        gqq_g=jnp.tile(p["gq_qn"][l], GQ_HEADS)[None, :], gqk_g=jnp.tile(p["gq_kn"][l], GQ_KV)[None, :],
        qa_g=jnp.pad(p["mla_qa_g"][l], (0, 2 * LANES - MLA_QLORA))[None, :], kva_g=p["mla_kva_g"][l][None, :],
        mqn_g=pad_head(p["mla_qn"][l]), mkn_g=pad_head(p["mla_kn"][l]),
        wuq=wuq.astype(BF16), wuk=wuk.astype(BF16), wuv=wuv.astype(BF16),
        na_bias=_na_bias_tables(p["na_rpb"][l]),
        s5_ws=s5_ws, s5_wct=s5_wct, s5_bb=s5_bb, s5_qj=s5_qj, s5_a=s5_a,
        s5_d=jnp.tile(p["s5_d"][l].reshape(S5_GROUPS, 1, S5_GROUP), (1, 1, S5_CHUNK)),
        w_glu=p["s5_w_glu"][l].astype(BF16), b_glu=p["s5_b_glu"][l][None, :],
        w_up=p["ffn_w_up"][l].astype(BF16), conv_w=p["ffn_conv_w"][l], conv_b=p["ffn_conv_b"][l][None, :],
        w_down=p["ffn_w_down"][l].astype(BF16),
    )


def kernel(x_prompt, x_sample, cache_na_k, cache_na_v, state_s5, cache_gqa_k, cache_gqa_v, cache_mla_ckv,
           cache_mla_krope, c, c_ctx, norm1_g, norm2_g, ada_w, ada_b, w_in, na_qn, na_kn, na_rpb, s5_lam_re,
           s5_lam_im, s5_log_dt, s5_b_re, s5_b_im, s5_c_re, s5_c_im, s5_d, s5_w_glu, s5_b_glu, gq_qn, gq_kn,
           mla_qa_g, mla_kva_g, mla_w_uq, mla_w_ukv, mla_qn, mla_kn, w_out, ffn_w_up, ffn_conv_w, ffn_conv_b,
           ffn_w_down):
    p = dict(norm1_g=norm1_g, norm2_g=norm2_g, w_in=w_in, na_qn=na_qn, na_kn=na_kn, na_rpb=na_rpb,
             s5_lam_re=s5_lam_re, s5_lam_im=s5_lam_im, s5_log_dt=s5_log_dt, s5_b_re=s5_b_re, s5_b_im=s5_b_im,
             s5_c_re=s5_c_re, s5_c_im=s5_c_im, s5_d=s5_d, s5_w_glu=s5_w_glu, s5_b_glu=s5_b_glu,
             gq_qn=gq_qn, gq_kn=gq_kn, mla_qa_g=mla_qa_g, mla_kva_g=mla_kva_g, mla_w_uq=mla_w_uq,
             mla_w_ukv=mla_w_ukv, mla_qn=mla_qn, mla_kn=mla_kn, w_out=w_out, ffn_w_up=ffn_w_up,
             ffn_conv_w=ffn_conv_w, ffn_conv_b=ffn_conv_b, ffn_w_down=ffn_w_down)
    nb, nd = BATCH, DEC_BATCH
    cvec = jnp.concatenate([c_ctx[None, :], c, jnp.zeros((8 - 1 - nd, D_MODEL), F32)], axis=0)
    mods_all = _ada_mods(cvec, ada_w, ada_b).reshape(DEPTH, 8, 6, D_MODEL)
    gq_tabs, mla_tabs = _rope_tables()

    xc = x_prompt.reshape(nb * SEQ, D_MODEL)
    xl = x_sample.reshape(nd * DEC_SEQ, D_MODEL)
    zero_state = jnp.zeros((S5_GROUPS, nb, 4 * S5_N), F32)
    caches = [jnp.zeros((nb, DEPTH) + shp, F32) for shp in CACHE_SHAPES]
    s5_states = []
    for l in range(DEPTH):
        lw = _layer_weights(l, p)
        mods = mods_all[l]
        s5_m = _s5_prep(lw)

        zna, zs5, zgq, zm = _inproj(xc, mods, lw["norm1_g"], lw["w_in"], 0, nb * SEQ)
        (ona, ogq, omla), caches = _ctx_attn(zna, zgq, zm, lw, l, caches)
        ys5a, ys5b, fin = _s5_mixer(zs5, s5_m, lw, zero_state, nb, S5_CTX_BLOCK)
        xc = _outproj(ona, ys5a, ys5b, ogq, omla, xc, mods, lw, 0, nb * SEQ)
        xc = _ffn(xc, mods, lw, 0, nb * SEQ, SEQ)
        s5_states.append(fin.reshape(S5_GROUPS, nb, 2, 2, S5_N).transpose(1, 2, 3, 0, 4))

        zna, zs5, zgq, zm = _inproj(xl, mods, lw["norm1_g"], lw["w_in"], 1, DEC_SEQ)
        ona = _na_lat(zna, cache_na_k, cache_na_v, l, lw["na_bias"], lw)
        ogq = _gq_lat(zgq, cache_gqa_k, cache_gqa_v, l, gq_tabs, lw)
        omla = _mla_lat(zm, cache_mla_ckv, cache_mla_krope, l, mla_tabs, lw)
        h0 = state_s5[:, l].astype(F32).transpose(3, 0, 1, 2, 4).reshape(S5_GROUPS, nd, 4 * S5_N)
        ys5a, ys5b, _ = _s5_mixer(zs5, s5_m, lw, h0, nd, nd)
        xl = _outproj(ona, ys5a, ys5b, ogq, omla, xl, mods, lw, 1, DEC_SEQ)
        xl = _ffn(xl, mods, lw, 1, DEC_SEQ, DEC_SEQ)

    return (xc.reshape(nb, SEQ, D_MODEL), xl.reshape(nd, DEC_SEQ, D_MODEL), caches[0], caches[1],
            jnp.stack(s5_states, axis=1), caches[2], caches[3], caches[4], caches[5])
```

```python
import functools

import numpy as np
import jax
import jax.numpy as jnp
from jax import lax
from jax.experimental import pallas as pl
from jax.experimental.pallas import tpu as pltpu

F32 = jnp.float32
BF16 = jnp.bfloat16

D_MODEL = 1024
BATCH = 32
SEQ = 256
DEPTH = 2
DEC_BATCH = 4
DEC_SEQ = 1024
PAST_LEN = 512
GRID_W = 64
GRID_ROWS = DEC_SEQ // GRID_W
MIX = D_MODEL // 4
DH = 64
NA_HEADS = MIX // DH
NA_WIN_R = 8
NA_WIN_C = 16
S5_GROUP = 16
S5_GROUPS = MIX // S5_GROUP
S5_N = 64
GQ_HEADS = MIX // DH
GQ_KV = GQ_HEADS // 2
MLA_HEADS = MIX // DH
MLA_NOPE = 64
MLA_ROPE = 32
MLA_QK = MLA_NOPE + MLA_ROPE
MLA_QLORA = (3 * D_MODEL) // 16
MLA_KVLORA = D_MODEL // 8
D_FF = 128 * ((8 * D_MODEL // 3 + 127) // 128)
ROPE_BASE = 10000.0
EPS = 1e-6
NEG = -1e30

LANES = 128
MXU_DIM = 256
VMEM_LIMIT = 48 * 1024 * 1024
MXU_DEN_MIN_KEYS = 1024

Z_NA = 3 * MIX
Z_S5 = MIX
Z_GQ = 2 * MIX
Z_MLA = 3 * LANES
Z_ALL = Z_NA + Z_S5 + Z_GQ + Z_MLA

S5_CHUNK = 16
S5_ROW = S5_CHUNK * S5_GROUP
S5_CTX_BLOCK = 8

FF_CHUNK = 256
FF_STEPS = D_FF // FF_CHUNK


def _cparams(sem):
    return pltpu.CompilerParams(dimension_semantics=sem, vmem_limit_bytes=VMEM_LIMIT)


def _dot(a, b):
    return jnp.dot(a, b, preferred_element_type=F32)


def _dot_nt(a, b):
    return lax.dot_general(a, b, (((1,), (1,)), ((), ())), preferred_element_type=F32)


def _dot_nt_f32(a, b):
    def split(x):
        hi = x.astype(BF16)
        r1 = x - hi.astype(F32)
        mid = r1.astype(BF16)
        return hi, mid, (r1 - mid.astype(F32)).astype(BF16)

    pa, pb = split(a), split(b)
    acc = None
    for i in range(3):
        for j in range(3 - i):
            term = _dot_nt(pa[i], pb[j])
            acc = term if acc is None else acc + term
    return acc


def _rms_rows(x, gain, denom=None, valid=None):
    xx = x * x
    if valid is not None and valid != x.shape[-1]:
        lane = lax.broadcasted_iota(jnp.int32, x.shape, 1)
        xx = jnp.where(lane < valid, xx, 0.0)
    denom = denom or (valid or x.shape[-1])
    ss = jnp.sum(xx, axis=-1, keepdims=True)
    return x * lax.rsqrt(ss / denom + EPS) * gain


def _seg_rms(x, seg, denom, gain):
    rows, width = x.shape
    if seg % LANES == 0:
        parts = []
        for s in range(width // seg):
            xs = x[:, s * seg:(s + 1) * seg]
            ss = jnp.sum(xs * xs, axis=-1, keepdims=True)
            parts.append(xs * lax.rsqrt(ss / denom + EPS))
        return jnp.concatenate(parts, axis=-1) * gain
    same_seg = (lax.broadcasted_iota(jnp.int32, (width, width), 0) // seg
                == lax.broadcasted_iota(jnp.int32, (width, width), 1) // seg)
    ones = jnp.where(same_seg, 1.0, 0.0).astype(BF16)
    xx = x * x
    hi = xx.astype(BF16)
    lo = (xx - hi.astype(F32)).astype(BF16)
    ss = _dot(hi, ones) + _dot(lo, ones)
    return x * lax.rsqrt(ss / denom + EPS) * gain


def _rope(x, cos, sin_a, sin_b, half):
    tiles = []
    for t in range(x.shape[-1] // LANES):
        xt = x[:, t * LANES:(t + 1) * LANES]
        up = pltpu.roll(xt, LANES - half, axis=1)
        dn = pltpu.roll(xt, half, axis=1)
        tiles.append(xt * cos + up * sin_a + dn * sin_b)
    return tiles[0] if len(tiles) == 1 else jnp.concatenate(tiles, axis=-1)


def _packed_attn(q, parts, nseg, scale, biases=None):
    tq, width = q.shape
    seg_id = lax.broadcasted_iota(jnp.int32, q.shape, 1) // DH
    lhs = jnp.concatenate([jnp.where(seg_id == h, q, 0.0) for h in range(nseg)], axis=0).astype(BF16)
    scores = []
    for i, (kb, _) in enumerate(parts):
        s = _dot_nt(lhs, kb) * scale
        if biases is not None and biases[i] is not None:
            s = s + biases[i]
        scores.append(s)
    m = scores[0].max(axis=-1, keepdims=True)
    for s in scores[1:]:
        m = jnp.maximum(m, s.max(axis=-1, keepdims=True))
    mxu_den = sum(kb.shape[0] for kb, _ in parts) >= MXU_DEN_MIN_KEYS
    den = None
    pv = None
    for s, (_, vb) in zip(scores, parts):
        p = jnp.exp(s - m)
        pb = p.astype(BF16)
        ps = _dot(pb, jnp.ones(vb.shape, BF16)) if mxu_den else p.sum(axis=-1, keepdims=True)
        den = ps if den is None else den + ps
        c = _dot(pb, vb)
        pv = c if pv is None else pv + c
    pv = pv / den
    out = jnp.zeros((tq, width), F32)
    for h in range(nseg):
        out = out + jnp.where(seg_id == h, pv[h * tq:(h + 1) * tq], 0.0)
    return out


def _mla_attn(q, parts, scale):
    tq = q.shape[0]
    vseg = None
    out = jnp.zeros((tq, MIX), F32)
    for h in range(MLA_HEADS):
        qh = q[:, h * LANES:(h + 1) * LANES].astype(BF16)
        scores = [_dot_nt(qh, kb[:, h * LANES:(h + 1) * LANES]) * scale for kb, _ in parts]
        m = scores[0].max(axis=-1, keepdims=True)
        for s in scores[1:]:
            m = jnp.maximum(m, s.max(axis=-1, keepdims=True))
        den = None
        pv = None
        for s, (_, vb) in zip(scores, parts):
            p = jnp.exp(s - m)
            ps = p.sum(axis=-1, keepdims=True)
            den = ps if den is None else den + ps
            vseg = lax.broadcasted_iota(jnp.int32, vb.shape, 1) // DH
            c = _dot(p.astype(BF16), jnp.where(vseg == h, vb, jnp.zeros_like(vb)))
            pv = c if pv is None else pv + c
        out = out + pv / den
    return out


def _mla_qkv(zm, qa_g, kva_g, mqn_g, mkn_g, wuq, wuk, wuv):
    ckv = _rms_rows(zm[:, 0:LANES], kva_g)
    col3 = zm[:, 2 * LANES:3 * LANES]
    lane = lax.broadcasted_iota(jnp.int32, col3.shape, 1)
    kr_placed = jnp.where((lane >= MLA_NOPE) & (lane < MLA_QK), col3, 0.0)
    cq = _rms_rows(zm[:, LANES:3 * LANES], qa_g, valid=MLA_QLORA)
    q = _seg_rms(_dot(cq.astype(BF16), wuq), LANES, MLA_QK, mqn_g)
    k, v = _mla_kv(ckv, kr_placed, mkn_g, wuk, wuv)
    return q, k, v, ckv, col3


def _mla_kv(ckv, kr_placed, mkn_g, wuk, wuv):
    cb = ckv.astype(BF16)
    kf = _dot(cb, wuk) + jnp.concatenate([kr_placed] * MLA_HEADS, axis=-1)
    return _seg_rms(kf, LANES, MLA_QK, mkn_g), _dot(cb, wuv)


def _ada_kernel(c_ref, w_ref, b_ref, o_ref):
    cv = c_ref[...]
    s = cv * jax.nn.sigmoid(cv)
    o_ref[0] = _dot(s.astype(BF16), w_ref[0].astype(BF16)) + b_ref[0]


def _ada_mods(cvec, ada_w, ada_b):
    tn = 1536
    n = ada_w.shape[-1]
    return pl.pallas_call(
        _ada_kernel,
        grid=(DEPTH, n // tn),
        in_specs=[pl.BlockSpec((8, D_MODEL), lambda l, j: (0, 0)),
                  pl.BlockSpec((1, D_MODEL, tn), lambda l, j: (l, 0, j)),
                  pl.BlockSpec((1, 1, tn), lambda l, j: (l, 0, j))],
        out_specs=pl.BlockSpec((1, 8, tn), lambda l, j: (l, 0, j)),
        out_shape=jax.ShapeDtypeStruct((DEPTH, 8, n), F32),
        compiler_params=_cparams(("arbitrary", "arbitrary")),
        name="ada_mods",
    )(cvec, ada_w, ada_b.reshape(DEPTH, 1, n))


def _inproj_kernel(x_ref, mod_ref, g_ref, w_ref, ona_ref, os5_ref, ogq_ref, omla_ref):
    x = x_ref[...]
    y = x * lax.rsqrt(jnp.mean(x * x, axis=-1, keepdims=True) + EPS) * g_ref[...]
    h = y * (1.0 + mod_ref[0, 1:2, :]) + mod_ref[0, 0:1, :]
    z = _dot(h.astype(BF16), w_ref[...])
    ona_ref[...] = z[:, 0:Z_NA]
    os5_ref[...] = z[:, Z_NA:Z_NA + Z_S5]
    ogq_ref[...] = z[:, Z_NA + Z_S5:Z_NA + Z_S5 + Z_GQ]
    omla_ref[...] = z[:, Z_NA + Z_S5 + Z_GQ:Z_ALL]


def _inproj(x, mods, g, w, mod_base, rows_per_mod):
    n = x.shape[0]
    tm = 512
    row = lambda i: (mod_base + (i * tm) // rows_per_mod, 0, 0)
    widths = (Z_NA, Z_S5, Z_GQ, Z_MLA)
    return pl.pallas_call(
        _inproj_kernel,
        grid=(n // tm,),
        in_specs=[pl.BlockSpec((tm, D_MODEL), lambda i: (i, 0)),
                  pl.BlockSpec((1, 6, D_MODEL), row),
                  pl.BlockSpec((1, D_MODEL), lambda i: (0, 0)),
                  pl.BlockSpec((D_MODEL, Z_ALL), lambda i: (0, 0))],
        out_specs=[pl.BlockSpec((tm, wd), lambda i: (i, 0)) for wd in widths],
        out_shape=[jax.ShapeDtypeStruct((n, wd), F32) for wd in widths],
        compiler_params=_cparams(("arbitrary",)),
        name="inproj",
    )(x, mods, g, w)


def _store_heads(ref, x, heads):
    for h in range(heads):
        ref[0, 0, h] = x[:, h * DH:(h + 1) * DH]


def _ctx_attn_kernel(zna_ref, zgq_ref, zm_ref, naq_g, nak_g, gqq_g, gqk_g, qa_g, kva_g, mqn_g, mkn_g,
                     wuq_ref, wuk_ref, wuv_ref, *refs):
    ona_ref, ogq_ref, omla_ref, nk_ref, nv_ref, gk_ref, gv_ref, ckv_ref, kr_ref = refs[-9:]
    zna = zna_ref[...]
    q = _seg_rms(zna[:, 0:MIX], DH, DH, naq_g[...])
    k = _seg_rms(zna[:, MIX:2 * MIX], DH, DH, nak_g[...])
    v = zna[:, 2 * MIX:3 * MIX]
    _store_heads(nk_ref, k, NA_HEADS)
    _store_heads(nv_ref, v, NA_HEADS)
    ona_ref[...] = _packed_attn(q, [(k.astype(BF16), v.astype(BF16))], NA_HEADS, DH ** -0.5)

    zgq = zgq_ref[...]
    gq = _seg_rms(zgq[:, 0:MIX], DH, DH, gqq_g[...])
    gk = _seg_rms(zgq[:, MIX:MIX + LANES], DH, DH, gqk_g[...])
    gv = zgq[:, MIX + LANES:2 * MIX]
    _store_heads(gk_ref, gk, GQ_KV)
    _store_heads(gv_ref, gv, GQ_KV)
    kv = [(gk.astype(BF16), gv.astype(BF16))]
    ogq_ref[...] = jnp.concatenate(
        [_packed_attn(gq[:, r * LANES:(r + 1) * LANES], kv, GQ_KV, DH ** -0.5) for r in range(2)], axis=-1)

    mq, mk, mv, ckv, col3 = _mla_qkv(zm_ref[...], qa_g[...], kva_g[...], mqn_g[...], mkn_g[...],
                                     wuq_ref[...], wuk_ref[...], wuv_ref[...])
    ckv_ref[0, 0] = ckv
    kr_ref[0, 0] = pltpu.roll(col3, LANES - MLA_NOPE, axis=1)[:, 0:MLA_ROPE]
    omla_ref[...] = _mla_attn(mq, [(mk.astype(BF16), mv.astype(BF16))], MLA_QK ** -0.5)


def _full(shape):
    nd = len(shape)
    return pl.BlockSpec(shape, lambda *a: (0,) * nd)


CACHE_SHAPES = ((NA_HEADS, SEQ, DH), (NA_HEADS, SEQ, DH), (GQ_KV, SEQ, DH), (GQ_KV, SEQ, DH),
                (SEQ, MLA_KVLORA), (SEQ, MLA_ROPE))


def _ctx_attn(zna, zgq, zm, lw, layer, caches):
    n = zna.shape[0]
    t = SEQ
    rows = lambda wd: pl.BlockSpec((t, wd), lambda b: (b, 0))
    cache_spec = lambda shp: pl.BlockSpec((1, 1) + shp, lambda b: (b, layer) + (0,) * len(shp))
    gains = [lw["naq_g"], lw["nak_g"], lw["gqq_g"], lw["gqk_g"], lw["qa_g"], lw["kva_g"], lw["mqn_g"], lw["mkn_g"]]
    weights = [lw["wuq"], lw["wuk"], lw["wuv"]]
    n_in = 3 + len(gains) + len(weights)
    outs = pl.pallas_call(
        _ctx_attn_kernel,
        grid=(n // t,),
        in_specs=[rows(Z_NA), rows(Z_GQ), rows(Z_MLA)] + [_full(a.shape) for a in gains + weights]
                 + [pl.BlockSpec(memory_space=pl.ANY)] * len(caches),
        out_specs=[rows(MIX)] * 3 + [cache_spec(shp) for shp in CACHE_SHAPES],
        out_shape=[jax.ShapeDtypeStruct((n, MIX), F32)] * 3
                  + [jax.ShapeDtypeStruct((n // t, DEPTH) + shp, F32) for shp in CACHE_SHAPES],
        input_output_aliases={n_in + i: 3 + i for i in range(len(caches))},
        compiler_params=_cparams(("arbitrary",)),
        name="ctx_attn",
    )(zna, zgq, zm, *gains, *weights, *caches)
    return outs[:3], list(outs[3:])


def _na_window_start(r):
    return jnp.clip(r - NA_WIN_R // 2, 0, GRID_ROWS - NA_WIN_R)


def _pack_heads(ref):
    return jnp.concatenate([ref[0, 0, h] for h in range(ref.shape[2])], axis=-1)


def _na_lat_kernel(zna_ref, kc_ref, vc_ref, bias_ref, naq_g, nak_g, o_ref, k_scr, v_scr, kc_scr, vc_scr):
    r = pl.program_id(1)

    @pl.when(r == 0)
    def _():
        k_scr[...] = _seg_rms(zna_ref[:, MIX:2 * MIX], DH, DH, nak_g[...]).astype(BF16)
        v_scr[...] = zna_ref[:, 2 * MIX:3 * MIX].astype(BF16)
        kc_scr[...] = _pack_heads(kc_ref).astype(BF16)
        vc_scr[...] = _pack_heads(vc_ref).astype(BF16)

    q = _seg_rms(zna_ref[pl.ds(pl.multiple_of(r * GRID_W, GRID_W), GRID_W), 0:MIX], DH, DH, naq_g[...])
    start = pl.multiple_of(_na_window_start(r) * GRID_W, GRID_W)
    span = NA_WIN_R * GRID_W
    parts = [(k_scr[pl.ds(start, span), :], v_scr[pl.ds(start, span), :]),
             (kc_scr[...], vc_scr[...])]
    d0 = _na_window_start(r) - r + NA_WIN_R - 1
    bias = jnp.concatenate([bias_ref[d0 + 2 * i] for i in range(NA_WIN_R // 2)], axis=-1)
    o_ref[...] = _packed_attn(q, parts, NA_HEADS, DH ** -0.5, biases=[bias, None])


def _na_lat(zna, kc, vc, layer, bias, lw):
    nb = DEC_BATCH
    cache = pl.BlockSpec((1, 1, NA_HEADS, PAST_LEN, DH), lambda b, r: (b, layer, 0, 0, 0))
    return pl.pallas_call(
        _na_lat_kernel,
        grid=(nb, GRID_ROWS),
        in_specs=[pl.BlockSpec((DEC_SEQ, Z_NA), lambda b, r: (b, 0)),
                  cache, cache,
                  _full(bias.shape),
                  _full(lw["naq_g"].shape), _full(lw["nak_g"].shape)],
        out_specs=pl.BlockSpec((GRID_W, MIX), lambda b, r: (b * GRID_ROWS + r, 0)),
        out_shape=jax.ShapeDtypeStruct((nb * DEC_SEQ, MIX), F32),
        scratch_shapes=[pltpu.VMEM((DEC_SEQ, MIX), BF16), pltpu.VMEM((DEC_SEQ, MIX), BF16),
                        pltpu.VMEM((PAST_LEN, MIX), BF16), pltpu.VMEM((PAST_LEN, MIX), BF16)],
        compiler_params=_cparams(("arbitrary", "arbitrary")),
        name="na_latent",
    )(zna, kc, vc, bias, lw["naq_g"], lw["nak_g"])


LAT_QB = 256
LAT_KEYS = DEC_SEQ + PAST_LEN


def _gq_lat_kernel(zgq_ref, kc_ref, vc_ref, cos_ref, sa_ref, sb_ref, gqq_g, gqk_g, o_ref, k_scr, v_scr):
    j = pl.program_id(1)

    @pl.when(j == 0)
    def _():
        k = _seg_rms(zgq_ref[:, MIX:MIX + LANES], DH, DH, gqk_g[...])
        k = _rope(k, cos_ref[...], sa_ref[...], sb_ref[...], DH // 4)
        k_scr[0:DEC_SEQ, :] = k.astype(BF16)
        k_scr[DEC_SEQ:LAT_KEYS, :] = _pack_heads(kc_ref).astype(BF16)
        v_scr[0:DEC_SEQ, :] = zgq_ref[:, MIX + LANES:2 * MIX].astype(BF16)
        v_scr[DEC_SEQ:LAT_KEYS, :] = _pack_heads(vc_ref).astype(BF16)

    rows = pl.ds(pl.multiple_of(j * LAT_QB, LAT_QB), LAT_QB)
    q = _seg_rms(zgq_ref[rows, 0:MIX], DH, DH, gqq_g[...])
    q = _rope(q, cos_ref[rows, :], sa_ref[rows, :], sb_ref[rows, :], DH // 4)
    kv = [(k_scr[...], v_scr[...])]
    o_ref[...] = jnp.concatenate(
        [_packed_attn(q[:, r * LANES:(r + 1) * LANES], kv, GQ_KV, DH ** -0.5) for r in range(2)], axis=-1)


def _gq_lat(zgq, kc, vc, layer, tabs, lw):
    nb = DEC_BATCH
    nq = DEC_SEQ // LAT_QB
    cache = pl.BlockSpec((1, 1, GQ_KV, PAST_LEN, DH), lambda b, j: (b, layer, 0, 0, 0))
    return pl.pallas_call(
        _gq_lat_kernel,
        grid=(nb, nq),
        in_specs=[pl.BlockSpec((DEC_SEQ, Z_GQ), lambda b, j: (b, 0)), cache, cache]
                 + [_full(t.shape) for t in tabs] + [_full(lw["gqq_g"].shape), _full(lw["gqk_g"].shape)],
        out_specs=pl.BlockSpec((LAT_QB, MIX), lambda b, j: (b * nq + j, 0)),
        out_shape=jax.ShapeDtypeStruct((nb * DEC_SEQ, MIX), F32),
        scratch_shapes=[pltpu.VMEM((LAT_KEYS, LANES), BF16), pltpu.VMEM((LAT_KEYS, LANES), BF16)],
        compiler_params=_cparams(("arbitrary", "arbitrary")),
        name="gq_latent",
    )(zgq, kc, vc, *tabs, lw["gqq_g"], lw["gqk_g"])


def _mla_lat_kernel(zm_ref, ckvc_ref, krc_ref, cos_ref, sa_ref, sb_ref, qa_g, kva_g, mqn_g, mkn_g,
                    wuq_ref, wuk_ref, wuv_ref, o_ref, k_scr, v_scr):
    j = pl.program_id(1)
    half = MLA_ROPE // 4

    @pl.when(j == 0)
    def _():
        zm = zm_ref[...]
        ckv = _rms_rows(zm[:, 0:LANES], kva_g[...])
        col3 = zm[:, 2 * LANES:3 * LANES]
        lane = lax.broadcasted_iota(jnp.int32, col3.shape, 1)
        kr_placed = jnp.where((lane >= MLA_NOPE) & (lane < MLA_QK), col3, 0.0)
        k, v = _mla_kv(ckv, kr_placed, mkn_g[...], wuk_ref[...], wuv_ref[...])
        k = _rope(k, cos_ref[...], sa_ref[...], sb_ref[...], half)
        k_scr[0:DEC_SEQ, :] = k.astype(BF16)
        v_scr[0:DEC_SEQ, :] = v.astype(BF16)
        krc = jnp.concatenate([jnp.zeros((PAST_LEN, MLA_NOPE), F32), krc_ref[0, 0],
                               jnp.zeros((PAST_LEN, LANES - MLA_QK), F32)], axis=-1)
        kc, vc = _mla_kv(ckvc_ref[0, 0], krc, mkn_g[...], wuk_ref[...], wuv_ref[...])
        k_scr[DEC_SEQ:LAT_KEYS, :] = kc.astype(BF16)
        v_scr[DEC_SEQ:LAT_KEYS, :] = vc.astype(BF16)

    rows = pl.ds(pl.multiple_of(j * LAT_QB, LAT_QB), LAT_QB)
    cq = _rms_rows(zm_ref[rows, LANES:3 * LANES], qa_g[...], valid=MLA_QLORA)
    q = _seg_rms(_dot(cq.astype(BF16), wuq_ref[...]), LANES, MLA_QK, mqn_g[...])
    q = _rope(q, cos_ref[rows, :], sa_ref[rows, :], sb_ref[rows, :], half)
    o_ref[...] = _mla_attn(q, [(k_scr[...], v_scr[...])], MLA_QK ** -0.5)


def _mla_lat(zm, ckvc, krc, layer, tabs, lw):
    nb = DEC_BATCH
    nq = DEC_SEQ // LAT_QB
    small = [lw["qa_g"], lw["kva_g"], lw["mqn_g"], lw["mkn_g"], lw["wuq"], lw["wuk"], lw["wuv"]]
    return pl.pallas_call(
        _mla_lat_kernel,
        grid=(nb, nq),
        in_specs=[pl.BlockSpec((DEC_SEQ, Z_MLA), lambda b, j: (b, 0)),
                  pl.BlockSpec((1, 1, PAST_LEN, MLA_KVLORA), lambda b, j: (b, layer, 0, 0)),
                  pl.BlockSpec((1, 1, PAST_LEN, MLA_ROPE), lambda b, j: (b, layer, 0, 0))]
                 + [_full(t.shape) for t in tabs] + [_full(a.shape) for a in small],
        out_specs=pl.BlockSpec((LAT_QB, MIX), lambda b, j: (b * nq + j, 0)),
        out_shape=jax.ShapeDtypeStruct((nb * DEC_SEQ, MIX), F32),
        scratch_shapes=[pltpu.VMEM((LAT_KEYS, MLA_HEADS * LANES), BF16), pltpu.VMEM((LAT_KEYS, MIX), BF16)],
        compiler_params=_cparams(("arbitrary", "arbitrary")),
        name="mla_latent",
    )(zm, ckvc, krc, *tabs, *small)


def _s5_prep_kernel(bb_ref, qj_ref, m_ref):
    bb, qj = bb_ref[0], qj_ref[0]
    kt = _dot_nt_f32(bb[:, 0:LANES], qj[:, 0:LANES])
    kr = _dot_nt_f32(bb[:, LANES:2 * LANES], qj[:, LANES:2 * LANES])
    lane = lax.broadcasted_iota(jnp.int32, kt.shape, 1)
    blocks = []
    for s_pos in range(S5_CHUNK):
        sh_f = S5_GROUP * s_pos
        sh_b = S5_GROUP * (S5_CHUNK - 1 - s_pos)
        f = jnp.where(lane >= sh_f, pltpu.roll(kt, sh_f, axis=1), 0.0) if sh_f else kt
        b = jnp.where(lane < S5_ROW - sh_b, pltpu.roll(kr, S5_ROW - sh_b, axis=1), 0.0) if sh_b else kr
        blocks.append(f + b)
    m_ref[0] = jnp.concatenate(blocks, axis=0).astype(BF16)


def _s5_prep(lw):
    blk = lambda *shape: pl.BlockSpec((1,) + shape, lambda i: (i, 0, 0))
    return pl.pallas_call(
        _s5_prep_kernel,
        grid=(S5_GROUPS,),
        in_specs=[blk(S5_GROUP, S5_ROW), blk(S5_ROW, S5_ROW)],
        out_specs=blk(S5_ROW, S5_ROW),
        out_shape=jax.ShapeDtypeStruct((S5_GROUPS, S5_ROW, S5_ROW), BF16),
        compiler_params=_cparams(("arbitrary",)),
        name="s5_prep",
    )(lw["s5_bb"], lw["s5_qj"])


S5_TILE_BLOCKS = LANES // S5_GROUP


S5_INTERLEAVE = 2


def _block_transpose(xs, lane_blk):
    n = len(xs)
    d = n // 2
    while d >= 1:
        hi = (lane_blk & d) != 0
        nxt = list(xs)
        for i in range(n):
            if i & d:
                continue
            lo_arr, hi_arr = xs[i], xs[i + d]
            nxt[i] = jnp.where(hi, pltpu.roll(hi_arr, d * S5_GROUP, axis=1), lo_arr)
            nxt[i + d] = jnp.where(hi, hi_arr, pltpu.roll(lo_arr, LANES - d * S5_GROUP, axis=1))
        xs = nxt
        d //= 2
    return xs


def _s5_kernel(nb, nch, ua_ref, ub_ref, m_ref, ws_ref, wct_ref, a_ref, d_ref, h0_ref, ya_ref, yb_ref, fin_ref,
               ug_scr, yg_scr, *tmp_scr):
    rows = nb * nch
    per = S5_TILE_BLOCKS
    lane_blk = lax.broadcasted_iota(jnp.int32, (rows, LANES), 1) // S5_GROUP
    u_refs = (ua_ref, ub_ref)
    y_refs = (ya_ref, yb_ref)
    tile = lambda t: slice(t * LANES, (t + 1) * LANES)

    for half in range(S5_GROUPS // per):
        for t in range(S5_CHUNK // per):
            pieces = [u_refs[half][pl.ds(t * per + i, rows, stride=S5_CHUNK), :] for i in range(per)]
            for j, arr in enumerate(_block_transpose(pieces, lane_blk)):
                ug_scr[half * per + j, :, tile(t)] = arr

    def group_set(i, carry):
        gs = [i * S5_INTERLEAVE + n for n in range(S5_INTERLEAVE)]
        sets = [tmp_scr[6 * n:6 * n + 6] for n in range(S5_INTERLEAVE)]
        state = []
        for g, (sf_scr, sb_scr, sfs_scr, sbs_scr, _, _) in zip(gs, sets):
            s = _dot(ug_scr[g].astype(BF16), ws_ref[g])
            sf_scr[...] = s[:, 0:LANES]
            sb_scr[...] = s[:, LANES:2 * LANES]
            sfs_scr[...] = pltpu.roll(s[:, 0:LANES], S5_N, axis=1)
            sbs_scr[...] = pltpu.roll(s[:, LANES:2 * LANES], S5_N, axis=1)
            h0 = h0_ref[g]
            h_f = h0[:, 0:LANES]
            h_b = h0[:, LANES:2 * LANES]
            state.append([h_f, pltpu.roll(h_f, S5_N, axis=1), h_b, pltpu.roll(h_b, S5_N, axis=1)])
        for k in range(nch):
            rows_f = pl.ds(k, nb, stride=nch)
            rows_b = pl.ds(nch - 1 - k, nb, stride=nch)
            for n, (g, (sf_scr, sb_scr, sfs_scr, sbs_scr, hf_scr, hb_scr)) in enumerate(zip(gs, sets)):
                a = a_ref[g]
                h_f, g_f, h_b, g_b = state[n]
                hf_scr[rows_f, :] = h_f
                hb_scr[rows_b, :] = h_b
                state[n] = [a[0:1] * h_f + a[1:2] * g_f + sf_scr[rows_f, :],
                            a[0:1] * g_f + a[2:3] * h_f + sfs_scr[rows_f, :],
                            a[3:4] * h_b + a[4:5] * g_b + sb_scr[rows_b, :],
                            a[3:4] * g_b + a[5:6] * h_b + sbs_scr[rows_b, :]]
        for n, (g, (_, _, _, _, hf_scr, hb_scr)) in enumerate(zip(gs, sets)):
            fin_ref[g] = jnp.concatenate([state[n][0], state[n][2]], axis=-1)
            hp = jnp.concatenate([hf_scr[...], hb_scr[...]], axis=-1).astype(BF16)
            x = ug_scr[g]
            yg_scr[g] = _dot(x.astype(BF16), m_ref[g]) + _dot_nt(hp, wct_ref[g]) + d_ref[g] * x
        return carry

    lax.fori_loop(0, S5_GROUPS // S5_INTERLEAVE, group_set, 0)

    for half in range(S5_GROUPS // per):
        for t in range(S5_CHUNK // per):
            pieces = [yg_scr[half * per + j, :, tile(t)] for j in range(per)]
            for i, arr in enumerate(_block_transpose(pieces, lane_blk)):
                y_refs[half][pl.ds(t * per + i, rows, stride=S5_CHUNK), :] = arr


def _s5_mixer(u, m, lw, h0, nb, nb_step):
    n = u.shape[0]
    nch = n // nb // S5_CHUNK
    tok = nb_step * nch * S5_CHUNK
    rows = nb_step * nch
    g = S5_GROUPS
    res = lambda a: pl.BlockSpec(a.shape, lambda i: (0, 0, 0))
    state = pl.BlockSpec((g, nb_step, S5_ROW), lambda i: (0, i, 0))
    half = lambda j: pl.BlockSpec((tok, LANES), lambda i: (i, j))
    ya, yb, fin = pl.pallas_call(
        functools.partial(_s5_kernel, nb_step, nch),
        grid=(nb // nb_step,),
        in_specs=[half(0), half(1), res(m), res(lw["s5_ws"]), res(lw["s5_wct"]), res(lw["s5_a"]), res(lw["s5_d"]), state],
        out_specs=[half(0), half(0), state],
        out_shape=[jax.ShapeDtypeStruct((n, LANES), F32), jax.ShapeDtypeStruct((n, LANES), F32),
                   jax.ShapeDtypeStruct((g, nb, S5_ROW), F32)],
        scratch_shapes=[pltpu.VMEM((g, rows, S5_ROW), F32), pltpu.VMEM((g, rows, S5_ROW), F32)]
                       + [pltpu.VMEM((rows, LANES), F32)] * (6 * S5_INTERLEAVE),
        compiler_params=_cparams(("arbitrary",)),
        name="s5_mixer",
    )(u, u, m, lw["s5_ws"], lw["s5_wct"], lw["s5_a"], lw["s5_d"], h0)
    return ya, yb, fin


def _outproj_kernel(ona_ref, ys5a_ref, ys5b_ref, ogq_ref, omla_ref, x_ref, mod_ref, w_ref, wglu_ref, bglu_ref,
                    o_ref):
    y = jax.nn.gelu(jnp.concatenate([ys5a_ref[...], ys5b_ref[...]], axis=-1), approximate=True)
    y = y * jax.nn.sigmoid(_dot(y.astype(BF16), wglu_ref[...]) + bglu_ref[...])
    mixed = _dot(ona_ref[...].astype(BF16), w_ref[0:MIX, :])
    mixed = mixed + _dot(y.astype(BF16), w_ref[MIX:2 * MIX, :])
    mixed = mixed + _dot(ogq_ref[...].astype(BF16), w_ref[2 * MIX:3 * MIX, :])
    mixed = mixed + _dot(omla_ref[...].astype(BF16), w_ref[3 * MIX:4 * MIX, :])
    o_ref[...] = x_ref[...] + mod_ref[0, 2:3, :] * mixed


def _outproj(ona, ys5a, ys5b, ogq, omla, x, mods, lw, mod_base, rows_per_mod):
    n = x.shape[0]
    tm = 512
    row = lambda i: (mod_base + (i * tm) // rows_per_mod, 0, 0)
    part = pl.BlockSpec((tm, MIX), lambda i: (i, 0))
    half = pl.BlockSpec((tm, LANES), lambda i: (i, 0))
    return pl.pallas_call(
        _outproj_kernel,
        grid=(n // tm,),
        in_specs=[part, half, half, part, part,
                  pl.BlockSpec((tm, D_MODEL), lambda i: (i, 0)),
                  pl.BlockSpec((1, 6, D_MODEL), row),
                  _full(lw["w_out"].shape), _full(lw["w_glu"].shape), _full(lw["b_glu"].shape)],
        out_specs=pl.BlockSpec((tm, D_MODEL), lambda i: (i, 0)),
        out_shape=jax.ShapeDtypeStruct((n, D_MODEL), F32),
        compiler_params=_cparams(("arbitrary",)),
        name="outproj",
    )(ona, ys5a, ys5b, ogq, omla, x, mods, lw["w_out"], lw["w_glu"], lw["b_glu"])


FF_ROWS = 1024
FFN_VMEM_LIMIT = 58 * 1024 * 1024


def _ffn_kernel(seq, x_ref, mod_ref, g_ref, wup_ref, cw_ref, cb_ref, wd_ref, o_ref, h_scr, ug_scr, uu_scr, act_scr):
    x = x_ref[...]
    y = x * lax.rsqrt(jnp.mean(x * x, axis=-1, keepdims=True) + EPS) * g_ref[...]
    h_scr[...] = (y * (1.0 + mod_ref[0, 4:5, :]) + mod_ref[0, 3:4, :]).astype(BF16)

    pos = lax.broadcasted_iota(jnp.int32, (FF_ROWS, FF_CHUNK), 0) % seq
    first = pos == 0
    last = pos == seq - 1

    def conv(u, cw, cb):
        prev = jnp.where(first, 0.0, pltpu.roll(u, 1, axis=0))
        nxt = jnp.where(last, 0.0, pltpu.roll(u, FF_ROWS - 1, axis=0))
        return cw[0:1, :] * prev + cw[1:2, :] * u + cw[2:3, :] * nxt + cb

    def cols(j, base=0):
        return pl.ds(pl.multiple_of(base + j * FF_CHUNK, FF_CHUNK), FF_CHUNK)

    def up_proj(j, slot):
        h = h_scr[...]
        ug_scr[slot] = _dot(h, wup_ref[:, cols(j)])
        uu_scr[slot] = _dot(h, wup_ref[:, cols(j, D_FF)])

    def gate_act(j, slot):
        gate = conv(ug_scr[slot], cw_ref[:, cols(j)], cb_ref[:, cols(j)])
        up = conv(uu_scr[slot], cw_ref[:, cols(j, D_FF)], cb_ref[:, cols(j, D_FF)])
        act_scr[slot] = (gate * jax.nn.sigmoid(gate) * up).astype(BF16)

    def down_proj(j, slot):
        return _dot(act_scr[slot], wd_ref[cols(j), :])

    last_j = FF_STEPS - 1
    up_proj(0, 0)
    up_proj(1, 1)
    gate_act(0, 0)
    up_proj(2, 0)
    gate_act(1, 1)
    o_ref[...] = down_proj(0, 0)

    def body(j, carry):
        slot = j % 2
        up_proj(j + 2, slot)
        gate_act(j + 1, 1 - slot)
        o_ref[...] += down_proj(j, slot)
        return carry

    lax.fori_loop(1, last_j - 1, body, 0)
    gate_act(last_j, last_j % 2)
    o_ref[...] += down_proj(last_j - 1, (last_j - 1) % 2)
    o_ref[...] = x_ref[...] + mod_ref[0, 5:6, :] * (o_ref[...] + down_proj(last_j, last_j % 2))


def _ffn(x, mods, lw, mod_base, rows_per_mod, seq):
    n = x.shape[0]
    row = lambda i: (mod_base + (i * FF_ROWS) // rows_per_mod, 0, 0)
    resident = lambda a: pl.BlockSpec(a.shape, lambda i: (0,) * a.ndim, pipeline_mode=pl.Buffered(1))
    return pl.pallas_call(
        functools.partial(_ffn_kernel, seq),
        grid=(n // FF_ROWS,),
        in_specs=[pl.BlockSpec((FF_ROWS, D_MODEL), lambda i: (i, 0)),
                  pl.BlockSpec((1, 6, D_MODEL), row),
                  pl.BlockSpec((1, D_MODEL), lambda i: (0, 0)),
                  resident(lw["w_up"]), resident(lw["conv_w"]), resident(lw["conv_b"]), resident(lw["w_down"])],
        out_specs=pl.BlockSpec((FF_ROWS, D_MODEL), lambda i: (i, 0)),
        out_shape=jax.ShapeDtypeStruct((n, D_MODEL), F32),
        scratch_shapes=[pltpu.VMEM((FF_ROWS, D_MODEL), BF16),
                        pltpu.VMEM((2, FF_ROWS, FF_CHUNK), F32), pltpu.VMEM((2, FF_ROWS, FF_CHUNK), F32),
                        pltpu.VMEM((2, FF_ROWS, FF_CHUNK), BF16)],
        compiler_params=pltpu.CompilerParams(dimension_semantics=("arbitrary",), vmem_limit_bytes=FFN_VMEM_LIMIT),
        name="conv_ffn",
    )(x, mods, lw["norm2_g"], lw["w_up"], lw["conv_w"], lw["conv_b"], lw["w_down"])


def _rope_tables():
    pos = np.arange(DEC_SEQ)

    def ang(p, half):
        inv = ROPE_BASE ** (-np.arange(half, dtype=np.float64) / half)
        a = p.astype(np.float64)[:, None] * inv[None, :]
        return np.concatenate([a, a], axis=-1)

    def tables(dim, lanes_before, lanes_after, reps):
        a = np.concatenate([ang(pos // GRID_W, dim // 4), ang(pos % GRID_W, dim // 4)], axis=-1)
        half = dim // 4
        first = (np.arange(dim) % (2 * half)) < half
        cos = np.cos(a)
        sin_a = np.where(first[None, :], -np.sin(a), 0.0)
        sin_b = np.where(first[None, :], 0.0, np.sin(a))

        def place(t, fill):
            t = np.concatenate([np.full((DEC_SEQ, lanes_before), fill), t,
                                np.full((DEC_SEQ, lanes_after), fill)], axis=-1)
            return jnp.asarray(np.tile(t, (1, reps)), F32)

        return place(cos, 1.0), place(sin_a, 0.0), place(sin_b, 0.0)

    gq = tables(DH, 0, 0, LANES // DH)
    mla = tables(MLA_ROPE, MLA_NOPE, LANES - MLA_QK, 1)
    return gq, mla


def _na_bias_tables(rpb):
    qc = np.arange(GRID_W)[:, None]
    kc = np.arange(GRID_W)[None, :]
    col_start = np.clip(qc - NA_WIN_C // 2, 0, GRID_W - NA_WIN_C)
    valid = (kc >= col_start) & (kc < col_start + NA_WIN_C)
    d_c = np.clip(kc - qc, 1 - NA_WIN_C, NA_WIN_C - 1) + NA_WIN_C - 1
    nrel = 2 * NA_WIN_C - 1
    onehot = jnp.asarray((d_c.reshape(-1)[None, :] == np.arange(nrel)[:, None]).astype(np.float32))
    t = jnp.dot(rpb.astype(F32).reshape(-1, nrel), onehot, precision=lax.Precision.HIGHEST)
    t = t.reshape(NA_HEADS, 2 * NA_WIN_R - 1, GRID_W, GRID_W)
    t = jnp.where(jnp.asarray(valid)[None, None], t, NEG)
    t = t.transpose(1, 0, 2, 3).reshape(2 * NA_WIN_R - 1, NA_HEADS * GRID_W, GRID_W)
    return jnp.concatenate([t[:-1], t[1:]], axis=-1)


def _s5_matrices(lam_re, lam_im, log_dt, b_re, b_im, c_re, c_im):
    c = S5_CHUNK
    cat = lambda parts: jnp.concatenate(parts, axis=-1)
    pw, bre, bim, cre, cim = [], [], [], [], []
    for d in range(2):
        dt = jnp.exp(log_dt[d].astype(F32))[:, None]
        lr, li = lam_re[d].astype(F32), lam_im[d].astype(F32)
        mag = jnp.exp(lr * dt)
        a_re, a_im = mag * jnp.cos(li * dt), mag * jnp.sin(li * dt)
        den = lr * lr + li * li
        f_re = ((a_re - 1.0) * lr + a_im * li) / den
        f_im = (a_im * lr - (a_re - 1.0) * li) / den
        br, bi = b_re[d].astype(F32), b_im[d].astype(F32)
        bre.append((f_re[..., None] * br - f_im[..., None] * bi).transpose(0, 2, 1))
        bim.append((f_re[..., None] * bi + f_im[..., None] * br).transpose(0, 2, 1))
        cre.append(c_re[d].astype(F32))
        cim.append(c_im[d].astype(F32))
        p_re, p_im = [jnp.ones_like(a_re)], [jnp.zeros_like(a_re)]
        for _ in range(c):
            p_re, p_im = (p_re + [p_re[-1] * a_re - p_im[-1] * a_im], p_im + [p_re[-1] * a_im + p_im[-1] * a_re])
        p_re, p_im = jnp.stack(p_re, axis=1), jnp.stack(p_im, axis=1)
        pw.append((p_re, p_im))
    up, dn = np.arange(c), np.arange(c)[::-1]
    (pfr, pfi), (pbr, pbi) = pw
    outer = lambda x, y: (x[:, :, None, :] * y[:, None, :, :])
    e_fr, e_fi, e_br, e_bi = pfr[:, dn], pfi[:, dn], pbr[:, up], pbi[:, up]
    bb = cat([bre[0], bim[0], bre[1], bim[1]])
    ws = (outer(cat([e_fr, e_fr, e_br, e_br]), bb)
          + outer(cat([-e_fi, e_fi, -e_bi, e_bi]), cat([bim[0], bre[0], bim[1], bre[1]])))
    e_fr, e_fi, e_br, e_bi = pfr[:, up + 1], pfi[:, up + 1], pbr[:, dn + 1], pbi[:, dn + 1]
    c_a, c_b = cat([cre[0], cre[0], cre[1], cre[1]]), cat([cim[0], cim[0], cim[1], cim[1]])
    wct = outer(cat([e_fr, -e_fi, e_br, -e_bi]), c_a) + outer(cat([-e_fi, -e_fr, -e_bi, -e_br]), c_b)
    e_fr, e_fi, e_br, e_bi = pfr[:, up], pfi[:, up], pbr[:, dn], pbi[:, dn]
    qj = outer(cat([e_fr, -e_fi, e_br, -e_bi]), c_a) + outer(cat([-e_fi, -e_fr, -e_bi, -e_br]), c_b)
    rows = lambda w: w.reshape(S5_GROUPS, S5_ROW, 4 * S5_N).astype(BF16)
    av = []
    for p_re, p_im in ((pfr[:, c], pfi[:, c]), (pbr[:, c], pbi[:, c])):
        av += [cat([p_re, p_re]), cat([-p_im, p_im]), cat([p_im, -p_im])]
    return rows(ws), rows(wct), bb, qj.reshape(S5_GROUPS, S5_ROW, 4 * S5_N), jnp.stack(av, axis=1)


def _layer_weights(l, p):
    w = p["w_in"][l]
    q0 = Z_NA + Z_S5
    gq = lambda h: w[:, q0 + h * DH:q0 + (h + 1) * DH]
    m0 = q0 + 2 * MIX
    w_in = jnp.concatenate(
        [w[:, 0:q0], gq(0), gq(2), gq(1), gq(3), w[:, q0 + MIX:m0],
         w[:, m0 + MLA_QLORA:m0 + MLA_QLORA + MLA_KVLORA], w[:, m0:m0 + MLA_QLORA],
         w[:, m0 + MLA_QLORA + MLA_KVLORA:], jnp.zeros((D_MODEL, Z_ALL - w.shape[1]), w.dtype)],
        axis=1).astype(BF16)
    wo = p["w_out"][l]
    og = lambda h: wo[2 * MIX + h * DH:2 * MIX + (h + 1) * DH]
    w_out = jnp.concatenate([wo[0:2 * MIX], og(0), og(2), og(1), og(3), wo[3 * MIX:]], axis=0).astype(BF16)
    pad_head = lambda g: jnp.tile(jnp.pad(g, (0, LANES - MLA_QK)), MLA_HEADS)[None, :]
    wuq = jnp.pad(p["mla_w_uq"][l].reshape(MLA_QLORA, MLA_HEADS, MLA_QK),
                  ((0, 2 * LANES - MLA_QLORA), (0, 0), (0, LANES - MLA_QK))).reshape(2 * LANES, MLA_HEADS * LANES)
    wukv = p["mla_w_ukv"][l].reshape(MLA_KVLORA, MLA_HEADS, MLA_NOPE + DH)
    wuk = jnp.pad(wukv[:, :, :MLA_NOPE], ((0, 0), (0, 0), (0, LANES - MLA_NOPE))).reshape(MLA_KVLORA, MLA_HEADS * LANES)
    wuv = wukv[:, :, MLA_NOPE:].reshape(MLA_KVLORA, MIX)
    s5_ws, s5_wct, s5_bb, s5_qj, s5_a = _s5_matrices(p["s5_lam_re"][l], p["s5_lam_im"][l], p["s5_log_dt"][l],
                                                     p["s5_b_re"][l], p["s5_b_im"][l], p["s5_c_re"][l], p["s5_c_im"][l])
    return dict(
        norm1_g=p["norm1_g"][l][None, :], norm2_g=p["norm2_g"][l][None, :],
        w_in=w_in, w_out=w_out,
        naq_g=jnp.tile(p["na_qn"][l], NA_HEADS)[None, :], nak_g=jnp.tile(p["na_kn"][l], NA_HEADS)[None, :],
        gqq_g=jnp.tile(p["gq_qn"][l], GQ_HEADS)[None, :], gqk_g=jnp.tile(p["gq_kn"][l], GQ_KV)[None, :],
        qa_g=jnp.pad(p["mla_qa_g"][l], (0, 2 * LANES - MLA_QLORA))[None, :], kva_g=p["mla_kva_g"][l][None, :],
        mqn_g=pad_head(p["mla_qn"][l]), mkn_g=pad_head(p["mla_kn"][l]),
        wuq=wuq.astype(BF16), wuk=wuk.astype(BF16), wuv=wuv.astype(BF16),
        na_bias=_na_bias_tables(p["na_rpb"][l]),
        s5_ws=s5_ws, s5_wct=s5_wct, s5_bb=s5_bb, s5_qj=s5_qj, s5_a=s5_a,
        s5_d=jnp.tile(p["s5_d"][l].reshape(S5_GROUPS, 1, S5_GROUP), (1, 1, S5_CHUNK)),
        w_glu=p["s5_w_glu"][l].astype(BF16), b_glu=p["s5_b_glu"][l][None, :],
        w_up=p["ffn_w_up"][l].astype(BF16), conv_w=p["ffn_conv_w"][l], conv_b=p["ffn_conv_b"][l][None, :],
        w_down=p["ffn_w_down"][l].astype(BF16),
    )


def kernel(x_prompt, x_sample, cache_na_k, cache_na_v, state_s5, cache_gqa_k, cache_gqa_v, cache_mla_ckv,
           cache_mla_krope, c, c_ctx, norm1_g, norm2_g, ada_w, ada_b, w_in, na_qn, na_kn, na_rpb, s5_lam_re,
           s5_lam_im, s5_log_dt, s5_b_re, s5_b_im, s5_c_re, s5_c_im, s5_d, s5_w_glu, s5_b_glu, gq_qn, gq_kn,
           mla_qa_g, mla_kva_g, mla_w_uq, mla_w_ukv, mla_qn, mla_kn, w_out, ffn_w_up, ffn_conv_w, ffn_conv_b,
           ffn_w_down):
    p = dict(norm1_g=norm1_g, norm2_g=norm2_g, w_in=w_in, na_qn=na_qn, na_kn=na_kn, na_rpb=na_rpb,
             s5_lam_re=s5_lam_re, s5_lam_im=s5_lam_im, s5_log_dt=s5_log_dt, s5_b_re=s5_b_re, s5_b_im=s5_b_im,
             s5_c_re=s5_c_re, s5_c_im=s5_c_im, s5_d=s5_d, s5_w_glu=s5_w_glu, s5_b_glu=s5_b_glu,
             gq_qn=gq_qn, gq_kn=gq_kn, mla_qa_g=mla_qa_g, mla_kva_g=mla_kva_g, mla_w_uq=mla_w_uq,
             mla_w_ukv=mla_w_ukv, mla_qn=mla_qn, mla_kn=mla_kn, w_out=w_out, ffn_w_up=ffn_w_up,
             ffn_conv_w=ffn_conv_w, ffn_conv_b=ffn_conv_b, ffn_w_down=ffn_w_down)
    nb, nd = BATCH, DEC_BATCH
    cvec = jnp.concatenate([c_ctx[None, :], c, jnp.zeros((8 - 1 - nd, D_MODEL), F32)], axis=0)
    mods_all = _ada_mods(cvec, ada_w, ada_b).reshape(DEPTH, 8, 6, D_MODEL)
    gq_tabs, mla_tabs = _rope_tables()

    xc = x_prompt.reshape(nb * SEQ, D_MODEL)
    xl = x_sample.reshape(nd * DEC_SEQ, D_MODEL)
    zero_state = jnp.zeros((S5_GROUPS, nb, 4 * S5_N), F32)
    caches = [jnp.zeros((nb, DEPTH) + shp, F32) for shp in CACHE_SHAPES]
    s5_states = []
    for l in range(DEPTH):
        lw = _layer_weights(l, p)
        mods = mods_all[l]
        s5_m = _s5_prep(lw)

        zna, zs5, zgq, zm = _inproj(xc, mods, lw["norm1_g"], lw["w_in"], 0, nb * SEQ)
        (ona, ogq, omla), caches = _ctx_attn(zna, zgq, zm, lw, l, caches)
        ys5a, ys5b, fin = _s5_mixer(zs5, s5_m, lw, zero_state, nb, S5_CTX_BLOCK)
        xc = _outproj(ona, ys5a, ys5b, ogq, omla, xc, mods, lw, 0, nb * SEQ)
        xc = _ffn(xc, mods, lw, 0, nb * SEQ, SEQ)
        s5_states.append(fin.reshape(S5_GROUPS, nb, 2, 2, S5_N).transpose(1, 2, 3, 0, 4))

        zna, zs5, zgq, zm = _inproj(xl, mods, lw["norm1_g"], lw["w_in"], 1, DEC_SEQ)
        ona = _na_lat(zna, cache_na_k, cache_na_v, l, lw["na_bias"], lw)
        ogq = _gq_lat(zgq, cache_gqa_k, cache_gqa_v, l, gq_tabs, lw)
        omla = _mla_lat(zm, cache_mla_ckv, cache_mla_krope, l, mla_tabs, lw)
        h0 = state_s5[:, l].astype(F32).transpose(3, 0, 1, 2, 4).reshape(S5_GROUPS, nd, 4 * S5_N)
        ys5a, ys5b, _ = _s5_mixer(zs5, s5_m, lw, h0, nd, nd)
        xl = _outproj(ona, ys5a, ys5b, ogq, omla, xl, mods, lw, 1, DEC_SEQ)
        xl = _ffn(xl, mods, lw, 1, DEC_SEQ, DEC_SEQ)

    return (xc.reshape(nb, SEQ, D_MODEL), xl.reshape(nd, DEC_SEQ, D_MODEL), caches[0], caches[1],
            jnp.stack(s5_states, axis=1), caches[2], caches[3], caches[4], caches[5])
```

```python
import functools

import numpy as np
import jax
import jax.numpy as jnp
from jax import lax
from jax.experimental import pallas as pl
from jax.experimental.pallas import tpu as pltpu

F32 = jnp.float32
BF16 = jnp.bfloat16

D_MODEL = 1024
BATCH = 32
SEQ = 256
DEPTH = 2
DEC_BATCH = 4
DEC_SEQ = 1024
PAST_LEN = 512
GRID_W = 64
GRID_ROWS = DEC_SEQ // GRID_W
MIX = D_MODEL // 4
DH = 64
NA_HEADS = MIX // DH
NA_WIN_R = 8
NA_WIN_C = 16
S5_GROUP = 16
S5_GROUPS = MIX // S5_GROUP
S5_N = 64
GQ_HEADS = MIX // DH
GQ_KV = GQ_HEADS // 2
MLA_HEADS = MIX // DH
MLA_NOPE = 64
MLA_ROPE = 32
MLA_QK = MLA_NOPE + MLA_ROPE
MLA_QLORA = (3 * D_MODEL) // 16
MLA_KVLORA = D_MODEL // 8
D_FF = 128 * ((8 * D_MODEL // 3 + 127) // 128)
ROPE_BASE = 10000.0
EPS = 1e-6
NEG = -1e30

LANES = 128
MXU_DIM = 256
VMEM_LIMIT = 48 * 1024 * 1024
MXU_DEN_MIN_KEYS = 1024

Z_NA = 3 * MIX
Z_S5 = MIX
Z_GQ = 2 * MIX
Z_MLA = 3 * LANES
Z_ALL = Z_NA + Z_S5 + Z_GQ + Z_MLA

S5_CHUNK = 16
S5_ROW = S5_CHUNK * S5_GROUP
S5_CTX_BLOCK = 16

FF_CHUNK = 256
FF_STEPS = D_FF // FF_CHUNK


def _cparams(sem):
    return pltpu.CompilerParams(dimension_semantics=sem, vmem_limit_bytes=VMEM_LIMIT)


def _dot(a, b):
    return jnp.dot(a, b, preferred_element_type=F32)


def _dot_nt(a, b):
    return lax.dot_general(a, b, (((1,), (1,)), ((), ())), preferred_element_type=F32)


def _dot_nt_f32(a, b):
    def split(x):
        hi = x.astype(BF16)
        r1 = x - hi.astype(F32)
        mid = r1.astype(BF16)
        return hi, mid, (r1 - mid.astype(F32)).astype(BF16)

    pa, pb = split(a), split(b)
    acc = None
    for i in range(3):
        for j in range(3 - i):
            term = _dot_nt(pa[i], pb[j])
            acc = term if acc is None else acc + term
    return acc


def _rms_rows(x, gain, denom=None, valid=None):
    xx = x * x
    if valid is not None and valid != x.shape[-1]:
        lane = lax.broadcasted_iota(jnp.int32, x.shape, 1)
        xx = jnp.where(lane < valid, xx, 0.0)
    denom = denom or (valid or x.shape[-1])
    ss = jnp.sum(xx, axis=-1, keepdims=True)
    return x * lax.rsqrt(ss / denom + EPS) * gain


def _seg_rms(x, seg, denom, gain):
    rows, width = x.shape
    if seg % LANES == 0:
        parts = []
        for s in range(width // seg):
            xs = x[:, s * seg:(s + 1) * seg]
            ss = jnp.sum(xs * xs, axis=-1, keepdims=True)
            parts.append(xs * lax.rsqrt(ss / denom + EPS))
        return jnp.concatenate(parts, axis=-1) * gain
    same_seg = (lax.broadcasted_iota(jnp.int32, (width, width), 0) // seg
                == lax.broadcasted_iota(jnp.int32, (width, width), 1) // seg)
    ones = jnp.where(same_seg, 1.0, 0.0).astype(BF16)
    xx = x * x
    hi = xx.astype(BF16)
    lo = (xx - hi.astype(F32)).astype(BF16)
    ss = _dot(hi, ones) + _dot(lo, ones)
    return x * lax.rsqrt(ss / denom + EPS) * gain


def _rope(x, cos, sin_a, sin_b, half):
    tiles = []
    for t in range(x.shape[-1] // LANES):
        xt = x[:, t * LANES:(t + 1) * LANES]
        up = pltpu.roll(xt, LANES - half, axis=1)
        dn = pltpu.roll(xt, half, axis=1)
        tiles.append(xt * cos + up * sin_a + dn * sin_b)
    return tiles[0] if len(tiles) == 1 else jnp.concatenate(tiles, axis=-1)


def _packed_attn(q, parts, nseg, scale, biases=None):
    tq, width = q.shape
    seg_id = lax.broadcasted_iota(jnp.int32, q.shape, 1) // DH
    lhs = jnp.concatenate([jnp.where(seg_id == h, q, 0.0) for h in range(nseg)], axis=0).astype(BF16)
    scores = []
    for i, (kb, _) in enumerate(parts):
        s = _dot_nt(lhs, kb) * scale
        if biases is not None and biases[i] is not None:
            s = s + biases[i]
        scores.append(s)
    m = scores[0].max(axis=-1, keepdims=True)
    for s in scores[1:]:
        m = jnp.maximum(m, s.max(axis=-1, keepdims=True))
    mxu_den = sum(kb.shape[0] for kb, _ in parts) >= MXU_DEN_MIN_KEYS
    den = None
    pv = None
    for s, (_, vb) in zip(scores, parts):
        p = jnp.exp(s - m)
        pb = p.astype(BF16)
        ps = _dot(pb, jnp.ones(vb.shape, BF16)) if mxu_den else p.sum(axis=-1, keepdims=True)
        den = ps if den is None else den + ps
        c = _dot(pb, vb)
        pv = c if pv is None else pv + c
    pv = pv / den
    out = jnp.zeros((tq, width), F32)
    for h in range(nseg):
        out = out + jnp.where(seg_id == h, pv[h * tq:(h + 1) * tq], 0.0)
    return out


def _mla_attn(q, parts, scale):
    tq = q.shape[0]
    vseg = None
    out = jnp.zeros((tq, MIX), F32)
    for h in range(MLA_HEADS):
        qh = q[:, h * LANES:(h + 1) * LANES].astype(BF16)
        scores = [_dot_nt(qh, kb[:, h * LANES:(h + 1) * LANES]) * scale for kb, _ in parts]
        m = scores[0].max(axis=-1, keepdims=True)
        for s in scores[1:]:
            m = jnp.maximum(m, s.max(axis=-1, keepdims=True))
        den = None
        pv = None
        for s, (_, vb) in zip(scores, parts):
            p = jnp.exp(s - m)
            ps = p.sum(axis=-1, keepdims=True)
            den = ps if den is None else den + ps
            vseg = lax.broadcasted_iota(jnp.int32, vb.shape, 1) // DH
            c = _dot(p.astype(BF16), jnp.where(vseg == h, vb, jnp.zeros_like(vb)))
            pv = c if pv is None else pv + c
        out = out + pv / den
    return out


def _mla_qkv(zm, qa_g, kva_g, mqn_g, mkn_g, wuq, wuk, wuv):
    ckv = _rms_rows(zm[:, 0:LANES], kva_g)
    col3 = zm[:, 2 * LANES:3 * LANES]
    lane = lax.broadcasted_iota(jnp.int32, col3.shape, 1)
    kr_placed = jnp.where((lane >= MLA_NOPE) & (lane < MLA_QK), col3, 0.0)
    cq = _rms_rows(zm[:, LANES:3 * LANES], qa_g, valid=MLA_QLORA)
    q = _seg_rms(_dot(cq.astype(BF16), wuq), LANES, MLA_QK, mqn_g)
    k, v = _mla_kv(ckv, kr_placed, mkn_g, wuk, wuv)
    return q, k, v, ckv, col3


def _mla_kv(ckv, kr_placed, mkn_g, wuk, wuv):
    cb = ckv.astype(BF16)
    kf = _dot(cb, wuk) + jnp.concatenate([kr_placed] * MLA_HEADS, axis=-1)
    return _seg_rms(kf, LANES, MLA_QK, mkn_g), _dot(cb, wuv)


def _ada_kernel(c_ref, w_ref, b_ref, o_ref):
    cv = c_ref[...]
    s = cv * jax.nn.sigmoid(cv)
    o_ref[0] = _dot(s.astype(BF16), w_ref[0].astype(BF16)) + b_ref[0]


def _ada_mods(cvec, ada_w, ada_b):
    tn = 1536
    n = ada_w.shape[-1]
    return pl.pallas_call(
        _ada_kernel,
        grid=(DEPTH, n // tn),
        in_specs=[pl.BlockSpec((8, D_MODEL), lambda l, j: (0, 0)),
                  pl.BlockSpec((1, D_MODEL, tn), lambda l, j: (l, 0, j)),
                  pl.BlockSpec((1, 1, tn), lambda l, j: (l, 0, j))],
        out_specs=pl.BlockSpec((1, 8, tn), lambda l, j: (l, 0, j)),
        out_shape=jax.ShapeDtypeStruct((DEPTH, 8, n), F32),
        compiler_params=_cparams(("arbitrary", "arbitrary")),
        name="ada_mods",
    )(cvec, ada_w, ada_b.reshape(DEPTH, 1, n))


def _inproj_kernel(x_ref, mod_ref, g_ref, w_ref, ona_ref, os5_ref, ogq_ref, omla_ref):
    x = x_ref[...]
    y = x * lax.rsqrt(jnp.mean(x * x, axis=-1, keepdims=True) + EPS) * g_ref[...]
    h = y * (1.0 + mod_ref[0, 1:2, :]) + mod_ref[0, 0:1, :]
    z = _dot(h.astype(BF16), w_ref[...])
    ona_ref[...] = z[:, 0:Z_NA]
    os5_ref[...] = z[:, Z_NA:Z_NA + Z_S5]
    ogq_ref[...] = z[:, Z_NA + Z_S5:Z_NA + Z_S5 + Z_GQ]
    omla_ref[...] = z[:, Z_NA + Z_S5 + Z_GQ:Z_ALL]


def _inproj(x, mods, g, w, mod_base, rows_per_mod):
    n = x.shape[0]
    tm = 512
    row = lambda i: (mod_base + (i * tm) // rows_per_mod, 0, 0)
    widths = (Z_NA, Z_S5, Z_GQ, Z_MLA)
    return pl.pallas_call(
        _inproj_kernel,
        grid=(n // tm,),
        in_specs=[pl.BlockSpec((tm, D_MODEL), lambda i: (i, 0)),
                  pl.BlockSpec((1, 6, D_MODEL), row),
                  pl.BlockSpec((1, D_MODEL), lambda i: (0, 0)),
                  pl.BlockSpec((D_MODEL, Z_ALL), lambda i: (0, 0))],
        out_specs=[pl.BlockSpec((tm, wd), lambda i: (i, 0)) for wd in widths],
        out_shape=[jax.ShapeDtypeStruct((n, wd), F32) for wd in widths],
        compiler_params=_cparams(("arbitrary",)),
        name="inproj",
    )(x, mods, g, w)


def _store_heads(ref, x, heads):
    for h in range(heads):
        ref[0, 0, h] = x[:, h * DH:(h + 1) * DH]


def _ctx_attn_kernel(zna_ref, zgq_ref, zm_ref, naq_g, nak_g, gqq_g, gqk_g, qa_g, kva_g, mqn_g, mkn_g,
                     wuq_ref, wuk_ref, wuv_ref, *refs):
    ona_ref, ogq_ref, omla_ref, nk_ref, nv_ref, gk_ref, gv_ref, ckv_ref, kr_ref = refs[-9:]
    zna = zna_ref[...]
    q = _seg_rms(zna[:, 0:MIX], DH, DH, naq_g[...])
    k = _seg_rms(zna[:, MIX:2 * MIX], DH, DH, nak_g[...])
    v = zna[:, 2 * MIX:3 * MIX]
    _store_heads(nk_ref, k, NA_HEADS)
    _store_heads(nv_ref, v, NA_HEADS)
    ona_ref[...] = _packed_attn(q, [(k.astype(BF16), v.astype(BF16))], NA_HEADS, DH ** -0.5)

    zgq = zgq_ref[...]
    gq = _seg_rms(zgq[:, 0:MIX], DH, DH, gqq_g[...])
    gk = _seg_rms(zgq[:, MIX:MIX + LANES], DH, DH, gqk_g[...])
    gv = zgq[:, MIX + LANES:2 * MIX]
    _store_heads(gk_ref, gk, GQ_KV)
    _store_heads(gv_ref, gv, GQ_KV)
    kv = [(gk.astype(BF16), gv.astype(BF16))]
    ogq_ref[...] = jnp.concatenate(
        [_packed_attn(gq[:, r * LANES:(r + 1) * LANES], kv, GQ_KV, DH ** -0.5) for r in range(2)], axis=-1)

    mq, mk, mv, ckv, col3 = _mla_qkv(zm_ref[...], qa_g[...], kva_g[...], mqn_g[...], mkn_g[...],
                                     wuq_ref[...], wuk_ref[...], wuv_ref[...])
    ckv_ref[0, 0] = ckv
    kr_ref[0, 0] = pltpu.roll(col3, LANES - MLA_NOPE, axis=1)[:, 0:MLA_ROPE]
    omla_ref[...] = _mla_attn(mq, [(mk.astype(BF16), mv.astype(BF16))], MLA_QK ** -0.5)


def _full(shape):
    nd = len(shape)
    return pl.BlockSpec(shape, lambda *a: (0,) * nd)


CACHE_SHAPES = ((NA_HEADS, SEQ, DH), (NA_HEADS, SEQ, DH), (GQ_KV, SEQ, DH), (GQ_KV, SEQ, DH),
                (SEQ, MLA_KVLORA), (SEQ, MLA_ROPE))


def _ctx_attn(zna, zgq, zm, lw, layer, caches):
    n = zna.shape[0]
    t = SEQ
    rows = lambda wd: pl.BlockSpec((t, wd), lambda b: (b, 0))
    cache_spec = lambda shp: pl.BlockSpec((1, 1) + shp, lambda b: (b, layer) + (0,) * len(shp))
    gains = [lw["naq_g"], lw["nak_g"], lw["gqq_g"], lw["gqk_g"], lw["qa_g"], lw["kva_g"], lw["mqn_g"], lw["mkn_g"]]
    weights = [lw["wuq"], lw["wuk"], lw["wuv"]]
    n_in = 3 + len(gains) + len(weights)
    outs = pl.pallas_call(
        _ctx_attn_kernel,
        grid=(n // t,),
        in_specs=[rows(Z_NA), rows(Z_GQ), rows(Z_MLA)] + [_full(a.shape) for a in gains + weights]
                 + [pl.BlockSpec(memory_space=pl.ANY)] * len(caches),
        out_specs=[rows(MIX)] * 3 + [cache_spec(shp) for shp in CACHE_SHAPES],
        out_shape=[jax.ShapeDtypeStruct((n, MIX), F32)] * 3
                  + [jax.ShapeDtypeStruct((n // t, DEPTH) + shp, F32) for shp in CACHE_SHAPES],
        input_output_aliases={n_in + i: 3 + i for i in range(len(caches))},
        compiler_params=_cparams(("arbitrary",)),
        name="ctx_attn",
    )(zna, zgq, zm, *gains, *weights, *caches)
    return outs[:3], list(outs[3:])


def _na_window_start(r):
    return jnp.clip(r - NA_WIN_R // 2, 0, GRID_ROWS - NA_WIN_R)


def _pack_heads(ref):
    return jnp.concatenate([ref[0, 0, h] for h in range(ref.shape[2])], axis=-1)


def _na_lat_kernel(zna_ref, kc_ref, vc_ref, bias_ref, naq_g, nak_g, o_ref, k_scr, v_scr, kc_scr, vc_scr):
    r = pl.program_id(1)

    @pl.when(r == 0)
    def _():
        k_scr[...] = _seg_rms(zna_ref[:, MIX:2 * MIX], DH, DH, nak_g[...]).astype(BF16)
        v_scr[...] = zna_ref[:, 2 * MIX:3 * MIX].astype(BF16)
        kc_scr[...] = _pack_heads(kc_ref).astype(BF16)
        vc_scr[...] = _pack_heads(vc_ref).astype(BF16)

    q = _seg_rms(zna_ref[pl.ds(pl.multiple_of(r * GRID_W, GRID_W), GRID_W), 0:MIX], DH, DH, naq_g[...])
    start = pl.multiple_of(_na_window_start(r) * GRID_W, GRID_W)
    span = NA_WIN_R * GRID_W
    parts = [(k_scr[pl.ds(start, span), :], v_scr[pl.ds(start, span), :]),
             (kc_scr[...], vc_scr[...])]
    d0 = _na_window_start(r) - r + NA_WIN_R - 1
    bias = jnp.concatenate([bias_ref[d0 + 2 * i] for i in range(NA_WIN_R // 2)], axis=-1)
    o_ref[...] = _packed_attn(q, parts, NA_HEADS, DH ** -0.5, biases=[bias, None])


def _na_lat(zna, kc, vc, layer, bias, lw):
    nb = DEC_BATCH
    cache = pl.BlockSpec((1, 1, NA_HEADS, PAST_LEN, DH), lambda b, r: (b, layer, 0, 0, 0))
    return pl.pallas_call(
        _na_lat_kernel,
        grid=(nb, GRID_ROWS),
        in_specs=[pl.BlockSpec((DEC_SEQ, Z_NA), lambda b, r: (b, 0)),
                  cache, cache,
                  _full(bias.shape),
                  _full(lw["naq_g"].shape), _full(lw["nak_g"].shape)],
        out_specs=pl.BlockSpec((GRID_W, MIX), lambda b, r: (b * GRID_ROWS + r, 0)),
        out_shape=jax.ShapeDtypeStruct((nb * DEC_SEQ, MIX), F32),
        scratch_shapes=[pltpu.VMEM((DEC_SEQ, MIX), BF16), pltpu.VMEM((DEC_SEQ, MIX), BF16),
                        pltpu.VMEM((PAST_LEN, MIX), BF16), pltpu.VMEM((PAST_LEN, MIX), BF16)],
        compiler_params=_cparams(("arbitrary", "arbitrary")),
        name="na_latent",
    )(zna, kc, vc, bias, lw["naq_g"], lw["nak_g"])


LAT_QB = 256
LAT_KEYS = DEC_SEQ + PAST_LEN


def _gq_lat_kernel(zgq_ref, kc_ref, vc_ref, cos_ref, sa_ref, sb_ref, gqq_g, gqk_g, o_ref, k_scr, v_scr):
    j = pl.program_id(1)

    @pl.when(j == 0)
    def _():
        k = _seg_rms(zgq_ref[:, MIX:MIX + LANES], DH, DH, gqk_g[...])
        k = _rope(k, cos_ref[...], sa_ref[...], sb_ref[...], DH // 4)
        k_scr[0:DEC_SEQ, :] = k.astype(BF16)
        k_scr[DEC_SEQ:LAT_KEYS, :] = _pack_heads(kc_ref).astype(BF16)
        v_scr[0:DEC_SEQ, :] = zgq_ref[:, MIX + LANES:2 * MIX].astype(BF16)
        v_scr[DEC_SEQ:LAT_KEYS, :] = _pack_heads(vc_ref).astype(BF16)

    rows = pl.ds(pl.multiple_of(j * LAT_QB, LAT_QB), LAT_QB)
    q = _seg_rms(zgq_ref[rows, 0:MIX], DH, DH, gqq_g[...])
    q = _rope(q, cos_ref[rows, :], sa_ref[rows, :], sb_ref[rows, :], DH // 4)
    kv = [(k_scr[...], v_scr[...])]
    o_ref[...] = jnp.concatenate(
        [_packed_attn(q[:, r * LANES:(r + 1) * LANES], kv, GQ_KV, DH ** -0.5) for r in range(2)], axis=-1)


def _gq_lat(zgq, kc, vc, layer, tabs, lw):
    nb = DEC_BATCH
    nq = DEC_SEQ // LAT_QB
    cache = pl.BlockSpec((1, 1, GQ_KV, PAST_LEN, DH), lambda b, j: (b, layer, 0, 0, 0))
    return pl.pallas_call(
        _gq_lat_kernel,
        grid=(nb, nq),
        in_specs=[pl.BlockSpec((DEC_SEQ, Z_GQ), lambda b, j: (b, 0)), cache, cache]
                 + [_full(t.shape) for t in tabs] + [_full(lw["gqq_g"].shape), _full(lw["gqk_g"].shape)],
        out_specs=pl.BlockSpec((LAT_QB, MIX), lambda b, j: (b * nq + j, 0)),
        out_shape=jax.ShapeDtypeStruct((nb * DEC_SEQ, MIX), F32),
        scratch_shapes=[pltpu.VMEM((LAT_KEYS, LANES), BF16), pltpu.VMEM((LAT_KEYS, LANES), BF16)],
        compiler_params=_cparams(("arbitrary", "arbitrary")),
        name="gq_latent",
    )(zgq, kc, vc, *tabs, lw["gqq_g"], lw["gqk_g"])


def _mla_lat_kernel(zm_ref, ckvc_ref, krc_ref, cos_ref, sa_ref, sb_ref, qa_g, kva_g, mqn_g, mkn_g,
                    wuq_ref, wuk_ref, wuv_ref, o_ref, k_scr, v_scr):
    j = pl.program_id(1)
    half = MLA_ROPE // 4

    @pl.when(j == 0)
    def _():
        zm = zm_ref[...]
        ckv = _rms_rows(zm[:, 0:LANES], kva_g[...])
        col3 = zm[:, 2 * LANES:3 * LANES]
        lane = lax.broadcasted_iota(jnp.int32, col3.shape, 1)
        kr_placed = jnp.where((lane >= MLA_NOPE) & (lane < MLA_QK), col3, 0.0)
        k, v = _mla_kv(ckv, kr_placed, mkn_g[...], wuk_ref[...], wuv_ref[...])
        k = _rope(k, cos_ref[...], sa_ref[...], sb_ref[...], half)
        k_scr[0:DEC_SEQ, :] = k.astype(BF16)
        v_scr[0:DEC_SEQ, :] = v.astype(BF16)
        krc = jnp.concatenate([jnp.zeros((PAST_LEN, MLA_NOPE), F32), krc_ref[0, 0],
                               jnp.zeros((PAST_LEN, LANES - MLA_QK), F32)], axis=-1)
        kc, vc = _mla_kv(ckvc_ref[0, 0], krc, mkn_g[...], wuk_ref[...], wuv_ref[...])
        k_scr[DEC_SEQ:LAT_KEYS, :] = kc.astype(BF16)
        v_scr[DEC_SEQ:LAT_KEYS, :] = vc.astype(BF16)

    rows = pl.ds(pl.multiple_of(j * LAT_QB, LAT_QB), LAT_QB)
    cq = _rms_rows(zm_ref[rows, LANES:3 * LANES], qa_g[...], valid=MLA_QLORA)
    q = _seg_rms(_dot(cq.astype(BF16), wuq_ref[...]), LANES, MLA_QK, mqn_g[...])
    q = _rope(q, cos_ref[rows, :], sa_ref[rows, :], sb_ref[rows, :], half)
    o_ref[...] = _mla_attn(q, [(k_scr[...], v_scr[...])], MLA_QK ** -0.5)


def _mla_lat(zm, ckvc, krc, layer, tabs, lw):
    nb = DEC_BATCH
    nq = DEC_SEQ // LAT_QB
    small = [lw["qa_g"], lw["kva_g"], lw["mqn_g"], lw["mkn_g"], lw["wuq"], lw["wuk"], lw["wuv"]]
    return pl.pallas_call(
        _mla_lat_kernel,
        grid=(nb, nq),
        in_specs=[pl.BlockSpec((DEC_SEQ, Z_MLA), lambda b, j: (b, 0)),
                  pl.BlockSpec((1, 1, PAST_LEN, MLA_KVLORA), lambda b, j: (b, layer, 0, 0)),
                  pl.BlockSpec((1, 1, PAST_LEN, MLA_ROPE), lambda b, j: (b, layer, 0, 0))]
                 + [_full(t.shape) for t in tabs] + [_full(a.shape) for a in small],
        out_specs=pl.BlockSpec((LAT_QB, MIX), lambda b, j: (b * nq + j, 0)),
        out_shape=jax.ShapeDtypeStruct((nb * DEC_SEQ, MIX), F32),
        scratch_shapes=[pltpu.VMEM((LAT_KEYS, MLA_HEADS * LANES), BF16), pltpu.VMEM((LAT_KEYS, MIX), BF16)],
        compiler_params=_cparams(("arbitrary", "arbitrary")),
        name="mla_latent",
    )(zm, ckvc, krc, *tabs, *small)


def _s5_prep_kernel(bb_ref, qj_ref, m_ref):
    bb, qj = bb_ref[0], qj_ref[0]
    kt = _dot_nt_f32(bb[:, 0:LANES], qj[:, 0:LANES])
    kr = _dot_nt_f32(bb[:, LANES:2 * LANES], qj[:, LANES:2 * LANES])
    lane = lax.broadcasted_iota(jnp.int32, kt.shape, 1)
    blocks = []
    for s_pos in range(S5_CHUNK):
        sh_f = S5_GROUP * s_pos
        sh_b = S5_GROUP * (S5_CHUNK - 1 - s_pos)
        f = jnp.where(lane >= sh_f, pltpu.roll(kt, sh_f, axis=1), 0.0) if sh_f else kt
        b = jnp.where(lane < S5_ROW - sh_b, pltpu.roll(kr, S5_ROW - sh_b, axis=1), 0.0) if sh_b else kr
        blocks.append(f + b)
    m_ref[0] = jnp.concatenate(blocks, axis=0).astype(BF16)


def _s5_prep(lw):
    blk = lambda *shape: pl.BlockSpec((1,) + shape, lambda i: (i, 0, 0))
    return pl.pallas_call(
        _s5_prep_kernel,
        grid=(S5_GROUPS,),
        in_specs=[blk(S5_GROUP, S5_ROW), blk(S5_ROW, S5_ROW)],
        out_specs=blk(S5_ROW, S5_ROW),
        out_shape=jax.ShapeDtypeStruct((S5_GROUPS, S5_ROW, S5_ROW), BF16),
        compiler_params=_cparams(("arbitrary",)),
        name="s5_prep",
    )(lw["s5_bb"], lw["s5_qj"])


S5_TILE_BLOCKS = LANES // S5_GROUP


S5_INTERLEAVE = 2


def _block_transpose(xs, lane_blk):
    n = len(xs)
    d = n // 2
    while d >= 1:
        hi = (lane_blk & d) != 0
        nxt = list(xs)
        for i in range(n):
            if i & d:
                continue
            lo_arr, hi_arr = xs[i], xs[i + d]
            nxt[i] = jnp.where(hi, pltpu.roll(hi_arr, d * S5_GROUP, axis=1), lo_arr)
            nxt[i + d] = jnp.where(hi, hi_arr, pltpu.roll(lo_arr, LANES - d * S5_GROUP, axis=1))
        xs = nxt
        d //= 2
    return xs


def _s5_kernel(nb, nch, ua_ref, ub_ref, m_ref, ws_ref, wct_ref, a_ref, d_ref, h0_ref, ya_ref, yb_ref, fin_ref,
               ug_scr, yg_scr, *tmp_scr):
    rows = nb * nch
    per = S5_TILE_BLOCKS
    lane_blk = lax.broadcasted_iota(jnp.int32, (rows, LANES), 1) // S5_GROUP
    u_refs = (ua_ref, ub_ref)
    y_refs = (ya_ref, yb_ref)
    tile = lambda t: slice(t * LANES, (t + 1) * LANES)

    for half in range(S5_GROUPS // per):
        for t in range(S5_CHUNK // per):
            pieces = [u_refs[half][pl.ds(t * per + i, rows, stride=S5_CHUNK), :] for i in range(per)]
            for j, arr in enumerate(_block_transpose(pieces, lane_blk)):
                ug_scr[half * per + j, :, tile(t)] = arr

    def group_set(i, carry):
        gs = [i * S5_INTERLEAVE + n for n in range(S5_INTERLEAVE)]
        sets = [tmp_scr[6 * n:6 * n + 6] for n in range(S5_INTERLEAVE)]
        state = []
        for g, (sf_scr, sb_scr, sfs_scr, sbs_scr, _, _) in zip(gs, sets):
            s = _dot(ug_scr[g].astype(BF16), ws_ref[g])
            sf_scr[...] = s[:, 0:LANES]
            sb_scr[...] = s[:, LANES:2 * LANES]
            sfs_scr[...] = pltpu.roll(s[:, 0:LANES], S5_N, axis=1)
            sbs_scr[...] = pltpu.roll(s[:, LANES:2 * LANES], S5_N, axis=1)
            h0 = h0_ref[g]
            h_f = h0[:, 0:LANES]
            h_b = h0[:, LANES:2 * LANES]
            state.append([h_f, pltpu.roll(h_f, S5_N, axis=1), h_b, pltpu.roll(h_b, S5_N, axis=1)])
        for k in range(nch):
            rows_f = pl.ds(k, nb, stride=nch)
            rows_b = pl.ds(nch - 1 - k, nb, stride=nch)
            for n, (g, (sf_scr, sb_scr, sfs_scr, sbs_scr, hf_scr, hb_scr)) in enumerate(zip(gs, sets)):
                a = a_ref[g]
                h_f, g_f, h_b, g_b = state[n]
                hf_scr[rows_f, :] = h_f
                hb_scr[rows_b, :] = h_b
                state[n] = [a[0:1] * h_f + a[1:2] * g_f + sf_scr[rows_f, :],
                            a[0:1] * g_f + a[2:3] * h_f + sfs_scr[rows_f, :],
                            a[3:4] * h_b + a[4:5] * g_b + sb_scr[rows_b, :],
                            a[3:4] * g_b + a[5:6] * h_b + sbs_scr[rows_b, :]]
        for n, (g, (_, _, _, _, hf_scr, hb_scr)) in enumerate(zip(gs, sets)):
            fin_ref[g] = jnp.concatenate([state[n][0], state[n][2]], axis=-1)
            hp = jnp.concatenate([hf_scr[...], hb_scr[...]], axis=-1).astype(BF16)
            x = ug_scr[g]
            yg_scr[g] = _dot(x.astype(BF16), m_ref[g]) + _dot_nt(hp, wct_ref[g]) + d_ref[g] * x
        return carry

    lax.fori_loop(0, S5_GROUPS // S5_INTERLEAVE, group_set, 0)

    for half in range(S5_GROUPS // per):
        for t in range(S5_CHUNK // per):
            pieces = [yg_scr[half * per + j, :, tile(t)] for j in range(per)]
            for i, arr in enumerate(_block_transpose(pieces, lane_blk)):
                y_refs[half][pl.ds(t * per + i, rows, stride=S5_CHUNK), :] = arr


def _s5_mixer(u, m, lw, h0, nb, nb_step):
    n = u.shape[0]
    nch = n // nb // S5_CHUNK
    tok = nb_step * nch * S5_CHUNK
    rows = nb_step * nch
    g = S5_GROUPS
    res = lambda a: pl.BlockSpec(a.shape, lambda i: (0, 0, 0))
    state = pl.BlockSpec((g, nb_step, S5_ROW), lambda i: (0, i, 0))
    half = lambda j: pl.BlockSpec((tok, LANES), lambda i: (i, j))
    ya, yb, fin = pl.pallas_call(
        functools.partial(_s5_kernel, nb_step, nch),
        grid=(nb // nb_step,),
        in_specs=[half(0), half(1), res(m), res(lw["s5_ws"]), res(lw["s5_wct"]), res(lw["s5_a"]), res(lw["s5_d"]), state],
        out_specs=[half(0), half(0), state],
        out_shape=[jax.ShapeDtypeStruct((n, LANES), F32), jax.ShapeDtypeStruct((n, LANES), F32),
                   jax.ShapeDtypeStruct((g, nb, S5_ROW), F32)],
        scratch_shapes=[pltpu.VMEM((g, rows, S5_ROW), F32), pltpu.VMEM((g, rows, S5_ROW), F32)]
                       + [pltpu.VMEM((rows, LANES), F32)] * (6 * S5_INTERLEAVE),
        compiler_params=_cparams(("arbitrary",)),
        name="s5_mixer",
    )(u, u, m, lw["s5_ws"], lw["s5_wct"], lw["s5_a"], lw["s5_d"], h0)
    return ya, yb, fin


def _outproj_kernel(ona_ref, ys5a_ref, ys5b_ref, ogq_ref, omla_ref, x_ref, mod_ref, w_ref, wglu_ref, bglu_ref,
                    o_ref):
    y = jax.nn.gelu(jnp.concatenate([ys5a_ref[...], ys5b_ref[...]], axis=-1), approximate=True)
    y = y * jax.nn.sigmoid(_dot(y.astype(BF16), wglu_ref[...]) + bglu_ref[...])
    mixed = _dot(ona_ref[...].astype(BF16), w_ref[0:MIX, :])
    mixed = mixed + _dot(y.astype(BF16), w_ref[MIX:2 * MIX, :])
    mixed = mixed + _dot(ogq_ref[...].astype(BF16), w_ref[2 * MIX:3 * MIX, :])
    mixed = mixed + _dot(omla_ref[...].astype(BF16), w_ref[3 * MIX:4 * MIX, :])
    o_ref[...] = x_ref[...] + mod_ref[0, 2:3, :] * mixed


def _outproj(ona, ys5a, ys5b, ogq, omla, x, mods, lw, mod_base, rows_per_mod):
    n = x.shape[0]
    tm = 512
    row = lambda i: (mod_base + (i * tm) // rows_per_mod, 0, 0)
    part = pl.BlockSpec((tm, MIX), lambda i: (i, 0))
    half = pl.BlockSpec((tm, LANES), lambda i: (i, 0))
    return pl.pallas_call(
        _outproj_kernel,
        grid=(n // tm,),
        in_specs=[part, half, half, part, part,
                  pl.BlockSpec((tm, D_MODEL), lambda i: (i, 0)),
                  pl.BlockSpec((1, 6, D_MODEL), row),
                  _full(lw["w_out"].shape), _full(lw["w_glu"].shape), _full(lw["b_glu"].shape)],
        out_specs=pl.BlockSpec((tm, D_MODEL), lambda i: (i, 0)),
        out_shape=jax.ShapeDtypeStruct((n, D_MODEL), F32),
        compiler_params=_cparams(("arbitrary",)),
        name="outproj",
    )(ona, ys5a, ys5b, ogq, omla, x, mods, lw["w_out"], lw["w_glu"], lw["b_glu"])


FF_ROWS = 1024
FFN_VMEM_LIMIT = 58 * 1024 * 1024


def _ffn_kernel(seq, x_ref, mod_ref, g_ref, wup_ref, cw_ref, cb_ref, wd_ref, o_ref, h_scr, ug_scr, uu_scr, act_scr):
    x = x_ref[...]
    y = x * lax.rsqrt(jnp.mean(x * x, axis=-1, keepdims=True) + EPS) * g_ref[...]
    h_scr[...] = (y * (1.0 + mod_ref[0, 4:5, :]) + mod_ref[0, 3:4, :]).astype(BF16)

    pos = lax.broadcasted_iota(jnp.int32, (FF_ROWS, FF_CHUNK), 0) % seq
    first = pos == 0
    last = pos == seq - 1

    def conv(u, cw, cb):
        prev = jnp.where(first, 0.0, pltpu.roll(u, 1, axis=0))
        nxt = jnp.where(last, 0.0, pltpu.roll(u, FF_ROWS - 1, axis=0))
        return cw[0:1, :] * prev + cw[1:2, :] * u + cw[2:3, :] * nxt + cb

    def cols(j, base=0):
        return pl.ds(pl.multiple_of(base + j * FF_CHUNK, FF_CHUNK), FF_CHUNK)

    def up_proj(j, slot):
        h = h_scr[...]
        ug_scr[slot] = _dot(h, wup_ref[:, cols(j)])
        uu_scr[slot] = _dot(h, wup_ref[:, cols(j, D_FF)])

    def gate_act(j, slot):
        gate = conv(ug_scr[slot], cw_ref[:, cols(j)], cb_ref[:, cols(j)])
        up = conv(uu_scr[slot], cw_ref[:, cols(j, D_FF)], cb_ref[:, cols(j, D_FF)])
        act_scr[slot] = (gate * jax.nn.sigmoid(gate) * up).astype(BF16)

    def down_proj(j, slot):
        return _dot(act_scr[slot], wd_ref[cols(j), :])

    last_j = FF_STEPS - 1
    up_proj(0, 0)
    up_proj(1, 1)
    gate_act(0, 0)
    up_proj(2, 0)
    gate_act(1, 1)
    o_ref[...] = down_proj(0, 0)

    def body(j, carry):
        slot = j % 2
        o_ref[...] += down_proj(j, slot)
        gate_act(j + 1, 1 - slot)
        up_proj(j + 2, slot)
        return carry

    lax.fori_loop(1, last_j - 1, body, 0)
    gate_act(last_j, last_j % 2)
    o_ref[...] += down_proj(last_j - 1, (last_j - 1) % 2)
    o_ref[...] = x_ref[...] + mod_ref[0, 5:6, :] * (o_ref[...] + down_proj(last_j, last_j % 2))


def _ffn(x, mods, lw, fw, layer, mod_base, rows_per_mod, seq):
    n = x.shape[0]
    row = lambda i: (mod_base + (i * FF_ROWS) // rows_per_mod, 0, 0)
    resident = lambda a: pl.BlockSpec((None,) + a.shape[1:], lambda i: (layer,) + (0,) * (a.ndim - 1),
                                      pipeline_mode=pl.Buffered(1))
    return pl.pallas_call(
        functools.partial(_ffn_kernel, seq),
        grid=(n // FF_ROWS,),
        in_specs=[pl.BlockSpec((FF_ROWS, D_MODEL), lambda i: (i, 0)),
                  pl.BlockSpec((1, 6, D_MODEL), row),
                  pl.BlockSpec((1, D_MODEL), lambda i: (0, 0)),
                  resident(fw["w_up"]), resident(fw["conv_w"]), resident(fw["conv_b"]), resident(fw["w_down"])],
        out_specs=pl.BlockSpec((FF_ROWS, D_MODEL), lambda i: (i, 0)),
        out_shape=jax.ShapeDtypeStruct((n, D_MODEL), F32),
        scratch_shapes=[pltpu.VMEM((FF_ROWS, D_MODEL), BF16),
                        pltpu.VMEM((2, FF_ROWS, FF_CHUNK), F32), pltpu.VMEM((2, FF_ROWS, FF_CHUNK), F32),
                        pltpu.VMEM((2, FF_ROWS, FF_CHUNK), BF16)],
        compiler_params=pltpu.CompilerParams(dimension_semantics=("arbitrary",), vmem_limit_bytes=FFN_VMEM_LIMIT),
        name="conv_ffn",
    )(x, mods, lw["norm2_g"], fw["w_up"], fw["conv_w"], fw["conv_b"], fw["w_down"])


def _rope_tables():
    pos = np.arange(DEC_SEQ)

    def ang(p, half):
        inv = ROPE_BASE ** (-np.arange(half, dtype=np.float64) / half)
        a = p.astype(np.float64)[:, None] * inv[None, :]
        return np.concatenate([a, a], axis=-1)

    def tables(dim, lanes_before, lanes_after, reps):
        a = np.concatenate([ang(pos // GRID_W, dim // 4), ang(pos % GRID_W, dim // 4)], axis=-1)
        half = dim // 4
        first = (np.arange(dim) % (2 * half)) < half
        cos = np.cos(a)
        sin_a = np.where(first[None, :], -np.sin(a), 0.0)
        sin_b = np.where(first[None, :], 0.0, np.sin(a))

        def place(t, fill):
            t = np.concatenate([np.full((DEC_SEQ, lanes_before), fill), t,
                                np.full((DEC_SEQ, lanes_after), fill)], axis=-1)
            return jnp.asarray(np.tile(t, (1, reps)), F32)

        return place(cos, 1.0), place(sin_a, 0.0), place(sin_b, 0.0)

    gq = tables(DH, 0, 0, LANES // DH)
    mla = tables(MLA_ROPE, MLA_NOPE, LANES - MLA_QK, 1)
    return gq, mla


def _na_bias_tables(rpb):
    qc = np.arange(GRID_W)[:, None]
    kc = np.arange(GRID_W)[None, :]
    col_start = np.clip(qc - NA_WIN_C // 2, 0, GRID_W - NA_WIN_C)
    valid = (kc >= col_start) & (kc < col_start + NA_WIN_C)
    d_c = np.clip(kc - qc, 1 - NA_WIN_C, NA_WIN_C - 1) + NA_WIN_C - 1
    nrel = 2 * NA_WIN_C - 1
    onehot = jnp.asarray((d_c.reshape(-1)[None, :] == np.arange(nrel)[:, None]).astype(np.float32))
    t = jnp.dot(rpb.astype(F32).reshape(-1, nrel), onehot, precision=lax.Precision.HIGHEST)
    t = t.reshape(NA_HEADS, 2 * NA_WIN_R - 1, GRID_W, GRID_W)
    t = jnp.where(jnp.asarray(valid)[None, None], t, NEG)
    t = t.transpose(1, 0, 2, 3).reshape(2 * NA_WIN_R - 1, NA_HEADS * GRID_W, GRID_W)
    return jnp.concatenate([t[:-1], t[1:]], axis=-1)


def _s5_matrices(lam_re, lam_im, log_dt, b_re, b_im, c_re, c_im):
    c = S5_CHUNK
    cat = lambda parts: jnp.concatenate(parts, axis=-1)
    pw, bre, bim, cre, cim = [], [], [], [], []
    for d in range(2):
        dt = jnp.exp(log_dt[d].astype(F32))[:, None]
        lr, li = lam_re[d].astype(F32), lam_im[d].astype(F32)
        mag = jnp.exp(lr * dt)
        a_re, a_im = mag * jnp.cos(li * dt), mag * jnp.sin(li * dt)
        den = lr * lr + li * li
        f_re = ((a_re - 1.0) * lr + a_im * li) / den
        f_im = (a_im * lr - (a_re - 1.0) * li) / den
        br, bi = b_re[d].astype(F32), b_im[d].astype(F32)
        bre.append((f_re[..., None] * br - f_im[..., None] * bi).transpose(0, 2, 1))
        bim.append((f_re[..., None] * bi + f_im[..., None] * br).transpose(0, 2, 1))
        cre.append(c_re[d].astype(F32))
        cim.append(c_im[d].astype(F32))
        p_re, p_im = [jnp.ones_like(a_re)], [jnp.zeros_like(a_re)]
        for _ in range(c):
            p_re, p_im = (p_re + [p_re[-1] * a_re - p_im[-1] * a_im], p_im + [p_re[-1] * a_im + p_im[-1] * a_re])
        p_re, p_im = jnp.stack(p_re, axis=1), jnp.stack(p_im, axis=1)
        pw.append((p_re, p_im))
    (pfr, pfi), (pbr, pbi) = pw
    up = lambda x, off=0: x[:, off:off + c]
    dn = lambda x, off=0: x[:, off:off + c][:, ::-1]
    outer = lambda x, y: (x[:, :, None, :] * y[:, None, :, :])
    e_fr, e_fi, e_br, e_bi = dn(pfr), dn(pfi), up(pbr), up(pbi)
    bb = cat([bre[0], bim[0], bre[1], bim[1]])
    ws = (outer(cat([e_fr, e_fr, e_br, e_br]), bb)
          + outer(cat([-e_fi, e_fi, -e_bi, e_bi]), cat([bim[0], bre[0], bim[1], bre[1]])))
    e_fr, e_fi, e_br, e_bi = up(pfr, 1), up(pfi, 1), dn(pbr, 1), dn(pbi, 1)
    c_a, c_b = cat([cre[0], cre[0], cre[1], cre[1]]), cat([cim[0], cim[0], cim[1], cim[1]])
    wct = outer(cat([e_fr, -e_fi, e_br, -e_bi]), c_a) + outer(cat([-e_fi, -e_fr, -e_bi, -e_br]), c_b)
    e_fr, e_fi, e_br, e_bi = up(pfr), up(pfi), dn(pbr), dn(pbi)
    qj = outer(cat([e_fr, -e_fi, e_br, -e_bi]), c_a) + outer(cat([-e_fi, -e_fr, -e_bi, -e_br]), c_b)
    rows = lambda w: w.reshape(S5_GROUPS, S5_ROW, 4 * S5_N).astype(BF16)
    av = []
    for p_re, p_im in ((pfr[:, c], pfi[:, c]), (pbr[:, c], pbi[:, c])):
        av += [cat([p_re, p_re]), cat([-p_im, p_im]), cat([p_im, -p_im])]
    return rows(ws), rows(wct), bb, qj.reshape(S5_GROUPS, S5_ROW, 4 * S5_N), jnp.stack(av, axis=1)


def _layer_weights(l, p):
    w = p["w_in"][l]
    q0 = Z_NA + Z_S5
    gq = lambda h: w[:, q0 + h * DH:q0 + (h + 1) * DH]
    m0 = q0 + 2 * MIX
    w_in = jnp.concatenate(
        [w[:, 0:q0], gq(0), gq(2), gq(1), gq(3), w[:, q0 + MIX:m0],
         w[:, m0 + MLA_QLORA:m0 + MLA_QLORA + MLA_KVLORA], w[:, m0:m0 + MLA_QLORA],
         w[:, m0 + MLA_QLORA + MLA_KVLORA:], jnp.zeros((D_MODEL, Z_ALL - w.shape[1]), w.dtype)],
        axis=1).astype(BF16)
    wo = p["w_out"][l]
    og = lambda h: wo[2 * MIX + h * DH:2 * MIX + (h + 1) * DH]
    w_out = jnp.concatenate([wo[0:2 * MIX], og(0), og(2), og(1), og(3), wo[3 * MIX:]], axis=0).astype(BF16)
    pad_head = lambda g: jnp.tile(jnp.pad(g, (0, LANES - MLA_QK)), MLA_HEADS)[None, :]
    wuq = jnp.pad(p["mla_w_uq"][l].reshape(MLA_QLORA, MLA_HEADS, MLA_QK),
                  ((0, 2 * LANES - MLA_QLORA), (0, 0), (0, LANES - MLA_QK))).reshape(2 * LANES, MLA_HEADS * LANES)
    wukv = p["mla_w_ukv"][l].reshape(MLA_KVLORA, MLA_HEADS, MLA_NOPE + DH)
    wuk = jnp.pad(wukv[:, :, :MLA_NOPE], ((0, 0), (0, 0), (0, LANES - MLA_NOPE))).reshape(MLA_KVLORA, MLA_HEADS * LANES)
    wuv = wukv[:, :, MLA_NOPE:].reshape(MLA_KVLORA, MIX)
    s5_ws, s5_wct, s5_bb, s5_qj, s5_a = _s5_matrices(p["s5_lam_re"][l], p["s5_lam_im"][l], p["s5_log_dt"][l],
                                                     p["s5_b_re"][l], p["s5_b_im"][l], p["s5_c_re"][l], p["s5_c_im"][l])
    return dict(
        norm1_g=p["norm1_g"][l][None, :], norm2_g=p["norm2_g"][l][None, :],
        w_in=w_in, w_out=w_out,
        naq_g=jnp.tile(p["na_qn"][l], NA_HEADS)[None, :], nak_g=jnp.tile(p["na_kn"][l], NA_HEADS)[None, :],
        gqq_g=jnp.tile(p["gq_qn"][l], GQ_HEADS)[None, :], gqk_g=jnp.tile(p["gq_kn"][l], GQ_KV)[None, :],
        qa_g=jnp.pad(p["mla_qa_g"][l], (0, 2 * LANES - MLA_QLORA))[None, :], kva_g=p["mla_kva_g"][l][None, :],
        mqn_g=pad_head(p["mla_qn"][l]), mkn_g=pad_head(p["mla_kn"][l]),
        wuq=wuq.astype(BF16), wuk=wuk.astype(BF16), wuv=wuv.astype(BF16),
        na_bias=_na_bias_tables(p["na_rpb"][l]),
        s5_ws=s5_ws, s5_wct=s5_wct, s5_bb=s5_bb, s5_qj=s5_qj, s5_a=s5_a,
        s5_d=jnp.tile(p["s5_d"][l].reshape(S5_GROUPS, 1, S5_GROUP), (1, 1, S5_CHUNK)),
        w_glu=p["s5_w_glu"][l].astype(BF16), b_glu=p["s5_b_glu"][l][None, :],
    )


def kernel(x_prompt, x_sample, cache_na_k, cache_na_v, state_s5, cache_gqa_k, cache_gqa_v, cache_mla_ckv,
           cache_mla_krope, c, c_ctx, norm1_g, norm2_g, ada_w, ada_b, w_in, na_qn, na_kn, na_rpb, s5_lam_re,
           s5_lam_im, s5_log_dt, s5_b_re, s5_b_im, s5_c_re, s5_c_im, s5_d, s5_w_glu, s5_b_glu, gq_qn, gq_kn,
           mla_qa_g, mla_kva_g, mla_w_uq, mla_w_ukv, mla_qn, mla_kn, w_out, ffn_w_up, ffn_conv_w, ffn_conv_b,
           ffn_w_down):
    p = dict(norm1_g=norm1_g, norm2_g=norm2_g, w_in=w_in, na_qn=na_qn, na_kn=na_kn, na_rpb=na_rpb,
             s5_lam_re=s5_lam_re, s5_lam_im=s5_lam_im, s5_log_dt=s5_log_dt, s5_b_re=s5_b_re, s5_b_im=s5_b_im,
             s5_c_re=s5_c_re, s5_c_im=s5_c_im, s5_d=s5_d, s5_w_glu=s5_w_glu, s5_b_glu=s5_b_glu,
             gq_qn=gq_qn, gq_kn=gq_kn, mla_qa_g=mla_qa_g, mla_kva_g=mla_kva_g, mla_w_uq=mla_w_uq,
             mla_w_ukv=mla_w_ukv, mla_qn=mla_qn, mla_kn=mla_kn, w_out=w_out, ffn_w_up=ffn_w_up,
             ffn_conv_w=ffn_conv_w, ffn_conv_b=ffn_conv_b, ffn_w_down=ffn_w_down)
    nb, nd = BATCH, DEC_BATCH
    cvec = jnp.concatenate([c_ctx[None, :], c, jnp.zeros((8 - 1 - nd, D_MODEL), F32)], axis=0)
    mods_all = _ada_mods(cvec, ada_w, ada_b).reshape(DEPTH, 8, 6, D_MODEL)
    gq_tabs, mla_tabs = _rope_tables()
    fw = dict(w_up=ffn_w_up.astype(BF16), conv_w=ffn_conv_w, conv_b=ffn_conv_b[:, None, :],
              w_down=ffn_w_down.astype(BF16))

    xc = x_prompt.reshape(nb * SEQ, D_MODEL)
    xl = x_sample.reshape(nd * DEC_SEQ, D_MODEL)
    zero_state = jnp.zeros((S5_GROUPS, nb, 4 * S5_N), F32)
    caches = [jnp.zeros((nb, DEPTH) + shp, F32) for shp in CACHE_SHAPES]
    s5_states = []
    for l in range(DEPTH):
        lw = _layer_weights(l, p)
        mods = mods_all[l]
        s5_m = _s5_prep(lw)

        zna, zs5, zgq, zm = _inproj(xc, mods, lw["norm1_g"], lw["w_in"], 0, nb * SEQ)
        (ona, ogq, omla), caches = _ctx_attn(zna, zgq, zm, lw, l, caches)
        ys5a, ys5b, fin = _s5_mixer(zs5, s5_m, lw, zero_state, nb, S5_CTX_BLOCK)
        xc = _outproj(ona, ys5a, ys5b, ogq, omla, xc, mods, lw, 0, nb * SEQ)
        xc = _ffn(xc, mods, lw, fw, l, 0, nb * SEQ, SEQ)
        s5_states.append(fin.reshape(S5_GROUPS, nb, 2, 2, S5_N).transpose(1, 2, 3, 0, 4))

        zna, zs5, zgq, zm = _inproj(xl, mods, lw["norm1_g"], lw["w_in"], 1, DEC_SEQ)
        ona = _na_lat(zna, cache_na_k, cache_na_v, l, lw["na_bias"], lw)
        ogq = _gq_lat(zgq, cache_gqa_k, cache_gqa_v, l, gq_tabs, lw)
        omla = _mla_lat(zm, cache_mla_ckv, cache_mla_krope, l, mla_tabs, lw)
        h0 = state_s5[:, l].astype(F32).transpose(3, 0, 1, 2, 4).reshape(S5_GROUPS, nd, 4 * S5_N)
        ys5a, ys5b, _ = _s5_mixer(zs5, s5_m, lw, h0, nd, nd)
        xl = _outproj(ona, ys5a, ys5b, ogq, omla, xl, mods, lw, 1, DEC_SEQ)
        xl = _ffn(xl, mods, lw, fw, l, 1, DEC_SEQ, DEC_SEQ)

    return (xc.reshape(nb, SEQ, D_MODEL), xl.reshape(nd, DEC_SEQ, D_MODEL), caches[0], caches[1],
            jnp.stack(s5_states, axis=1), caches[2], caches[3], caches[4], caches[5])
```

```python
import functools

import numpy as np
import jax
import jax.numpy as jnp
from jax import lax
from jax.experimental import pallas as pl
from jax.experimental.pallas import tpu as pltpu

F32 = jnp.float32
BF16 = jnp.bfloat16

D_MODEL = 1024
BATCH = 32
SEQ = 256
DEPTH = 2
DEC_BATCH = 4
DEC_SEQ = 1024
PAST_LEN = 512
GRID_W = 64
GRID_ROWS = DEC_SEQ // GRID_W
MIX = D_MODEL // 4
DH = 64
NA_HEADS = MIX // DH
NA_WIN_R = 8
NA_WIN_C = 16
S5_GROUP = 16
S5_GROUPS = MIX // S5_GROUP
S5_N = 64
GQ_HEADS = MIX // DH
GQ_KV = GQ_HEADS // 2
MLA_HEADS = MIX // DH
MLA_NOPE = 64
MLA_ROPE = 32
MLA_QK = MLA_NOPE + MLA_ROPE
MLA_QLORA = (3 * D_MODEL) // 16
MLA_KVLORA = D_MODEL // 8
D_FF = 128 * ((8 * D_MODEL // 3 + 127) // 128)
ROPE_BASE = 10000.0
EPS = 1e-6
NEG = -1e30

LANES = 128
MXU_DIM = 256
VMEM_LIMIT = 48 * 1024 * 1024
MXU_DEN_MIN_KEYS = 1024

Z_NA = 3 * MIX
Z_S5 = MIX
Z_GQ = 2 * MIX
Z_MLA = 3 * LANES
Z_ALL = Z_NA + Z_S5 + Z_GQ + Z_MLA

S5_CHUNK = 16
S5_ROW = S5_CHUNK * S5_GROUP
S5_CTX_BLOCK = 16

FF_CHUNK = 256
FF_STEPS = D_FF // FF_CHUNK


def _cparams(sem):
    return pltpu.CompilerParams(dimension_semantics=sem, vmem_limit_bytes=VMEM_LIMIT)


def _dot(a, b):
    return jnp.dot(a, b, preferred_element_type=F32)


def _dot_nt(a, b):
    return lax.dot_general(a, b, (((1,), (1,)), ((), ())), preferred_element_type=F32)


def _dot_nt_f32(a, b):
    def split(x):
        hi = x.astype(BF16)
        r1 = x - hi.astype(F32)
        mid = r1.astype(BF16)
        return hi, mid, (r1 - mid.astype(F32)).astype(BF16)

    pa, pb = split(a), split(b)
    acc = None
    for i in range(3):
        for j in range(3 - i):
            term = _dot_nt(pa[i], pb[j])
            acc = term if acc is None else acc + term
    return acc


def _rms_rows(x, gain, denom=None, valid=None):
    xx = x * x
    if valid is not None and valid != x.shape[-1]:
        lane = lax.broadcasted_iota(jnp.int32, x.shape, 1)
        xx = jnp.where(lane < valid, xx, 0.0)
    denom = denom or (valid or x.shape[-1])
    ss = jnp.sum(xx, axis=-1, keepdims=True)
    return x * lax.rsqrt(ss / denom + EPS) * gain


def _seg_rms(x, seg, denom, gain):
    rows, width = x.shape
    if seg % LANES == 0:
        parts = []
        for s in range(width // seg):
            xs = x[:, s * seg:(s + 1) * seg]
            ss = jnp.sum(xs * xs, axis=-1, keepdims=True)
            parts.append(xs * lax.rsqrt(ss / denom + EPS))
        return jnp.concatenate(parts, axis=-1) * gain
    same_seg = (lax.broadcasted_iota(jnp.int32, (width, width), 0) // seg
                == lax.broadcasted_iota(jnp.int32, (width, width), 1) // seg)
    ones = jnp.where(same_seg, 1.0, 0.0).astype(BF16)
    xx = x * x
    hi = xx.astype(BF16)
    lo = (xx - hi.astype(F32)).astype(BF16)
    ss = _dot(hi, ones) + _dot(lo, ones)
    return x * lax.rsqrt(ss / denom + EPS) * gain


def _rope(x, cos, sin_a, sin_b, half):
    tiles = []
    for t in range(x.shape[-1] // LANES):
        xt = x[:, t * LANES:(t + 1) * LANES]
        up = pltpu.roll(xt, LANES - half, axis=1)
        dn = pltpu.roll(xt, half, axis=1)
        tiles.append(xt * cos + up * sin_a + dn * sin_b)
    return tiles[0] if len(tiles) == 1 else jnp.concatenate(tiles, axis=-1)


def _packed_attn(q, parts, nseg, scale, biases=None):
    tq, width = q.shape
    seg_id = lax.broadcasted_iota(jnp.int32, q.shape, 1) // DH
    lhs = jnp.concatenate([jnp.where(seg_id == h, q, 0.0) for h in range(nseg)], axis=0).astype(BF16)
    scores = []
    for i, (kb, _) in enumerate(parts):
        s = _dot_nt(lhs, kb) * scale
        if biases is not None and biases[i] is not None:
            s = s + biases[i]
        scores.append(s)
    m = scores[0].max(axis=-1, keepdims=True)
    for s in scores[1:]:
        m = jnp.maximum(m, s.max(axis=-1, keepdims=True))
    mxu_den = sum(kb.shape[0] for kb, _ in parts) >= MXU_DEN_MIN_KEYS
    den = None
    pv = None
    for s, (_, vb) in zip(scores, parts):
        p = jnp.exp(s - m)
        pb = p.astype(BF16)
        ps = _dot(pb, jnp.ones(vb.shape, BF16)) if mxu_den else p.sum(axis=-1, keepdims=True)
        den = ps if den is None else den + ps
        c = _dot(pb, vb)
        pv = c if pv is None else pv + c
    pv = pv / den
    out = jnp.zeros((tq, width), F32)
    for h in range(nseg):
        out = out + jnp.where(seg_id == h, pv[h * tq:(h + 1) * tq], 0.0)
    return out


def _mla_attn(q, parts, scale):
    tq = q.shape[0]
    vseg = None
    out = jnp.zeros((tq, MIX), F32)
    for h in range(MLA_HEADS):
        qh = q[:, h * LANES:(h + 1) * LANES].astype(BF16)
        scores = [_dot_nt(qh, kb[:, h * LANES:(h + 1) * LANES]) * scale for kb, _ in parts]
        m = scores[0].max(axis=-1, keepdims=True)
        for s in scores[1:]:
            m = jnp.maximum(m, s.max(axis=-1, keepdims=True))
        den = None
        pv = None
        for s, (_, vb) in zip(scores, parts):
            p = jnp.exp(s - m)
            ps = p.sum(axis=-1, keepdims=True)
            den = ps if den is None else den + ps
            vseg = lax.broadcasted_iota(jnp.int32, vb.shape, 1) // DH
            c = _dot(p.astype(BF16), jnp.where(vseg == h, vb, jnp.zeros_like(vb)))
            pv = c if pv is None else pv + c
        out = out + pv / den
    return out


def _mla_qkv(zm, qa_g, kva_g, mqn_g, mkn_g, wuq, wuk, wuv):
    ckv = _rms_rows(zm[:, 0:LANES], kva_g)
    col3 = zm[:, 2 * LANES:3 * LANES]
    lane = lax.broadcasted_iota(jnp.int32, col3.shape, 1)
    kr_placed = jnp.where((lane >= MLA_NOPE) & (lane < MLA_QK), col3, 0.0)
    cq = _rms_rows(zm[:, LANES:3 * LANES], qa_g, valid=MLA_QLORA)
    q = _seg_rms(_dot(cq.astype(BF16), wuq), LANES, MLA_QK, mqn_g)
    k, v = _mla_kv(ckv, kr_placed, mkn_g, wuk, wuv)
    return q, k, v, ckv, col3


def _mla_kv(ckv, kr_placed, mkn_g, wuk, wuv):
    cb = ckv.astype(BF16)
    kf = _dot(cb, wuk) + jnp.concatenate([kr_placed] * MLA_HEADS, axis=-1)
    return _seg_rms(kf, LANES, MLA_QK, mkn_g), _dot(cb, wuv)


def _ada_kernel(c_ref, w_ref, b_ref, o_ref):
    cv = c_ref[...]
    s = cv * jax.nn.sigmoid(cv)
    o_ref[0] = _dot(s.astype(BF16), w_ref[0].astype(BF16)) + b_ref[0]


def _ada_mods(cvec, ada_w, ada_b):
    tn = 1536
    n = ada_w.shape[-1]
    return pl.pallas_call(
        _ada_kernel,
        grid=(DEPTH, n // tn),
        in_specs=[pl.BlockSpec((8, D_MODEL), lambda l, j: (0, 0)),
                  pl.BlockSpec((1, D_MODEL, tn), lambda l, j: (l, 0, j)),
                  pl.BlockSpec((1, 1, tn), lambda l, j: (l, 0, j))],
        out_specs=pl.BlockSpec((1, 8, tn), lambda l, j: (l, 0, j)),
        out_shape=jax.ShapeDtypeStruct((DEPTH, 8, n), F32),
        compiler_params=_cparams(("arbitrary", "arbitrary")),
        name="ada_mods",
    )(cvec, ada_w, ada_b.reshape(DEPTH, 1, n))


def _inproj_kernel(x_ref, mod_ref, g_ref, w_ref, ona_ref, os5_ref, ogq_ref, omla_ref):
    x = x_ref[...]
    y = x * lax.rsqrt(jnp.mean(x * x, axis=-1, keepdims=True) + EPS) * g_ref[...]
    h = y * (1.0 + mod_ref[0, 1:2, :]) + mod_ref[0, 0:1, :]
    z = _dot(h.astype(BF16), w_ref[...])
    ona_ref[...] = z[:, 0:Z_NA]
    os5_ref[...] = z[:, Z_NA:Z_NA + Z_S5]
    ogq_ref[...] = z[:, Z_NA + Z_S5:Z_NA + Z_S5 + Z_GQ]
    omla_ref[...] = z[:, Z_NA + Z_S5 + Z_GQ:Z_ALL]


def _inproj(x, mods, g, w, mod_base, rows_per_mod):
    n = x.shape[0]
    tm = 512
    row = lambda i: (mod_base + (i * tm) // rows_per_mod, 0, 0)
    widths = (Z_NA, Z_S5, Z_GQ, Z_MLA)
    return pl.pallas_call(
        _inproj_kernel,
        grid=(n // tm,),
        in_specs=[pl.BlockSpec((tm, D_MODEL), lambda i: (i, 0)),
                  pl.BlockSpec((1, 6, D_MODEL), row),
                  pl.BlockSpec((1, D_MODEL), lambda i: (0, 0)),
                  pl.BlockSpec((D_MODEL, Z_ALL), lambda i: (0, 0))],
        out_specs=[pl.BlockSpec((tm, wd), lambda i: (i, 0)) for wd in widths],
        out_shape=[jax.ShapeDtypeStruct((n, wd), F32) for wd in widths],
        compiler_params=_cparams(("arbitrary",)),
        name="inproj",
    )(x, mods, g, w)


def _store_heads(ref, x, heads):
    for h in range(heads):
        ref[0, 0, h] = x[:, h * DH:(h + 1) * DH]


def _ctx_attn_kernel(zna_ref, zgq_ref, zm_ref, naq_g, nak_g, gqq_g, gqk_g, qa_g, kva_g, mqn_g, mkn_g,
                     wuq_ref, wuk_ref, wuv_ref, *refs):
    ona_ref, ogq_ref, omla_ref, nk_ref, nv_ref, gk_ref, gv_ref, ckv_ref, kr_ref = refs[-9:]
    for ref in refs[-6:]:
        for later in range(1, ref.shape[1]):
            ref[0, later] = jnp.zeros(ref.shape[2:], F32)
    zna = zna_ref[...]
    q = _seg_rms(zna[:, 0:MIX], DH, DH, naq_g[...])
    k = _seg_rms(zna[:, MIX:2 * MIX], DH, DH, nak_g[...])
    v = zna[:, 2 * MIX:3 * MIX]
    _store_heads(nk_ref, k, NA_HEADS)
    _store_heads(nv_ref, v, NA_HEADS)
    ona_ref[...] = _packed_attn(q, [(k.astype(BF16), v.astype(BF16))], NA_HEADS, DH ** -0.5)

    zgq = zgq_ref[...]
    gq = _seg_rms(zgq[:, 0:MIX], DH, DH, gqq_g[...])
    gk = _seg_rms(zgq[:, MIX:MIX + LANES], DH, DH, gqk_g[...])
    gv = zgq[:, MIX + LANES:2 * MIX]
    _store_heads(gk_ref, gk, GQ_KV)
    _store_heads(gv_ref, gv, GQ_KV)
    kv = [(gk.astype(BF16), gv.astype(BF16))]
    ogq_ref[...] = jnp.concatenate(
        [_packed_attn(gq[:, r * LANES:(r + 1) * LANES], kv, GQ_KV, DH ** -0.5) for r in range(2)], axis=-1)

    mq, mk, mv, ckv, col3 = _mla_qkv(zm_ref[...], qa_g[...], kva_g[...], mqn_g[...], mkn_g[...],
                                     wuq_ref[...], wuk_ref[...], wuv_ref[...])
    ckv_ref[0, 0] = ckv
    kr_ref[0, 0] = pltpu.roll(col3, LANES - MLA_NOPE, axis=1)[:, 0:MLA_ROPE]
    omla_ref[...] = _mla_attn(mq, [(mk.astype(BF16), mv.astype(BF16))], MLA_QK ** -0.5)


def _full(shape):
    nd = len(shape)
    return pl.BlockSpec(shape, lambda *a: (0,) * nd)


CACHE_SHAPES = ((NA_HEADS, SEQ, DH), (NA_HEADS, SEQ, DH), (GQ_KV, SEQ, DH), (GQ_KV, SEQ, DH),
                (SEQ, MLA_KVLORA), (SEQ, MLA_ROPE))


def _ctx_attn(zna, zgq, zm, lw, layer, caches):
    n = zna.shape[0]
    t = SEQ
    rows = lambda wd: pl.BlockSpec((t, wd), lambda b: (b, 0))
    if caches:
        cache_spec = lambda shp: pl.BlockSpec((1, 1) + shp, lambda b: (b, layer) + (0,) * len(shp))
    else:
        assert layer == 0
        cache_spec = lambda shp: pl.BlockSpec((1, DEPTH) + shp, lambda b: (b, 0) + (0,) * len(shp))
    gains = [lw["naq_g"], lw["nak_g"], lw["gqq_g"], lw["gqk_g"], lw["qa_g"], lw["kva_g"], lw["mqn_g"], lw["mkn_g"]]
    weights = [lw["wuq"], lw["wuk"], lw["wuv"]]
    n_in = 3 + len(gains) + len(weights)
    outs = pl.pallas_call(
        _ctx_attn_kernel,
        grid=(n // t,),
        in_specs=[rows(Z_NA), rows(Z_GQ), rows(Z_MLA)] + [_full(a.shape) for a in gains + weights]
                 + [pl.BlockSpec(memory_space=pl.ANY)] * len(caches),
        out_specs=[rows(MIX)] * 3 + [cache_spec(shp) for shp in CACHE_SHAPES],
        out_shape=[jax.ShapeDtypeStruct((n, MIX), F32)] * 3
                  + [jax.ShapeDtypeStruct((n // t, DEPTH) + shp, F32) for shp in CACHE_SHAPES],
        input_output_aliases={n_in + i: 3 + i for i in range(len(caches))},
        compiler_params=_cparams(("arbitrary",)),
        name="ctx_attn",
    )(zna, zgq, zm, *gains, *weights, *caches)
    return outs[:3], list(outs[3:])


NA_ROWS_PER_STEP = 4


def _na_window_start(r):
    return jnp.clip(r - NA_WIN_R // 2, 0, GRID_ROWS - NA_WIN_R)


def _pack_heads(ref):
    return jnp.concatenate([ref[0, 0, h] for h in range(ref.shape[2])], axis=-1)


def _na_lat_kernel(zna_ref, kc_ref, vc_ref, bias_ref, naq_g, nak_g, o_ref, k_scr, v_scr, kc_scr, vc_scr):
    r = pl.program_id(1)

    @pl.when(r == 0)
    def _():
        k_scr[...] = _seg_rms(zna_ref[:, MIX:2 * MIX], DH, DH, nak_g[...]).astype(BF16)
        v_scr[...] = zna_ref[:, 2 * MIX:3 * MIX].astype(BF16)
        kc_scr[...] = _pack_heads(kc_ref).astype(BF16)
        vc_scr[...] = _pack_heads(vc_ref).astype(BF16)

    span = NA_WIN_R * GRID_W
    for sub in range(NA_ROWS_PER_STEP):
        row = r * NA_ROWS_PER_STEP + sub
        q = _seg_rms(zna_ref[pl.ds(pl.multiple_of(row * GRID_W, GRID_W), GRID_W), 0:MIX], DH, DH, naq_g[...])
        start = pl.multiple_of(_na_window_start(row) * GRID_W, GRID_W)
        parts = [(k_scr[pl.ds(start, span), :], v_scr[pl.ds(start, span), :]),
                 (kc_scr[...], vc_scr[...])]
        d0 = _na_window_start(row) - row + NA_WIN_R - 1
        bias = jnp.concatenate([bias_ref[d0 + 2 * i] for i in range(NA_WIN_R // 2)], axis=-1)
        o_ref[sub * GRID_W:(sub + 1) * GRID_W, :] = _packed_attn(q, parts, NA_HEADS, DH ** -0.5,
                                                                 biases=[bias, None])


def _na_lat(zna, kc, vc, layer, bias, lw):
    nb = DEC_BATCH
    cache = pl.BlockSpec((1, 1, NA_HEADS, PAST_LEN, DH), lambda b, r: (b, layer, 0, 0, 0))
    return pl.pallas_call(
        _na_lat_kernel,
        grid=(nb, GRID_ROWS // NA_ROWS_PER_STEP),
        in_specs=[pl.BlockSpec((DEC_SEQ, Z_NA), lambda b, r: (b, 0)),
                  cache, cache,
                  _full(bias.shape),
                  _full(lw["naq_g"].shape), _full(lw["nak_g"].shape)],
        out_specs=pl.BlockSpec((NA_ROWS_PER_STEP * GRID_W, MIX),
                               lambda b, r: (b * (GRID_ROWS // NA_ROWS_PER_STEP) + r, 0)),
        out_shape=jax.ShapeDtypeStruct((nb * DEC_SEQ, MIX), F32),
        scratch_shapes=[pltpu.VMEM((DEC_SEQ, MIX), BF16), pltpu.VMEM((DEC_SEQ, MIX), BF16),
                        pltpu.VMEM((PAST_LEN, MIX), BF16), pltpu.VMEM((PAST_LEN, MIX), BF16)],
        compiler_params=_cparams(("arbitrary", "arbitrary")),
        name="na_latent",
    )(zna, kc, vc, bias, lw["naq_g"], lw["nak_g"])


LAT_QB = 256
LAT_KEYS = DEC_SEQ + PAST_LEN


def _gq_lat_kernel(zgq_ref, kc_ref, vc_ref, cos_ref, sa_ref, sb_ref, gqq_g, gqk_g, o_ref, k_scr, v_scr):
    j = pl.program_id(1)

    @pl.when(j == 0)
    def _():
        k = _seg_rms(zgq_ref[:, MIX:MIX + LANES], DH, DH, gqk_g[...])
        k = _rope(k, cos_ref[...], sa_ref[...], sb_ref[...], DH // 4)
        k_scr[0:DEC_SEQ, :] = k.astype(BF16)
        k_scr[DEC_SEQ:LAT_KEYS, :] = _pack_heads(kc_ref).astype(BF16)
        v_scr[0:DEC_SEQ, :] = zgq_ref[:, MIX + LANES:2 * MIX].astype(BF16)
        v_scr[DEC_SEQ:LAT_KEYS, :] = _pack_heads(vc_ref).astype(BF16)

    rows = pl.ds(pl.multiple_of(j * LAT_QB, LAT_QB), LAT_QB)
    q = _seg_rms(zgq_ref[rows, 0:MIX], DH, DH, gqq_g[...])
    q = _rope(q, cos_ref[rows, :], sa_ref[rows, :], sb_ref[rows, :], DH // 4)
    kv = [(k_scr[...], v_scr[...])]
    o_ref[...] = jnp.concatenate(
        [_packed_attn(q[:, r * LANES:(r + 1) * LANES], kv, GQ_KV, DH ** -0.5) for r in range(2)], axis=-1)


def _gq_lat(zgq, kc, vc, layer, tabs, lw):
    nb = DEC_BATCH
    nq = DEC_SEQ // LAT_QB
    cache = pl.BlockSpec((1, 1, GQ_KV, PAST_LEN, DH), lambda b, j: (b, layer, 0, 0, 0))
    return pl.pallas_call(
        _gq_lat_kernel,
        grid=(nb, nq),
        in_specs=[pl.BlockSpec((DEC_SEQ, Z_GQ), lambda b, j: (b, 0)), cache, cache]
                 + [_full(t.shape) for t in tabs] + [_full(lw["gqq_g"].shape), _full(lw["gqk_g"].shape)],
        out_specs=pl.BlockSpec((LAT_QB, MIX), lambda b, j: (b * nq + j, 0)),
        out_shape=jax.ShapeDtypeStruct((nb * DEC_SEQ, MIX), F32),
        scratch_shapes=[pltpu.VMEM((LAT_KEYS, LANES), BF16), pltpu.VMEM((LAT_KEYS, LANES), BF16)],
        compiler_params=_cparams(("arbitrary", "arbitrary")),
        name="gq_latent",
    )(zgq, kc, vc, *tabs, lw["gqq_g"], lw["gqk_g"])


def _mla_lat_kernel(zm_ref, ckvc_ref, krc_ref, cos_ref, sa_ref, sb_ref, qa_g, kva_g, mqn_g, mkn_g,
                    wuq_ref, wuk_ref, wuv_ref, o_ref, k_scr, v_scr):
    j = pl.program_id(1)
    half = MLA_ROPE // 4

    @pl.when(j == 0)
    def _():
        zm = zm_ref[...]
        ckv = _rms_rows(zm[:, 0:LANES], kva_g[...])
        col3 = zm[:, 2 * LANES:3 * LANES]
        lane = lax.broadcasted_iota(jnp.int32, col3.shape, 1)
        kr_placed = jnp.where((lane >= MLA_NOPE) & (lane < MLA_QK), col3, 0.0)
        k, v = _mla_kv(ckv, kr_placed, mkn_g[...], wuk_ref[...], wuv_ref[...])
        k = _rope(k, cos_ref[...], sa_ref[...], sb_ref[...], half)
        k_scr[0:DEC_SEQ, :] = k.astype(BF16)
        v_scr[0:DEC_SEQ, :] = v.astype(BF16)
        krc = jnp.concatenate([jnp.zeros((PAST_LEN, MLA_NOPE), F32), krc_ref[0, 0],
                               jnp.zeros((PAST_LEN, LANES - MLA_QK), F32)], axis=-1)
        kc, vc = _mla_kv(ckvc_ref[0, 0], krc, mkn_g[...], wuk_ref[...], wuv_ref[...])
        k_scr[DEC_SEQ:LAT_KEYS, :] = kc.astype(BF16)
        v_scr[DEC_SEQ:LAT_KEYS, :] = vc.astype(BF16)

    rows = pl.ds(pl.multiple_of(j * LAT_QB, LAT_QB), LAT_QB)
    cq = _rms_rows(zm_ref[rows, LANES:3 * LANES], qa_g[...], valid=MLA_QLORA)
    q = _seg_rms(_dot(cq.astype(BF16), wuq_ref[...]), LANES, MLA_QK, mqn_g[...])
    q = _rope(q, cos_ref[rows, :], sa_ref[rows, :], sb_ref[rows, :], half)
    o_ref[...] = _mla_attn(q, [(k_scr[...], v_scr[...])], MLA_QK ** -0.5)


def _mla_lat(zm, ckvc, krc, layer, tabs, lw):
    nb = DEC_BATCH
    nq = DEC_SEQ // LAT_QB
    small = [lw["qa_g"], lw["kva_g"], lw["mqn_g"], lw["mkn_g"], lw["wuq"], lw["wuk"], lw["wuv"]]
    return pl.pallas_call(
        _mla_lat_kernel,
        grid=(nb, nq),
        in_specs=[pl.BlockSpec((DEC_SEQ, Z_MLA), lambda b, j: (b, 0)),
                  pl.BlockSpec((1, 1, PAST_LEN, MLA_KVLORA), lambda b, j: (b, layer, 0, 0)),
                  pl.BlockSpec((1, 1, PAST_LEN, MLA_ROPE), lambda b, j: (b, layer, 0, 0))]
                 + [_full(t.shape) for t in tabs] + [_full(a.shape) for a in small],
        out_specs=pl.BlockSpec((LAT_QB, MIX), lambda b, j: (b * nq + j, 0)),
        out_shape=jax.ShapeDtypeStruct((nb * DEC_SEQ, MIX), F32),
        scratch_shapes=[pltpu.VMEM((LAT_KEYS, MLA_HEADS * LANES), BF16), pltpu.VMEM((LAT_KEYS, MIX), BF16)],
        compiler_params=_cparams(("arbitrary", "arbitrary")),
        name="mla_latent",
    )(zm, ckvc, krc, *tabs, *small)


def _s5_prep_kernel(bb_ref, qj_ref, m_ref):
    bb, qj = bb_ref[0], qj_ref[0]
    kt = _dot_nt_f32(bb[:, 0:LANES], qj[:, 0:LANES])
    kr = _dot_nt_f32(bb[:, LANES:2 * LANES], qj[:, LANES:2 * LANES])
    lane = lax.broadcasted_iota(jnp.int32, kt.shape, 1)
    blocks = []
    for s_pos in range(S5_CHUNK):
        sh_f = S5_GROUP * s_pos
        sh_b = S5_GROUP * (S5_CHUNK - 1 - s_pos)
        f = jnp.where(lane >= sh_f, pltpu.roll(kt, sh_f, axis=1), 0.0) if sh_f else kt
        b = jnp.where(lane < S5_ROW - sh_b, pltpu.roll(kr, S5_ROW - sh_b, axis=1), 0.0) if sh_b else kr
        blocks.append(f + b)
    m_ref[0] = jnp.concatenate(blocks, axis=0).astype(BF16)


def _s5_prep(lw):
    blk = lambda *shape: pl.BlockSpec((1,) + shape, lambda i: (i, 0, 0))
    return pl.pallas_call(
        _s5_prep_kernel,
        grid=(S5_GROUPS,),
        in_specs=[blk(S5_GROUP, S5_ROW), blk(S5_ROW, S5_ROW)],
        out_specs=blk(S5_ROW, S5_ROW),
        out_shape=jax.ShapeDtypeStruct((S5_GROUPS, S5_ROW, S5_ROW), BF16),
        compiler_params=_cparams(("arbitrary",)),
        name="s5_prep",
    )(lw["s5_bb"], lw["s5_qj"])


S5_TILE_BLOCKS = LANES // S5_GROUP


S5_INTERLEAVE = 2


def _block_transpose(xs, lane_blk):
    n = len(xs)
    d = n // 2
    while d >= 1:
        hi = (lane_blk & d) != 0
        nxt = list(xs)
        for i in range(n):
            if i & d:
                continue
            lo_arr, hi_arr = xs[i], xs[i + d]
            nxt[i] = jnp.where(hi, pltpu.roll(hi_arr, d * S5_GROUP, axis=1), lo_arr)
            nxt[i + d] = jnp.where(hi, hi_arr, pltpu.roll(lo_arr, LANES - d * S5_GROUP, axis=1))
        xs = nxt
        d //= 2
    return xs


def _s5_kernel(nb, nch, ua_ref, ub_ref, m_ref, ws_ref, wct_ref, a_ref, d_ref, h0_ref, ya_ref, yb_ref, fin_ref,
               ug_scr, yg_scr, *tmp_scr):
    rows = nb * nch
    per = S5_TILE_BLOCKS
    lane_blk = lax.broadcasted_iota(jnp.int32, (rows, LANES), 1) // S5_GROUP
    u_refs = (ua_ref, ub_ref)
    y_refs = (ya_ref, yb_ref)
    tile = lambda t: slice(t * LANES, (t + 1) * LANES)

    for half in range(S5_GROUPS // per):
        for t in range(S5_CHUNK // per):
            pieces = [u_refs[half][pl.ds(t * per + i, rows, stride=S5_CHUNK), :] for i in range(per)]
            for j, arr in enumerate(_block_transpose(pieces, lane_blk)):
                ug_scr[half * per + j, :, tile(t)] = arr

    def group_set(i, carry):
        gs = [i * S5_INTERLEAVE + n for n in range(S5_INTERLEAVE)]
        sets = [tmp_scr[6 * n:6 * n + 6] for n in range(S5_INTERLEAVE)]
        state = []
        for g, (sf_scr, sb_scr, sfs_scr, sbs_scr, _, _) in zip(gs, sets):
            s = _dot(ug_scr[g].astype(BF16), ws_ref[g])
            sf_scr[...] = s[:, 0:LANES]
            sb_scr[...] = s[:, LANES:2 * LANES]
            sfs_scr[...] = pltpu.roll(s[:, 0:LANES], S5_N, axis=1)
            sbs_scr[...] = pltpu.roll(s[:, LANES:2 * LANES], S5_N, axis=1)
            h0 = h0_ref[g]
            h_f = h0[:, 0:LANES]
            h_b = h0[:, LANES:2 * LANES]
            state.append([h_f, pltpu.roll(h_f, S5_N, axis=1), h_b, pltpu.roll(h_b, S5_N, axis=1)])
        for k in range(nch):
            rows_f = pl.ds(k, nb, stride=nch)
            rows_b = pl.ds(nch - 1 - k, nb, stride=nch)
            for n, (g, (sf_scr, sb_scr, sfs_scr, sbs_scr, hf_scr, hb_scr)) in enumerate(zip(gs, sets)):
                a = a_ref[g]
                h_f, g_f, h_b, g_b = state[n]
                hf_scr[rows_f, :] = h_f
                hb_scr[rows_b, :] = h_b
                state[n] = [a[0:1] * h_f + a[1:2] * g_f + sf_scr[rows_f, :],
                            a[0:1] * g_f + a[2:3] * h_f + sfs_scr[rows_f, :],
                            a[3:4] * h_b + a[4:5] * g_b + sb_scr[rows_b, :],
                            a[3:4] * g_b + a[5:6] * h_b + sbs_scr[rows_b, :]]
        for n, (g, (_, _, _, _, hf_scr, hb_scr)) in enumerate(zip(gs, sets)):
            fin_ref[g] = jnp.concatenate([state[n][0], state[n][2]], axis=-1)
            hp = jnp.concatenate([hf_scr[...], hb_scr[...]], axis=-1).astype(BF16)
            x = ug_scr[g]
            yg_scr[g] = _dot(x.astype(BF16), m_ref[g]) + _dot_nt(hp, wct_ref[g]) + d_ref[g] * x
        return carry

    lax.fori_loop(0, S5_GROUPS // S5_INTERLEAVE, group_set, 0)

    for half in range(S5_GROUPS // per):
        for t in range(S5_CHUNK // per):
            pieces = [yg_scr[half * per + j, :, tile(t)] for j in range(per)]
            for i, arr in enumerate(_block_transpose(pieces, lane_blk)):
                y_refs[half][pl.ds(t * per + i, rows, stride=S5_CHUNK), :] = arr


def _s5_mixer(u, m, lw, h0, nb, nb_step):
    n = u.shape[0]
    nch = n // nb // S5_CHUNK
    tok = nb_step * nch * S5_CHUNK
    rows = nb_step * nch
    g = S5_GROUPS
    res = lambda a: pl.BlockSpec(a.shape, lambda i: (0, 0, 0))
    state = pl.BlockSpec((g, nb_step, S5_ROW), lambda i: (0, i, 0))
    half = lambda j: pl.BlockSpec((tok, LANES), lambda i: (i, j))
    ya, yb, fin = pl.pallas_call(
        functools.partial(_s5_kernel, nb_step, nch),
        grid=(nb // nb_step,),
        in_specs=[half(0), half(1), res(m), res(lw["s5_ws"]), res(lw["s5_wct"]), res(lw["s5_a"]), res(lw["s5_d"]), state],
        out_specs=[half(0), half(0), state],
        out_shape=[jax.ShapeDtypeStruct((n, LANES), F32), jax.ShapeDtypeStruct((n, LANES), F32),
                   jax.ShapeDtypeStruct((g, nb, S5_ROW), F32)],
        scratch_shapes=[pltpu.VMEM((g, rows, S5_ROW), F32), pltpu.VMEM((g, rows, S5_ROW), F32)]
                       + [pltpu.VMEM((rows, LANES), F32)] * (6 * S5_INTERLEAVE),
        compiler_params=_cparams(("arbitrary",)),
        name="s5_mixer",
    )(u, u, m, lw["s5_ws"], lw["s5_wct"], lw["s5_a"], lw["s5_d"], h0)
    return ya, yb, fin


def _outproj_kernel(ona_ref, ys5a_ref, ys5b_ref, ogq_ref, omla_ref, x_ref, mod_ref, w_ref, wglu_ref, bglu_ref,
                    o_ref):
    y = jax.nn.gelu(jnp.concatenate([ys5a_ref[...], ys5b_ref[...]], axis=-1), approximate=True)
    y = y * jax.nn.sigmoid(_dot(y.astype(BF16), wglu_ref[...]) + bglu_ref[...])
    mixed = _dot(ona_ref[...].astype(BF16), w_ref[0:MIX, :])
    mixed = mixed + _dot(y.astype(BF16), w_ref[MIX:2 * MIX, :])
    mixed = mixed + _dot(ogq_ref[...].astype(BF16), w_ref[2 * MIX:3 * MIX, :])
    mixed = mixed + _dot(omla_ref[...].astype(BF16), w_ref[3 * MIX:4 * MIX, :])
    o_ref[...] = x_ref[...] + mod_ref[0, 2:3, :] * mixed


def _outproj(ona, ys5a, ys5b, ogq, omla, x, mods, lw, mod_base, rows_per_mod):
    n = x.shape[0]
    tm = 512
    row = lambda i: (mod_base + (i * tm) // rows_per_mod, 0, 0)
    part = pl.BlockSpec((tm, MIX), lambda i: (i, 0))
    half = pl.BlockSpec((tm, LANES), lambda i: (i, 0))
    return pl.pallas_call(
        _outproj_kernel,
        grid=(n // tm,),
        in_specs=[part, half, half, part, part,
                  pl.BlockSpec((tm, D_MODEL), lambda i: (i, 0)),
                  pl.BlockSpec((1, 6, D_MODEL), row),
                  _full(lw["w_out"].shape), _full(lw["w_glu"].shape), _full(lw["b_glu"].shape)],
        out_specs=pl.BlockSpec((tm, D_MODEL), lambda i: (i, 0)),
        out_shape=jax.ShapeDtypeStruct((n, D_MODEL), F32),
        compiler_params=_cparams(("arbitrary",)),
        name="outproj",
    )(ona, ys5a, ys5b, ogq, omla, x, mods, lw["w_out"], lw["w_glu"], lw["b_glu"])


FF_ROWS = 1024
FFN_VMEM_LIMIT = 58 * 1024 * 1024


def _ffn_kernel(seq, x_ref, mod_ref, g_ref, wup_ref, cw_ref, cb_ref, wd_ref, o_ref, h_scr, ug_scr, uu_scr, act_scr):
    x = x_ref[...]
    y = x * lax.rsqrt(jnp.mean(x * x, axis=-1, keepdims=True) + EPS) * g_ref[...]
    h_scr[...] = (y * (1.0 + mod_ref[0, 4:5, :]) + mod_ref[0, 3:4, :]).astype(BF16)

    pos = lax.broadcasted_iota(jnp.int32, (FF_ROWS, FF_CHUNK), 0) % seq
    first = pos == 0
    last = pos == seq - 1

    def conv(u, cw, cb):
        prev = jnp.where(first, 0.0, pltpu.roll(u, 1, axis=0))
        nxt = jnp.where(last, 0.0, pltpu.roll(u, FF_ROWS - 1, axis=0))
        return cw[0:1, :] * prev + cw[1:2, :] * u + cw[2:3, :] * nxt + cb

    def cols(j, base=0):
        return pl.ds(pl.multiple_of(base + j * FF_CHUNK, FF_CHUNK), FF_CHUNK)

    def up_proj(j, slot):
        h = h_scr[...]
        ug_scr[slot] = _dot(h, wup_ref[:, cols(j)])
        uu_scr[slot] = _dot(h, wup_ref[:, cols(j, D_FF)])

    def gate_act(j, slot):
        gate = conv(ug_scr[slot], cw_ref[:, cols(j)], cb_ref[:, cols(j)])
        up = conv(uu_scr[slot], cw_ref[:, cols(j, D_FF)], cb_ref[:, cols(j, D_FF)])
        act_scr[slot] = (gate * jax.nn.sigmoid(gate) * up).astype(BF16)

    def down_proj(j, slot):
        return _dot(act_scr[slot], wd_ref[cols(j), :])

    last_j = FF_STEPS - 1
    up_proj(0, 0)
    up_proj(1, 1)
    gate_act(0, 0)
    up_proj(2, 0)
    gate_act(1, 1)
    o_ref[...] = down_proj(0, 0)

    def body(j, carry):
        slot = j % 2
        o_ref[...] += down_proj(j, slot)
        gate_act(j + 1, 1 - slot)
        up_proj(j + 2, slot)
        return carry

    lax.fori_loop(1, last_j - 1, body, 0)
    gate_act(last_j, last_j % 2)
    o_ref[...] += down_proj(last_j - 1, (last_j - 1) % 2)
    o_ref[...] = x_ref[...] + mod_ref[0, 5:6, :] * (o_ref[...] + down_proj(last_j, last_j % 2))


def _ffn(x, mods, lw, fw, layer, mod_base, rows_per_mod, seq):
    n = x.shape[0]
    row = lambda i: (mod_base + (i * FF_ROWS) // rows_per_mod, 0, 0)
    resident = lambda a: pl.BlockSpec((None,) + a.shape[1:], lambda i: (layer,) + (0,) * (a.ndim - 1),
                                      pipeline_mode=pl.Buffered(1))
    return pl.pallas_call(
        functools.partial(_ffn_kernel, seq),
        grid=(n // FF_ROWS,),
        in_specs=[pl.BlockSpec((FF_ROWS, D_MODEL), lambda i: (i, 0)),
                  pl.BlockSpec((1, 6, D_MODEL), row),
                  pl.BlockSpec((1, D_MODEL), lambda i: (0, 0)),
                  resident(fw["w_up"]), resident(fw["conv_w"]), resident(fw["conv_b"]), resident(fw["w_down"])],
        out_specs=pl.BlockSpec((FF_ROWS, D_MODEL), lambda i: (i, 0)),
        out_shape=jax.ShapeDtypeStruct((n, D_MODEL), F32),
        scratch_shapes=[pltpu.VMEM((FF_ROWS, D_MODEL), BF16),
                        pltpu.VMEM((2, FF_ROWS, FF_CHUNK), F32), pltpu.VMEM((2, FF_ROWS, FF_CHUNK), F32),
                        pltpu.VMEM((2, FF_ROWS, FF_CHUNK), BF16)],
        compiler_params=pltpu.CompilerParams(dimension_semantics=("arbitrary",), vmem_limit_bytes=FFN_VMEM_LIMIT),
        name="conv_ffn",
    )(x, mods, lw["norm2_g"], fw["w_up"], fw["conv_w"], fw["conv_b"], fw["w_down"])


def _rope_tables():
    pos = np.arange(DEC_SEQ)

    def ang(p, half):
        inv = ROPE_BASE ** (-np.arange(half, dtype=np.float64) / half)
        a = p.astype(np.float64)[:, None] * inv[None, :]
        return np.concatenate([a, a], axis=-1)

    def tables(dim, lanes_before, lanes_after, reps):
        a = np.concatenate([ang(pos // GRID_W, dim // 4), ang(pos % GRID_W, dim // 4)], axis=-1)
        half = dim // 4
        first = (np.arange(dim) % (2 * half)) < half
        cos = np.cos(a)
        sin_a = np.where(first[None, :], -np.sin(a), 0.0)
        sin_b = np.where(first[None, :], 0.0, np.sin(a))

        def place(t, fill):
            t = np.concatenate([np.full((DEC_SEQ, lanes_before), fill), t,
                                np.full((DEC_SEQ, lanes_after), fill)], axis=-1)
            return jnp.asarray(np.tile(t, (1, reps)), F32)

        return place(cos, 1.0), place(sin_a, 0.0), place(sin_b, 0.0)

    gq = tables(DH, 0, 0, LANES // DH)
    mla = tables(MLA_ROPE, MLA_NOPE, LANES - MLA_QK, 1)
    return gq, mla


def _na_bias_tables(rpb):
    qc = np.arange(GRID_W)[:, None]
    kc = np.arange(GRID_W)[None, :]
    col_start = np.clip(qc - NA_WIN_C // 2, 0, GRID_W - NA_WIN_C)
    valid = (kc >= col_start) & (kc < col_start + NA_WIN_C)
    d_c = np.clip(kc - qc, 1 - NA_WIN_C, NA_WIN_C - 1) + NA_WIN_C - 1
    nrel = 2 * NA_WIN_C - 1
    onehot = jnp.asarray((d_c.reshape(-1)[None, :] == np.arange(nrel)[:, None]).astype(np.float32))
    t = jnp.dot(rpb.astype(F32).reshape(-1, nrel), onehot, precision=lax.Precision.HIGHEST)
    t = t.reshape(NA_HEADS, 2 * NA_WIN_R - 1, GRID_W, GRID_W)
    t = jnp.where(jnp.asarray(valid)[None, None], t, NEG)
    t = t.transpose(1, 0, 2, 3).reshape(2 * NA_WIN_R - 1, NA_HEADS * GRID_W, GRID_W)
    return jnp.concatenate([t[:-1], t[1:]], axis=-1)


def _s5_matrices(lam_re, lam_im, log_dt, b_re, b_im, c_re, c_im):
    c = S5_CHUNK
    cat = lambda parts: jnp.concatenate(parts, axis=-1)
    pw, bre, bim, cre, cim = [], [], [], [], []
    for d in range(2):
        dt = jnp.exp(log_dt[d].astype(F32))[:, None]
        lr, li = lam_re[d].astype(F32), lam_im[d].astype(F32)
        mag = jnp.exp(lr * dt)
        a_re, a_im = mag * jnp.cos(li * dt), mag * jnp.sin(li * dt)
        den = lr * lr + li * li
        f_re = ((a_re - 1.0) * lr + a_im * li) / den
        f_im = (a_im * lr - (a_re - 1.0) * li) / den
        br, bi = b_re[d].astype(F32), b_im[d].astype(F32)
        bre.append((f_re[..., None] * br - f_im[..., None] * bi).transpose(0, 2, 1))
        bim.append((f_re[..., None] * bi + f_im[..., None] * br).transpose(0, 2, 1))
        cre.append(c_re[d].astype(F32))
        cim.append(c_im[d].astype(F32))
        p_re, p_im = [jnp.ones_like(a_re)], [jnp.zeros_like(a_re)]
        for _ in range(c):
            p_re, p_im = (p_re + [p_re[-1] * a_re - p_im[-1] * a_im], p_im + [p_re[-1] * a_im + p_im[-1] * a_re])
        p_re, p_im = jnp.stack(p_re, axis=1), jnp.stack(p_im, axis=1)
        pw.append((p_re, p_im))
    (pfr, pfi), (pbr, pbi) = pw
    up = lambda x, off=0: x[:, off:off + c]
    dn = lambda x, off=0: x[:, off:off + c][:, ::-1]
    outer = lambda x, y: (x[:, :, None, :] * y[:, None, :, :])
    e_fr, e_fi, e_br, e_bi = dn(pfr), dn(pfi), up(pbr), up(pbi)
    bb = cat([bre[0], bim[0], bre[1], bim[1]])
    ws = (outer(cat([e_fr, e_fr, e_br, e_br]), bb)
          + outer(cat([-e_fi, e_fi, -e_bi, e_bi]), cat([bim[0], bre[0], bim[1], bre[1]])))
    e_fr, e_fi, e_br, e_bi = up(pfr, 1), up(pfi, 1), dn(pbr, 1), dn(pbi, 1)
    c_a, c_b = cat([cre[0], cre[0], cre[1], cre[1]]), cat([cim[0], cim[0], cim[1], cim[1]])
    wct = outer(cat([e_fr, -e_fi, e_br, -e_bi]), c_a) + outer(cat([-e_fi, -e_fr, -e_bi, -e_br]), c_b)
    e_fr, e_fi, e_br, e_bi = up(pfr), up(pfi), dn(pbr), dn(pbi)
    qj = outer(cat([e_fr, -e_fi, e_br, -e_bi]), c_a) + outer(cat([-e_fi, -e_fr, -e_bi, -e_br]), c_b)
    rows = lambda w: w.reshape(S5_GROUPS, S5_ROW, 4 * S5_N).astype(BF16)
    av = []
    for p_re, p_im in ((pfr[:, c], pfi[:, c]), (pbr[:, c], pbi[:, c])):
        av += [cat([p_re, p_re]), cat([-p_im, p_im]), cat([p_im, -p_im])]
    return rows(ws), rows(wct), bb, qj.reshape(S5_GROUPS, S5_ROW, 4 * S5_N), jnp.stack(av, axis=1)


def _layer_weights(l, p):
    w = p["w_in"][l]
    q0 = Z_NA + Z_S5
    gq = lambda h: w[:, q0 + h * DH:q0 + (h + 1) * DH]
    m0 = q0 + 2 * MIX
    w_in = jnp.concatenate(
        [w[:, 0:q0], gq(0), gq(2), gq(1), gq(3), w[:, q0 + MIX:m0],
         w[:, m0 + MLA_QLORA:m0 + MLA_QLORA + MLA_KVLORA], w[:, m0:m0 + MLA_QLORA],
         w[:, m0 + MLA_QLORA + MLA_KVLORA:], jnp.zeros((D_MODEL, Z_ALL - w.shape[1]), w.dtype)],
        axis=1).astype(BF16)
    wo = p["w_out"][l]
    og = lambda h: wo[2 * MIX + h * DH:2 * MIX + (h + 1) * DH]
    w_out = jnp.concatenate([wo[0:2 * MIX], og(0), og(2), og(1), og(3), wo[3 * MIX:]], axis=0).astype(BF16)
    pad_head = lambda g: jnp.tile(jnp.pad(g, (0, LANES - MLA_QK)), MLA_HEADS)[None, :]
    wuq = jnp.pad(p["mla_w_uq"][l].reshape(MLA_QLORA, MLA_HEADS, MLA_QK),
                  ((0, 2 * LANES - MLA_QLORA), (0, 0), (0, LANES - MLA_QK))).reshape(2 * LANES, MLA_HEADS * LANES)
    wukv = p["mla_w_ukv"][l].reshape(MLA_KVLORA, MLA_HEADS, MLA_NOPE + DH)
    wuk = jnp.pad(wukv[:, :, :MLA_NOPE], ((0, 0), (0, 0), (0, LANES - MLA_NOPE))).reshape(MLA_KVLORA, MLA_HEADS * LANES)
    wuv = wukv[:, :, MLA_NOPE:].reshape(MLA_KVLORA, MIX)
    s5_ws, s5_wct, s5_bb, s5_qj, s5_a = _s5_matrices(p["s5_lam_re"][l], p["s5_lam_im"][l], p["s5_log_dt"][l],
                                                     p["s5_b_re"][l], p["s5_b_im"][l], p["s5_c_re"][l], p["s5_c_im"][l])
    return dict(
        norm1_g=p["norm1_g"][l][None, :], norm2_g=p["norm2_g"][l][None, :],
        w_in=w_in, w_out=w_out,
        naq_g=jnp.tile(p["na_qn"][l], NA_HEADS)[None, :], nak_g=jnp.tile(p["na_kn"][l], NA_HEADS)[None, :],
        gqq_g=jnp.tile(p["gq_qn"][l], GQ_HEADS)[None, :], gqk_g=jnp.tile(p["gq_kn"][l], GQ_KV)[None, :],
        qa_g=jnp.pad(p["mla_qa_g"][l], (0, 2 * LANES - MLA_QLORA))[None, :], kva_g=p["mla_kva_g"][l][None, :],
        mqn_g=pad_head(p["mla_qn"][l]), mkn_g=pad_head(p["mla_kn"][l]),
        wuq=wuq.astype(BF16), wuk=wuk.astype(BF16), wuv=wuv.astype(BF16),
        na_bias=_na_bias_tables(p["na_rpb"][l]),
        s5_ws=s5_ws, s5_wct=s5_wct, s5_bb=s5_bb, s5_qj=s5_qj, s5_a=s5_a,
        s5_d=jnp.tile(p["s5_d"][l].reshape(S5_GROUPS, 1, S5_GROUP), (1, 1, S5_CHUNK)),
        w_glu=p["s5_w_glu"][l].astype(BF16), b_glu=p["s5_b_glu"][l][None, :],
    )


def kernel(x_prompt, x_sample, cache_na_k, cache_na_v, state_s5, cache_gqa_k, cache_gqa_v, cache_mla_ckv,
           cache_mla_krope, c, c_ctx, norm1_g, norm2_g, ada_w, ada_b, w_in, na_qn, na_kn, na_rpb, s5_lam_re,
           s5_lam_im, s5_log_dt, s5_b_re, s5_b_im, s5_c_re, s5_c_im, s5_d, s5_w_glu, s5_b_glu, gq_qn, gq_kn,
           mla_qa_g, mla_kva_g, mla_w_uq, mla_w_ukv, mla_qn, mla_kn, w_out, ffn_w_up, ffn_conv_w, ffn_conv_b,
           ffn_w_down):
    p = dict(norm1_g=norm1_g, norm2_g=norm2_g, w_in=w_in, na_qn=na_qn, na_kn=na_kn, na_rpb=na_rpb,
             s5_lam_re=s5_lam_re, s5_lam_im=s5_lam_im, s5_log_dt=s5_log_dt, s5_b_re=s5_b_re, s5_b_im=s5_b_im,
             s5_c_re=s5_c_re, s5_c_im=s5_c_im, s5_d=s5_d, s5_w_glu=s5_w_glu, s5_b_glu=s5_b_glu,
             gq_qn=gq_qn, gq_kn=gq_kn, mla_qa_g=mla_qa_g, mla_kva_g=mla_kva_g, mla_w_uq=mla_w_uq,
             mla_w_ukv=mla_w_ukv, mla_qn=mla_qn, mla_kn=mla_kn, w_out=w_out, ffn_w_up=ffn_w_up,
             ffn_conv_w=ffn_conv_w, ffn_conv_b=ffn_conv_b, ffn_w_down=ffn_w_down)
    nb, nd = BATCH, DEC_BATCH
    cvec = jnp.concatenate([c_ctx[None, :], c, jnp.zeros((8 - 1 - nd, D_MODEL), F32)], axis=0)
    mods_all = _ada_mods(cvec, ada_w, ada_b).reshape(DEPTH, 8, 6, D_MODEL)
    gq_tabs, mla_tabs = _rope_tables()
    fw = dict(w_up=ffn_w_up.astype(BF16), conv_w=ffn_conv_w, conv_b=ffn_conv_b[:, None, :],
              w_down=ffn_w_down.astype(BF16))

    xc = x_prompt.reshape(nb * SEQ, D_MODEL)
    xl = x_sample.reshape(nd * DEC_SEQ, D_MODEL)
    zero_state = jnp.zeros((S5_GROUPS, nb, 4 * S5_N), F32)
    caches = []
    s5_states = []
    for l in range(DEPTH):
        lw = _layer_weights(l, p)
        mods = mods_all[l]
        s5_m = _s5_prep(lw)

        zna, zs5, zgq, zm = _inproj(xc, mods, lw["norm1_g"], lw["w_in"], 0, nb * SEQ)
        (ona, ogq, omla), caches = _ctx_attn(zna, zgq, zm, lw, l, caches)
        ys5a, ys5b, fin = _s5_mixer(zs5, s5_m, lw, zero_state, nb, S5_CTX_BLOCK)
        xc = _outproj(ona, ys5a, ys5b, ogq, omla, xc, mods, lw, 0, nb * SEQ)
        xc = _ffn(xc, mods, lw, fw, l, 0, nb * SEQ, SEQ)
        s5_states.append(fin.reshape(S5_GROUPS, nb, 2, 2, S5_N).transpose(1, 2, 3, 0, 4))

        zna, zs5, zgq, zm = _inproj(xl, mods, lw["norm1_g"], lw["w_in"], 1, DEC_SEQ)
        ona = _na_lat(zna, cache_na_k, cache_na_v, l, lw["na_bias"], lw)
        ogq = _gq_lat(zgq, cache_gqa_k, cache_gqa_v, l, gq_tabs, lw)
        omla = _mla_lat(zm, cache_mla_ckv, cache_mla_krope, l, mla_tabs, lw)
        h0 = state_s5[:, l].astype(F32).transpose(3, 0, 1, 2, 4).reshape(S5_GROUPS, nd, 4 * S5_N)
        ys5a, ys5b, _ = _s5_mixer(zs5, s5_m, lw, h0, nd, nd)
        xl = _outproj(ona, ys5a, ys5b, ogq, omla, xl, mods, lw, 1, DEC_SEQ)
        xl = _ffn(xl, mods, lw, fw, l, 1, DEC_SEQ, DEC_SEQ)

    return (xc.reshape(nb, SEQ, D_MODEL), xl.reshape(nd, DEC_SEQ, D_MODEL), caches[0], caches[1],
            jnp.stack(s5_states, axis=1), caches[2], caches[3], caches[4], caches[5])
```

```python
import functools

import numpy as np
import jax
import jax.numpy as jnp
from jax import lax
from jax.experimental import pallas as pl
from jax.experimental.pallas import tpu as pltpu

F32 = jnp.float32
BF16 = jnp.bfloat16

D_MODEL = 1024
BATCH = 32
SEQ = 256
DEPTH = 2
DEC_BATCH = 4
DEC_SEQ = 1024
PAST_LEN = 512
GRID_W = 64
GRID_ROWS = DEC_SEQ // GRID_W
MIX = D_MODEL // 4
DH = 64
NA_HEADS = MIX // DH
NA_WIN_R = 8
NA_WIN_C = 16
S5_GROUP = 16
S5_GROUPS = MIX // S5_GROUP
S5_N = 64
GQ_HEADS = MIX // DH
GQ_KV = GQ_HEADS // 2
MLA_HEADS = MIX // DH
MLA_NOPE = 64
MLA_ROPE = 32
MLA_QK = MLA_NOPE + MLA_ROPE
MLA_QLORA = (3 * D_MODEL) // 16
MLA_KVLORA = D_MODEL // 8
D_FF = 128 * ((8 * D_MODEL // 3 + 127) // 128)
ROPE_BASE = 10000.0
EPS = 1e-6
NEG = -1e30

LANES = 128
MXU_DIM = 256
VMEM_LIMIT = 48 * 1024 * 1024
MXU_DEN_MIN_KEYS = 1024

Z_NA = 3 * MIX
Z_S5 = MIX
Z_GQ = 2 * MIX
Z_MLA = 3 * LANES
Z_ALL = Z_NA + Z_S5 + Z_GQ + Z_MLA

S5_CHUNK = 16
S5_ROW = S5_CHUNK * S5_GROUP
S5_CTX_BLOCK = 16

PROJ_ROWS = 1024
FF_CHUNK = 256
FF_STEPS = D_FF // FF_CHUNK


def _cparams(sem):
    return pltpu.CompilerParams(dimension_semantics=sem, vmem_limit_bytes=VMEM_LIMIT)


def _dot(a, b):
    return jnp.dot(a, b, preferred_element_type=F32)


def _dot_nt(a, b):
    return lax.dot_general(a, b, (((1,), (1,)), ((), ())), preferred_element_type=F32)


def _dot_nt_f32(a, b):
    def split(x):
        hi = x.astype(BF16)
        r1 = x - hi.astype(F32)
        mid = r1.astype(BF16)
        return hi, mid, (r1 - mid.astype(F32)).astype(BF16)

    pa, pb = split(a), split(b)
    acc = None
    for i in range(3):
        for j in range(3 - i):
            term = _dot_nt(pa[i], pb[j])
            acc = term if acc is None else acc + term
    return acc


def _rms_rows(x, gain, denom=None, valid=None):
    xx = x * x
    if valid is not None and valid != x.shape[-1]:
        lane = lax.broadcasted_iota(jnp.int32, x.shape, 1)
        xx = jnp.where(lane < valid, xx, 0.0)
    denom = denom or (valid or x.shape[-1])
    ss = jnp.sum(xx, axis=-1, keepdims=True)
    return x * lax.rsqrt(ss / denom + EPS) * gain


def _seg_rms(x, seg, denom, gain):
    rows, width = x.shape
    if seg % LANES == 0:
        parts = []
        for s in range(width // seg):
            xs = x[:, s * seg:(s + 1) * seg]
            ss = jnp.sum(xs * xs, axis=-1, keepdims=True)
            parts.append(xs * lax.rsqrt(ss / denom + EPS))
        return jnp.concatenate(parts, axis=-1) * gain
    same_seg = (lax.broadcasted_iota(jnp.int32, (width, width), 0) // seg
                == lax.broadcasted_iota(jnp.int32, (width, width), 1) // seg)
    ones = jnp.where(same_seg, 1.0, 0.0).astype(BF16)
    xx = x * x
    hi = xx.astype(BF16)
    lo = (xx - hi.astype(F32)).astype(BF16)
    ss = _dot(hi, ones) + _dot(lo, ones)
    return x * lax.rsqrt(ss / denom + EPS) * gain


def _rope(x, cos, sin_a, sin_b, half):
    tiles = []
    for t in range(x.shape[-1] // LANES):
        xt = x[:, t * LANES:(t + 1) * LANES]
        up = pltpu.roll(xt, LANES - half, axis=1)
        dn = pltpu.roll(xt, half, axis=1)
        tiles.append(xt * cos + up * sin_a + dn * sin_b)
    return tiles[0] if len(tiles) == 1 else jnp.concatenate(tiles, axis=-1)


def _packed_attn(q, parts, nseg, scale, biases=None):
    tq, width = q.shape
    seg_id = lax.broadcasted_iota(jnp.int32, q.shape, 1) // DH
    lhs = jnp.concatenate([jnp.where(seg_id == h, q, 0.0) for h in range(nseg)], axis=0).astype(BF16)
    scores = []
    for i, (kb, _) in enumerate(parts):
        s = _dot_nt(lhs, kb) * scale
        if biases is not None and biases[i] is not None:
            s = s + biases[i]
        scores.append(s)
    m = scores[0].max(axis=-1, keepdims=True)
    for s in scores[1:]:
        m = jnp.maximum(m, s.max(axis=-1, keepdims=True))
    mxu_den = sum(kb.shape[0] for kb, _ in parts) >= MXU_DEN_MIN_KEYS
    den = None
    pv = None
    for s, (_, vb) in zip(scores, parts):
        p = jnp.exp(s - m)
        pb = p.astype(BF16)
        ps = _dot(pb, jnp.ones(vb.shape, BF16)) if mxu_den else p.sum(axis=-1, keepdims=True)
        den = ps if den is None else den + ps
        c = _dot(pb, vb)
        pv = c if pv is None else pv + c
    pv = pv / den
    out = jnp.zeros((tq, width), F32)
    for h in range(nseg):
        out = out + jnp.where(seg_id == h, pv[h * tq:(h + 1) * tq], 0.0)
    return out


def _mla_attn(q, parts, scale):
    tq = q.shape[0]
    vseg = None
    out = jnp.zeros((tq, MIX), F32)
    for h in range(MLA_HEADS):
        qh = q[:, h * LANES:(h + 1) * LANES].astype(BF16)
        scores = [_dot_nt(qh, kb[:, h * LANES:(h + 1) * LANES]) * scale for kb, _ in parts]
        m = scores[0].max(axis=-1, keepdims=True)
        for s in scores[1:]:
            m = jnp.maximum(m, s.max(axis=-1, keepdims=True))
        den = None
        pv = None
        for s, (_, vb) in zip(scores, parts):
            p = jnp.exp(s - m)
            ps = p.sum(axis=-1, keepdims=True)
            den = ps if den is None else den + ps
            vseg = lax.broadcasted_iota(jnp.int32, vb.shape, 1) // DH
            c = _dot(p.astype(BF16), jnp.where(vseg == h, vb, jnp.zeros_like(vb)))
            pv = c if pv is None else pv + c
        out = out + pv / den
    return out


def _mla_qkv(zm, qa_g, kva_g, mqn_g, mkn_g, wuq, wuk, wuv):
    ckv = _rms_rows(zm[:, 0:LANES], kva_g)
    col3 = zm[:, 2 * LANES:3 * LANES]
    lane = lax.broadcasted_iota(jnp.int32, col3.shape, 1)
    kr_placed = jnp.where((lane >= MLA_NOPE) & (lane < MLA_QK), col3, 0.0)
    cq = _rms_rows(zm[:, LANES:3 * LANES], qa_g, valid=MLA_QLORA)
    q = _seg_rms(_dot(cq.astype(BF16), wuq), LANES, MLA_QK, mqn_g)
    k, v = _mla_kv(ckv, kr_placed, mkn_g, wuk, wuv)
    return q, k, v, ckv, col3


def _mla_kv(ckv, kr_placed, mkn_g, wuk, wuv):
    cb = ckv.astype(BF16)
    kf = _dot(cb, wuk) + jnp.concatenate([kr_placed] * MLA_HEADS, axis=-1)
    return _seg_rms(kf, LANES, MLA_QK, mkn_g), _dot(cb, wuv)


def _ada_kernel(c_ref, w_ref, b_ref, o_ref):
    cv = c_ref[...]
    s = cv * jax.nn.sigmoid(cv)
    o_ref[0] = _dot(s.astype(BF16), w_ref[0].astype(BF16)) + b_ref[0]


def _ada_mods(cvec, ada_w, ada_b):
    tn = 1536
    n = ada_w.shape[-1]
    return pl.pallas_call(
        _ada_kernel,
        grid=(DEPTH, n // tn),
        in_specs=[pl.BlockSpec((8, D_MODEL), lambda l, j: (0, 0)),
                  pl.BlockSpec((1, D_MODEL, tn), lambda l, j: (l, 0, j)),
                  pl.BlockSpec((1, 1, tn), lambda l, j: (l, 0, j))],
        out_specs=pl.BlockSpec((1, 8, tn), lambda l, j: (l, 0, j)),
        out_shape=jax.ShapeDtypeStruct((DEPTH, 8, n), F32),
        compiler_params=_cparams(("arbitrary", "arbitrary")),
        name="ada_mods",
    )(cvec, ada_w, ada_b.reshape(DEPTH, 1, n))


def _inproj_kernel(x_ref, mod_ref, g_ref, w_ref, ona_ref, os5_ref, ogq_ref, omla_ref):
    x = x_ref[...]
    y = x * lax.rsqrt(jnp.mean(x * x, axis=-1, keepdims=True) + EPS) * g_ref[...]
    h = y * (1.0 + mod_ref[0, 1:2, :]) + mod_ref[0, 0:1, :]
    z = _dot(h.astype(BF16), w_ref[...])
    ona_ref[...] = z[:, 0:Z_NA]
    os5_ref[...] = z[:, Z_NA:Z_NA + Z_S5]
    ogq_ref[...] = z[:, Z_NA + Z_S5:Z_NA + Z_S5 + Z_GQ]
    omla_ref[...] = z[:, Z_NA + Z_S5 + Z_GQ:Z_ALL]


def _inproj(x, mods, g, w, mod_base, rows_per_mod):
    n = x.shape[0]
    tm = PROJ_ROWS
    row = lambda i: (mod_base + (i * tm) // rows_per_mod, 0, 0)
    widths = (Z_NA, Z_S5, Z_GQ, Z_MLA)
    return pl.pallas_call(
        _inproj_kernel,
        grid=(n // tm,),
        in_specs=[pl.BlockSpec((tm, D_MODEL), lambda i: (i, 0)),
                  pl.BlockSpec((1, 6, D_MODEL), row),
                  pl.BlockSpec((1, D_MODEL), lambda i: (0, 0)),
                  pl.BlockSpec((D_MODEL, Z_ALL), lambda i: (0, 0))],
        out_specs=[pl.BlockSpec((tm, wd), lambda i: (i, 0)) for wd in widths],
        out_shape=[jax.ShapeDtypeStruct((n, wd), F32) for wd in widths],
        compiler_params=_cparams(("arbitrary",)),
        name="inproj",
    )(x, mods, g, w)


CTX_SEQS = 2


def _store_heads(ref, i, x, heads):
    for h in range(heads):
        ref[i, 0, h] = x[:, h * DH:(h + 1) * DH]


def _ctx_attn_kernel(zna_ref, zgq_ref, zm_ref, naq_g, nak_g, gqq_g, gqk_g, qa_g, kva_g, mqn_g, mkn_g,
                     wuq_ref, wuk_ref, wuv_ref, *refs):
    ona_ref, ogq_ref, omla_ref, nk_ref, nv_ref, gk_ref, gv_ref, ckv_ref, kr_ref = refs[-9:]
    for ref in refs[-6:]:
        for later in range(1, ref.shape[1]):
            ref[:, later] = jnp.zeros((CTX_SEQS,) + ref.shape[2:], F32)
    for i in range(CTX_SEQS):
        rows = slice(i * SEQ, (i + 1) * SEQ)
        zna = zna_ref[rows, :]
        q = _seg_rms(zna[:, 0:MIX], DH, DH, naq_g[...])
        k = _seg_rms(zna[:, MIX:2 * MIX], DH, DH, nak_g[...])
        v = zna[:, 2 * MIX:3 * MIX]
        _store_heads(nk_ref, i, k, NA_HEADS)
        _store_heads(nv_ref, i, v, NA_HEADS)
        ona_ref[rows, :] = _packed_attn(q, [(k.astype(BF16), v.astype(BF16))], NA_HEADS, DH ** -0.5)

        zgq = zgq_ref[rows, :]
        gq = _seg_rms(zgq[:, 0:MIX], DH, DH, gqq_g[...])
        gk = _seg_rms(zgq[:, MIX:MIX + LANES], DH, DH, gqk_g[...])
        gv = zgq[:, MIX + LANES:2 * MIX]
        _store_heads(gk_ref, i, gk, GQ_KV)
        _store_heads(gv_ref, i, gv, GQ_KV)
        kv = [(gk.astype(BF16), gv.astype(BF16))]
        ogq_ref[rows, :] = jnp.concatenate(
            [_packed_attn(gq[:, r * LANES:(r + 1) * LANES], kv, GQ_KV, DH ** -0.5) for r in range(2)], axis=-1)

        mq, mk, mv, ckv, col3 = _mla_qkv(zm_ref[rows, :], qa_g[...], kva_g[...], mqn_g[...], mkn_g[...],
                                         wuq_ref[...], wuk_ref[...], wuv_ref[...])
        ckv_ref[i, 0] = ckv
        kr_ref[i, 0] = pltpu.roll(col3, LANES - MLA_NOPE, axis=1)[:, 0:MLA_ROPE]
        omla_ref[rows, :] = _mla_attn(mq, [(mk.astype(BF16), mv.astype(BF16))], MLA_QK ** -0.5)


def _full(shape):
    nd = len(shape)
    return pl.BlockSpec(shape, lambda *a: (0,) * nd)


CACHE_SHAPES = ((NA_HEADS, SEQ, DH), (NA_HEADS, SEQ, DH), (GQ_KV, SEQ, DH), (GQ_KV, SEQ, DH),
                (SEQ, MLA_KVLORA), (SEQ, MLA_ROPE))


def _ctx_attn(zna, zgq, zm, lw, layer, caches):
    n = zna.shape[0]
    t = CTX_SEQS * SEQ
    rows = lambda wd: pl.BlockSpec((t, wd), lambda b: (b, 0))
    if caches:
        cache_spec = lambda shp: pl.BlockSpec((CTX_SEQS, 1) + shp, lambda b: (b, layer) + (0,) * len(shp))
    else:
        assert layer == 0
        cache_spec = lambda shp: pl.BlockSpec((CTX_SEQS, DEPTH) + shp, lambda b: (b, 0) + (0,) * len(shp))
    gains = [lw["naq_g"], lw["nak_g"], lw["gqq_g"], lw["gqk_g"], lw["qa_g"], lw["kva_g"], lw["mqn_g"], lw["mkn_g"]]
    weights = [lw["wuq"], lw["wuk"], lw["wuv"]]
    n_in = 3 + len(gains) + len(weights)
    outs = pl.pallas_call(
        _ctx_attn_kernel,
        grid=(n // t,),
        in_specs=[rows(Z_NA), rows(Z_GQ), rows(Z_MLA)] + [_full(a.shape) for a in gains + weights]
                 + [pl.BlockSpec(memory_space=pl.ANY)] * len(caches),
        out_specs=[rows(MIX)] * 3 + [cache_spec(shp) for shp in CACHE_SHAPES],
        out_shape=[jax.ShapeDtypeStruct((n, MIX), F32)] * 3
                  + [jax.ShapeDtypeStruct((n // SEQ, DEPTH) + shp, F32) for shp in CACHE_SHAPES],
        input_output_aliases={n_in + i: 3 + i for i in range(len(caches))},
        compiler_params=_cparams(("arbitrary",)),
        name="ctx_attn",
    )(zna, zgq, zm, *gains, *weights, *caches)
    return outs[:3], list(outs[3:])


NA_ROWS_PER_STEP = 4


def _na_window_start(r):
    return jnp.clip(r - NA_WIN_R // 2, 0, GRID_ROWS - NA_WIN_R)


def _pack_heads(ref):
    return jnp.concatenate([ref[0, 0, h] for h in range(ref.shape[2])], axis=-1)


def _na_lat_kernel(zna_ref, kc_ref, vc_ref, bias_ref, naq_g, nak_g, o_ref, k_scr, v_scr, kc_scr, vc_scr):
    r = pl.program_id(1)

    @pl.when(r == 0)
    def _():
        k_scr[...] = _seg_rms(zna_ref[:, MIX:2 * MIX], DH, DH, nak_g[...]).astype(BF16)
        v_scr[...] = zna_ref[:, 2 * MIX:3 * MIX].astype(BF16)
        kc_scr[...] = _pack_heads(kc_ref).astype(BF16)
        vc_scr[...] = _pack_heads(vc_ref).astype(BF16)

    span = NA_WIN_R * GRID_W
    for sub in range(NA_ROWS_PER_STEP):
        row = r * NA_ROWS_PER_STEP + sub
        q = _seg_rms(zna_ref[pl.ds(pl.multiple_of(row * GRID_W, GRID_W), GRID_W), 0:MIX], DH, DH, naq_g[...])
        start = pl.multiple_of(_na_window_start(row) * GRID_W, GRID_W)
        parts = [(k_scr[pl.ds(start, span), :], v_scr[pl.ds(start, span), :]),
                 (kc_scr[...], vc_scr[...])]
        d0 = _na_window_start(row) - row + NA_WIN_R - 1
        bias = jnp.concatenate([bias_ref[d0 + 2 * i] for i in range(NA_WIN_R // 2)], axis=-1)
        o_ref[sub * GRID_W:(sub + 1) * GRID_W, :] = _packed_attn(q, parts, NA_HEADS, DH ** -0.5,
                                                                 biases=[bias, None])


def _na_lat(zna, kc, vc, layer, bias, lw):
    nb = DEC_BATCH
    cache = pl.BlockSpec((1, 1, NA_HEADS, PAST_LEN, DH), lambda b, r: (b, layer, 0, 0, 0))
    return pl.pallas_call(
        _na_lat_kernel,
        grid=(nb, GRID_ROWS // NA_ROWS_PER_STEP),
        in_specs=[pl.BlockSpec((DEC_SEQ, Z_NA), lambda b, r: (b, 0)),
                  cache, cache,
                  _full(bias.shape),
                  _full(lw["naq_g"].shape), _full(lw["nak_g"].shape)],
        out_specs=pl.BlockSpec((NA_ROWS_PER_STEP * GRID_W, MIX),
                               lambda b, r: (b * (GRID_ROWS // NA_ROWS_PER_STEP) + r, 0)),
        out_shape=jax.ShapeDtypeStruct((nb * DEC_SEQ, MIX), F32),
        scratch_shapes=[pltpu.VMEM((DEC_SEQ, MIX), BF16), pltpu.VMEM((DEC_SEQ, MIX), BF16),
                        pltpu.VMEM((PAST_LEN, MIX), BF16), pltpu.VMEM((PAST_LEN, MIX), BF16)],
        compiler_params=_cparams(("arbitrary", "arbitrary")),
        name="na_latent",
    )(zna, kc, vc, bias, lw["naq_g"], lw["nak_g"])


LAT_QB = 256
LAT_KEYS = DEC_SEQ + PAST_LEN


def _gq_lat_kernel(zgq_ref, kc_ref, vc_ref, cos_ref, sa_ref, sb_ref, gqq_g, gqk_g, o_ref, k_scr, v_scr):
    j = pl.program_id(1)

    @pl.when(j == 0)
    def _():
        k = _seg_rms(zgq_ref[:, MIX:MIX + LANES], DH, DH, gqk_g[...])
        k = _rope(k, cos_ref[...], sa_ref[...], sb_ref[...], DH // 4)
        k_scr[0:DEC_SEQ, :] = k.astype(BF16)
        k_scr[DEC_SEQ:LAT_KEYS, :] = _pack_heads(kc_ref).astype(BF16)
        v_scr[0:DEC_SEQ, :] = zgq_ref[:, MIX + LANES:2 * MIX].astype(BF16)
        v_scr[DEC_SEQ:LAT_KEYS, :] = _pack_heads(vc_ref).astype(BF16)

    rows = pl.ds(pl.multiple_of(j * LAT_QB, LAT_QB), LAT_QB)
    q = _seg_rms(zgq_ref[rows, 0:MIX], DH, DH, gqq_g[...])
    q = _rope(q, cos_ref[rows, :], sa_ref[rows, :], sb_ref[rows, :], DH // 4)
    kv = [(k_scr[...], v_scr[...])]
    o_ref[...] = jnp.concatenate(
        [_packed_attn(q[:, r * LANES:(r + 1) * LANES], kv, GQ_KV, DH ** -0.5) for r in range(2)], axis=-1)


def _gq_lat(zgq, kc, vc, layer, tabs, lw):
    nb = DEC_BATCH
    nq = DEC_SEQ // LAT_QB
    cache = pl.BlockSpec((1, 1, GQ_KV, PAST_LEN, DH), lambda b, j: (b, layer, 0, 0, 0))
    return pl.pallas_call(
        _gq_lat_kernel,
        grid=(nb, nq),
        in_specs=[pl.BlockSpec((DEC_SEQ, Z_GQ), lambda b, j: (b, 0)), cache, cache]
                 + [_full(t.shape) for t in tabs] + [_full(lw["gqq_g"].shape), _full(lw["gqk_g"].shape)],
        out_specs=pl.BlockSpec((LAT_QB, MIX), lambda b, j: (b * nq + j, 0)),
        out_shape=jax.ShapeDtypeStruct((nb * DEC_SEQ, MIX), F32),
        scratch_shapes=[pltpu.VMEM((LAT_KEYS, LANES), BF16), pltpu.VMEM((LAT_KEYS, LANES), BF16)],
        compiler_params=_cparams(("arbitrary", "arbitrary")),
        name="gq_latent",
    )(zgq, kc, vc, *tabs, lw["gqq_g"], lw["gqk_g"])


def _mla_lat_kernel(zm_ref, ckvc_ref, krc_ref, cos_ref, sa_ref, sb_ref, qa_g, kva_g, mqn_g, mkn_g,
                    wuq_ref, wuk_ref, wuv_ref, o_ref, k_scr, v_scr):
    j = pl.program_id(1)
    half = MLA_ROPE // 4

    @pl.when(j == 0)
    def _():
        zm = zm_ref[...]
        ckv = _rms_rows(zm[:, 0:LANES], kva_g[...])
        col3 = zm[:, 2 * LANES:3 * LANES]
        lane = lax.broadcasted_iota(jnp.int32, col3.shape, 1)
        kr_placed = jnp.where((lane >= MLA_NOPE) & (lane < MLA_QK), col3, 0.0)
        k, v = _mla_kv(ckv, kr_placed, mkn_g[...], wuk_ref[...], wuv_ref[...])
        k = _rope(k, cos_ref[...], sa_ref[...], sb_ref[...], half)
        k_scr[0:DEC_SEQ, :] = k.astype(BF16)
        v_scr[0:DEC_SEQ, :] = v.astype(BF16)
        krc = jnp.concatenate([jnp.zeros((PAST_LEN, MLA_NOPE), F32), krc_ref[0, 0],
                               jnp.zeros((PAST_LEN, LANES - MLA_QK), F32)], axis=-1)
        kc, vc = _mla_kv(ckvc_ref[0, 0], krc, mkn_g[...], wuk_ref[...], wuv_ref[...])
        k_scr[DEC_SEQ:LAT_KEYS, :] = kc.astype(BF16)
        v_scr[DEC_SEQ:LAT_KEYS, :] = vc.astype(BF16)

    rows = pl.ds(pl.multiple_of(j * LAT_QB, LAT_QB), LAT_QB)
    cq = _rms_rows(zm_ref[rows, LANES:3 * LANES], qa_g[...], valid=MLA_QLORA)
    q = _seg_rms(_dot(cq.astype(BF16), wuq_ref[...]), LANES, MLA_QK, mqn_g[...])
    q = _rope(q, cos_ref[rows, :], sa_ref[rows, :], sb_ref[rows, :], half)
    o_ref[...] = _mla_attn(q, [(k_scr[...], v_scr[...])], MLA_QK ** -0.5)


def _mla_lat(zm, ckvc, krc, layer, tabs, lw):
    nb = DEC_BATCH
    nq = DEC_SEQ // LAT_QB
    small = [lw["qa_g"], lw["kva_g"], lw["mqn_g"], lw["mkn_g"], lw["wuq"], lw["wuk"], lw["wuv"]]
    return pl.pallas_call(
        _mla_lat_kernel,
        grid=(nb, nq),
        in_specs=[pl.BlockSpec((DEC_SEQ, Z_MLA), lambda b, j: (b, 0)),
                  pl.BlockSpec((1, 1, PAST_LEN, MLA_KVLORA), lambda b, j: (b, layer, 0, 0)),
                  pl.BlockSpec((1, 1, PAST_LEN, MLA_ROPE), lambda b, j: (b, layer, 0, 0))]
                 + [_full(t.shape) for t in tabs] + [_full(a.shape) for a in small],
        out_specs=pl.BlockSpec((LAT_QB, MIX), lambda b, j: (b * nq + j, 0)),
        out_shape=jax.ShapeDtypeStruct((nb * DEC_SEQ, MIX), F32),
        scratch_shapes=[pltpu.VMEM((LAT_KEYS, MLA_HEADS * LANES), BF16), pltpu.VMEM((LAT_KEYS, MIX), BF16)],
        compiler_params=_cparams(("arbitrary", "arbitrary")),
        name="mla_latent",
    )(zm, ckvc, krc, *tabs, *small)


def _s5_prep_kernel(bb_ref, qj_ref, m_ref):
    bb, qj = bb_ref[0], qj_ref[0]
    kt = _dot_nt_f32(bb[:, 0:LANES], qj[:, 0:LANES])
    kr = _dot_nt_f32(bb[:, LANES:2 * LANES], qj[:, LANES:2 * LANES])
    lane = lax.broadcasted_iota(jnp.int32, kt.shape, 1)
    blocks = []
    for s_pos in range(S5_CHUNK):
        sh_f = S5_GROUP * s_pos
        sh_b = S5_GROUP * (S5_CHUNK - 1 - s_pos)
        f = jnp.where(lane >= sh_f, pltpu.roll(kt, sh_f, axis=1), 0.0) if sh_f else kt
        b = jnp.where(lane < S5_ROW - sh_b, pltpu.roll(kr, S5_ROW - sh_b, axis=1), 0.0) if sh_b else kr
        blocks.append(f + b)
    m_ref[0] = jnp.concatenate(blocks, axis=0).astype(BF16)


def _s5_prep(lw):
    blk = lambda *shape: pl.BlockSpec((1,) + shape, lambda i: (i, 0, 0))
    return pl.pallas_call(
        _s5_prep_kernel,
        grid=(S5_GROUPS,),
        in_specs=[blk(S5_GROUP, S5_ROW), blk(S5_ROW, S5_ROW)],
        out_specs=blk(S5_ROW, S5_ROW),
        out_shape=jax.ShapeDtypeStruct((S5_GROUPS, S5_ROW, S5_ROW), BF16),
        compiler_params=_cparams(("arbitrary",)),
        name="s5_prep",
    )(lw["s5_bb"], lw["s5_qj"])


S5_TILE_BLOCKS = LANES // S5_GROUP


S5_INTERLEAVE = 4


def _block_transpose(xs, lane_blk):
    n = len(xs)
    d = n // 2
    while d >= 1:
        hi = (lane_blk & d) != 0
        nxt = list(xs)
        for i in range(n):
            if i & d:
                continue
            lo_arr, hi_arr = xs[i], xs[i + d]
            nxt[i] = jnp.where(hi, pltpu.roll(hi_arr, d * S5_GROUP, axis=1), lo_arr)
            nxt[i + d] = jnp.where(hi, hi_arr, pltpu.roll(lo_arr, LANES - d * S5_GROUP, axis=1))
        xs = nxt
        d //= 2
    return xs


def _s5_kernel(nb, nch, ua_ref, ub_ref, m_ref, ws_ref, wct_ref, a_ref, d_ref, h0_ref, ya_ref, yb_ref, fin_ref,
               ug_scr, yg_scr, *tmp_scr):
    rows = nb * nch
    per = S5_TILE_BLOCKS
    lane_blk = lax.broadcasted_iota(jnp.int32, (rows, LANES), 1) // S5_GROUP
    u_refs = (ua_ref, ub_ref)
    y_refs = (ya_ref, yb_ref)
    tile = lambda t: slice(t * LANES, (t + 1) * LANES)

    for half in range(S5_GROUPS // per):
        for t in range(S5_CHUNK // per):
            pieces = [u_refs[half][pl.ds(t * per + i, rows, stride=S5_CHUNK), :] for i in range(per)]
            for j, arr in enumerate(_block_transpose(pieces, lane_blk)):
                ug_scr[half * per + j, :, tile(t)] = arr

    def group_set(i, carry):
        gs = [i * S5_INTERLEAVE + n for n in range(S5_INTERLEAVE)]
        sets = [tmp_scr[6 * n:6 * n + 6] for n in range(S5_INTERLEAVE)]
        state = []
        for g, (sf_scr, sb_scr, sfs_scr, sbs_scr, _, _) in zip(gs, sets):
            s = _dot(ug_scr[g].astype(BF16), ws_ref[g])
            sf_scr[...] = s[:, 0:LANES]
            sb_scr[...] = s[:, LANES:2 * LANES]
            sfs_scr[...] = pltpu.roll(s[:, 0:LANES], S5_N, axis=1)
            sbs_scr[...] = pltpu.roll(s[:, LANES:2 * LANES], S5_N, axis=1)
            h0 = h0_ref[g]
            h_f = h0[:, 0:LANES]
            h_b = h0[:, LANES:2 * LANES]
            state.append([h_f, pltpu.roll(h_f, S5_N, axis=1), h_b, pltpu.roll(h_b, S5_N, axis=1)])
        for k in range(nch):
            rows_f = pl.ds(k, nb, stride=nch)
            rows_b = pl.ds(nch - 1 - k, nb, stride=nch)
            for n, (g, (sf_scr, sb_scr, sfs_scr, sbs_scr, hf_scr, hb_scr)) in enumerate(zip(gs, sets)):
                a = a_ref[g]
                h_f, g_f, h_b, g_b = state[n]
                hf_scr[rows_f, :] = h_f
                hb_scr[rows_b, :] = h_b
                state[n] = [a[0:1] * h_f + a[1:2] * g_f + sf_scr[rows_f, :],
                            a[0:1] * g_f + a[2:3] * h_f + sfs_scr[rows_f, :],
                            a[3:4] * h_b + a[4:5] * g_b + sb_scr[rows_b, :],
                            a[3:4] * g_b + a[5:6] * h_b + sbs_scr[rows_b, :]]
        for n, (g, (_, _, _, _, hf_scr, hb_scr)) in enumerate(zip(gs, sets)):
            fin_ref[g] = jnp.concatenate([state[n][0], state[n][2]], axis=-1)
            hp = jnp.concatenate([hf_scr[...], hb_scr[...]], axis=-1).astype(BF16)
            x = ug_scr[g]
            yg_scr[g] = _dot(x.astype(BF16), m_ref[g]) + _dot_nt(hp, wct_ref[g]) + d_ref[g] * x
        return carry

    lax.fori_loop(0, S5_GROUPS // S5_INTERLEAVE, group_set, 0)

    for half in range(S5_GROUPS // per):
        for t in range(S5_CHUNK // per):
            pieces = [yg_scr[half * per + j, :, tile(t)] for j in range(per)]
            for i, arr in enumerate(_block_transpose(pieces, lane_blk)):
                y_refs[half][pl.ds(t * per + i, rows, stride=S5_CHUNK), :] = arr


def _s5_mixer(u, m, lw, h0, nb, nb_step):
    n = u.shape[0]
    nch = n // nb // S5_CHUNK
    tok = nb_step * nch * S5_CHUNK
    rows = nb_step * nch
    g = S5_GROUPS
    res = lambda a: pl.BlockSpec(a.shape, lambda i: (0, 0, 0))
    state = pl.BlockSpec((g, nb_step, S5_ROW), lambda i: (0, i, 0))
    half = lambda j: pl.BlockSpec((tok, LANES), lambda i: (i, j))
    ya, yb, fin = pl.pallas_call(
        functools.partial(_s5_kernel, nb_step, nch),
        grid=(nb // nb_step,),
        in_specs=[half(0), half(1), res(m), res(lw["s5_ws"]), res(lw["s5_wct"]), res(lw["s5_a"]), res(lw["s5_d"]), state],
        out_specs=[half(0), half(0), state],
        out_shape=[jax.ShapeDtypeStruct((n, LANES), F32), jax.ShapeDtypeStruct((n, LANES), F32),
                   jax.ShapeDtypeStruct((g, nb, S5_ROW), F32)],
        scratch_shapes=[pltpu.VMEM((g, rows, S5_ROW), F32), pltpu.VMEM((g, rows, S5_ROW), F32)]
                       + [pltpu.VMEM((rows, LANES), F32)] * (6 * S5_INTERLEAVE),
        compiler_params=_cparams(("arbitrary",)),
        name="s5_mixer",
    )(u, u, m, lw["s5_ws"], lw["s5_wct"], lw["s5_a"], lw["s5_d"], h0)
    return ya, yb, fin


def _outproj_kernel(ona_ref, ys5a_ref, ys5b_ref, ogq_ref, omla_ref, x_ref, mod_ref, w_ref, wglu_ref, bglu_ref,
                    o_ref):
    y = jax.nn.gelu(jnp.concatenate([ys5a_ref[...], ys5b_ref[...]], axis=-1), approximate=True)
    y = y * jax.nn.sigmoid(_dot(y.astype(BF16), wglu_ref[...]) + bglu_ref[...])
    mixed = _dot(ona_ref[...].astype(BF16), w_ref[0:MIX, :])
    mixed = mixed + _dot(y.astype(BF16), w_ref[MIX:2 * MIX, :])
    mixed = mixed + _dot(ogq_ref[...].astype(BF16), w_ref[2 * MIX:3 * MIX, :])
    mixed = mixed + _dot(omla_ref[...].astype(BF16), w_ref[3 * MIX:4 * MIX, :])
    o_ref[...] = x_ref[...] + mod_ref[0, 2:3, :] * mixed


def _outproj(ona, ys5a, ys5b, ogq, omla, x, mods, lw, mod_base, rows_per_mod):
    n = x.shape[0]
    tm = PROJ_ROWS
    row = lambda i: (mod_base + (i * tm) // rows_per_mod, 0, 0)
    part = pl.BlockSpec((tm, MIX), lambda i: (i, 0))
    half = pl.BlockSpec((tm, LANES), lambda i: (i, 0))
    return pl.pallas_call(
        _outproj_kernel,
        grid=(n // tm,),
        in_specs=[part, half, half, part, part,
                  pl.BlockSpec((tm, D_MODEL), lambda i: (i, 0)),
                  pl.BlockSpec((1, 6, D_MODEL), row),
                  _full(lw["w_out"].shape), _full(lw["w_glu"].shape), _full(lw["b_glu"].shape)],
        out_specs=pl.BlockSpec((tm, D_MODEL), lambda i: (i, 0)),
        out_shape=jax.ShapeDtypeStruct((n, D_MODEL), F32),
        compiler_params=_cparams(("arbitrary",)),
        name="outproj",
    )(ona, ys5a, ys5b, ogq, omla, x, mods, lw["w_out"], lw["w_glu"], lw["b_glu"])


FF_ROWS = 1024
FFN_VMEM_LIMIT = 58 * 1024 * 1024


def _ffn_kernel(seq, x_ref, mod_ref, g_ref, wup_ref, cw_ref, cb_ref, wd_ref, o_ref, h_scr, ug_scr, uu_scr, act_scr):
    x = x_ref[...]
    y = x * lax.rsqrt(jnp.mean(x * x, axis=-1, keepdims=True) + EPS) * g_ref[...]
    h_scr[...] = (y * (1.0 + mod_ref[0, 4:5, :]) + mod_ref[0, 3:4, :]).astype(BF16)

    pos = lax.broadcasted_iota(jnp.int32, (FF_ROWS, FF_CHUNK), 0) % seq
    first = pos == 0
    last = pos == seq - 1

    def conv(u, cw, cb):
        prev = jnp.where(first, 0.0, pltpu.roll(u, 1, axis=0))
        nxt = jnp.where(last, 0.0, pltpu.roll(u, FF_ROWS - 1, axis=0))
        return cw[0:1, :] * prev + cw[1:2, :] * u + cw[2:3, :] * nxt + cb

    def cols(j, base=0):
        return pl.ds(pl.multiple_of(base + j * FF_CHUNK, FF_CHUNK), FF_CHUNK)

    def up_proj(j, slot):
        h = h_scr[...]
        ug_scr[slot] = _dot(h, wup_ref[:, cols(j)])
        uu_scr[slot] = _dot(h, wup_ref[:, cols(j, D_FF)])

    def gate_act(j, slot):
        gate = conv(ug_scr[slot], cw_ref[:, cols(j)], cb_ref[:, cols(j)])
        up = conv(uu_scr[slot], cw_ref[:, cols(j, D_FF)], cb_ref[:, cols(j, D_FF)])
        act_scr[slot] = (gate * jax.nn.sigmoid(gate) * up).astype(BF16)

    def down_proj(j, slot):
        return _dot(act_scr[slot], wd_ref[cols(j), :])

    last_j = FF_STEPS - 1
    up_proj(0, 0)
    up_proj(1, 1)
    gate_act(0, 0)
    up_proj(2, 0)
    gate_act(1, 1)
    o_ref[...] = down_proj(0, 0)

    def body(j, carry):
        slot = j % 2
        o_ref[...] += down_proj(j, slot)
        gate_act(j + 1, 1 - slot)
        up_proj(j + 2, slot)
        return carry

    lax.fori_loop(1, last_j - 1, body, 0)
    gate_act(last_j, last_j % 2)
    o_ref[...] += down_proj(last_j - 1, (last_j - 1) % 2)
    o_ref[...] = x_ref[...] + mod_ref[0, 5:6, :] * (o_ref[...] + down_proj(last_j, last_j % 2))


def _ffn(x, mods, lw, fw, layer, mod_base, rows_per_mod, seq):
    n = x.shape[0]
    row = lambda i: (mod_base + (i * FF_ROWS) // rows_per_mod, 0, 0)
    resident = lambda a: pl.BlockSpec((None,) + a.shape[1:], lambda i: (layer,) + (0,) * (a.ndim - 1),
                                      pipeline_mode=pl.Buffered(1))
    return pl.pallas_call(
        functools.partial(_ffn_kernel, seq),
        grid=(n // FF_ROWS,),
        in_specs=[pl.BlockSpec((FF_ROWS, D_MODEL), lambda i: (i, 0)),
                  pl.BlockSpec((1, 6, D_MODEL), row),
                  pl.BlockSpec((1, D_MODEL), lambda i: (0, 0)),
                  resident(fw["w_up"]), resident(fw["conv_w"]), resident(fw["conv_b"]), resident(fw["w_down"])],
        out_specs=pl.BlockSpec((FF_ROWS, D_MODEL), lambda i: (i, 0)),
        out_shape=jax.ShapeDtypeStruct((n, D_MODEL), F32),
        scratch_shapes=[pltpu.VMEM((FF_ROWS, D_MODEL), BF16),
                        pltpu.VMEM((2, FF_ROWS, FF_CHUNK), F32), pltpu.VMEM((2, FF_ROWS, FF_CHUNK), F32),
                        pltpu.VMEM((2, FF_ROWS, FF_CHUNK), BF16)],
        compiler_params=pltpu.CompilerParams(dimension_semantics=("arbitrary",), vmem_limit_bytes=FFN_VMEM_LIMIT),
        name="conv_ffn",
    )(x, mods, lw["norm2_g"], fw["w_up"], fw["conv_w"], fw["conv_b"], fw["w_down"])


def _rope_tables():
    pos = np.arange(DEC_SEQ)

    def ang(p, half):
        inv = ROPE_BASE ** (-np.arange(half, dtype=np.float64) / half)
        a = p.astype(np.float64)[:, None] * inv[None, :]
        return np.concatenate([a, a], axis=-1)

    def tables(dim, lanes_before, lanes_after, reps):
        a = np.concatenate([ang(pos // GRID_W, dim // 4), ang(pos % GRID_W, dim // 4)], axis=-1)
        half = dim // 4
        first = (np.arange(dim) % (2 * half)) < half
        cos = np.cos(a)
        sin_a = np.where(first[None, :], -np.sin(a), 0.0)
        sin_b = np.where(first[None, :], 0.0, np.sin(a))

        def place(t, fill):
            t = np.concatenate([np.full((DEC_SEQ, lanes_before), fill), t,
                                np.full((DEC_SEQ, lanes_after), fill)], axis=-1)
            return jnp.asarray(np.tile(t, (1, reps)), F32)

        return place(cos, 1.0), place(sin_a, 0.0), place(sin_b, 0.0)

    gq = tables(DH, 0, 0, LANES // DH)
    mla = tables(MLA_ROPE, MLA_NOPE, LANES - MLA_QK, 1)
    return gq, mla


def _na_bias_tables(rpb):
    qc = np.arange(GRID_W)[:, None]
    kc = np.arange(GRID_W)[None, :]
    col_start = np.clip(qc - NA_WIN_C // 2, 0, GRID_W - NA_WIN_C)
    valid = (kc >= col_start) & (kc < col_start + NA_WIN_C)
    d_c = np.clip(kc - qc, 1 - NA_WIN_C, NA_WIN_C - 1) + NA_WIN_C - 1
    nrel = 2 * NA_WIN_C - 1
    onehot = jnp.asarray((d_c.reshape(-1)[None, :] == np.arange(nrel)[:, None]).astype(np.float32))
    t = jnp.dot(rpb.astype(F32).reshape(-1, nrel), onehot, precision=lax.Precision.HIGHEST)
    t = t.reshape(NA_HEADS, 2 * NA_WIN_R - 1, GRID_W, GRID_W)
    t = jnp.where(jnp.asarray(valid)[None, None], t, NEG)
    t = t.transpose(1, 0, 2, 3).reshape(2 * NA_WIN_R - 1, NA_HEADS * GRID_W, GRID_W)
    return jnp.concatenate([t[:-1], t[1:]], axis=-1)


def _s5_matrices(lam_re, lam_im, log_dt, b_re, b_im, c_re, c_im):
    c = S5_CHUNK
    cat = lambda parts: jnp.concatenate(parts, axis=-1)
    pw, bre, bim, cre, cim = [], [], [], [], []
    for d in range(2):
        dt = jnp.exp(log_dt[d].astype(F32))[:, None]
        lr, li = lam_re[d].astype(F32), lam_im[d].astype(F32)
        mag = jnp.exp(lr * dt)
        a_re, a_im = mag * jnp.cos(li * dt), mag * jnp.sin(li * dt)
        den = lr * lr + li * li
        f_re = ((a_re - 1.0) * lr + a_im * li) / den
        f_im = (a_im * lr - (a_re - 1.0) * li) / den
        br, bi = b_re[d].astype(F32), b_im[d].astype(F32)
        bre.append((f_re[..., None] * br - f_im[..., None] * bi).transpose(0, 2, 1))
        bim.append((f_re[..., None] * bi + f_im[..., None] * br).transpose(0, 2, 1))
        cre.append(c_re[d].astype(F32))
        cim.append(c_im[d].astype(F32))
        p_re, p_im = [jnp.ones_like(a_re)], [jnp.zeros_like(a_re)]
        for _ in range(c):
            p_re, p_im = (p_re + [p_re[-1] * a_re - p_im[-1] * a_im], p_im + [p_re[-1] * a_im + p_im[-1] * a_re])
        p_re, p_im = jnp.stack(p_re, axis=1), jnp.stack(p_im, axis=1)
        pw.append((p_re, p_im))
    (pfr, pfi), (pbr, pbi) = pw
    up = lambda x, off=0: x[:, off:off + c]
    dn = lambda x, off=0: x[:, off:off + c][:, ::-1]
    outer = lambda x, y: (x[:, :, None, :] * y[:, None, :, :])
    e_fr, e_fi, e_br, e_bi = dn(pfr), dn(pfi), up(pbr), up(pbi)
    bb = cat([bre[0], bim[0], bre[1], bim[1]])
    ws = (outer(cat([e_fr, e_fr, e_br, e_br]), bb)
          + outer(cat([-e_fi, e_fi, -e_bi, e_bi]), cat([bim[0], bre[0], bim[1], bre[1]])))
    e_fr, e_fi, e_br, e_bi = up(pfr, 1), up(pfi, 1), dn(pbr, 1), dn(pbi, 1)
    c_a, c_b = cat([cre[0], cre[0], cre[1], cre[1]]), cat([cim[0], cim[0], cim[1], cim[1]])
    wct = outer(cat([e_fr, -e_fi, e_br, -e_bi]), c_a) + outer(cat([-e_fi, -e_fr, -e_bi, -e_br]), c_b)
    e_fr, e_fi, e_br, e_bi = up(pfr), up(pfi), dn(pbr), dn(pbi)
    qj = outer(cat([e_fr, -e_fi, e_br, -e_bi]), c_a) + outer(cat([-e_fi, -e_fr, -e_bi, -e_br]), c_b)
    rows = lambda w: w.reshape(S5_GROUPS, S5_ROW, 4 * S5_N).astype(BF16)
    av = []
    for p_re, p_im in ((pfr[:, c], pfi[:, c]), (pbr[:, c], pbi[:, c])):
        av += [cat([p_re, p_re]), cat([-p_im, p_im]), cat([p_im, -p_im])]
    return rows(ws), rows(wct), bb, qj.reshape(S5_GROUPS, S5_ROW, 4 * S5_N), jnp.stack(av, axis=1)


def _layer_weights(l, p):
    w = p["w_in"][l]
    q0 = Z_NA + Z_S5
    gq = lambda h: w[:, q0 + h * DH:q0 + (h + 1) * DH]
    m0 = q0 + 2 * MIX
    w_in = jnp.concatenate(
        [w[:, 0:q0], gq(0), gq(2), gq(1), gq(3), w[:, q0 + MIX:m0],
         w[:, m0 + MLA_QLORA:m0 + MLA_QLORA + MLA_KVLORA], w[:, m0:m0 + MLA_QLORA],
         w[:, m0 + MLA_QLORA + MLA_KVLORA:], jnp.zeros((D_MODEL, Z_ALL - w.shape[1]), w.dtype)],
        axis=1).astype(BF16)
    wo = p["w_out"][l]
    og = lambda h: wo[2 * MIX + h * DH:2 * MIX + (h + 1) * DH]
    w_out = jnp.concatenate([wo[0:2 * MIX], og(0), og(2), og(1), og(3), wo[3 * MIX:]], axis=0).astype(BF16)
    pad_head = lambda g: jnp.tile(jnp.pad(g, (0, LANES - MLA_QK)), MLA_HEADS)[None, :]
    wuq = jnp.pad(p["mla_w_uq"][l].reshape(MLA_QLORA, MLA_HEADS, MLA_QK),
                  ((0, 2 * LANES - MLA_QLORA), (0, 0), (0, LANES - MLA_QK))).reshape(2 * LANES, MLA_HEADS * LANES)
    wukv = p["mla_w_ukv"][l].reshape(MLA_KVLORA, MLA_HEADS, MLA_NOPE + DH)
    wuk = jnp.pad(wukv[:, :, :MLA_NOPE], ((0, 0), (0, 0), (0, LANES - MLA_NOPE))).reshape(MLA_KVLORA, MLA_HEADS * LANES)
    wuv = wukv[:, :, MLA_NOPE:].reshape(MLA_KVLORA, MIX)
    s5_ws, s5_wct, s5_bb, s5_qj, s5_a = _s5_matrices(p["s5_lam_re"][l], p["s5_lam_im"][l], p["s5_log_dt"][l],
                                                     p["s5_b_re"][l], p["s5_b_im"][l], p["s5_c_re"][l], p["s5_c_im"][l])
    return dict(
        norm1_g=p["norm1_g"][l][None, :], norm2_g=p["norm2_g"][l][None, :],
        w_in=w_in, w_out=w_out,
        naq_g=jnp.tile(p["na_qn"][l], NA_HEADS)[None, :], nak_g=jnp.tile(p["na_kn"][l], NA_HEADS)[None, :],
        gqq_g=jnp.tile(p["gq_qn"][l], GQ_HEADS)[None, :], gqk_g=jnp.tile(p["gq_kn"][l], GQ_KV)[None, :],
        qa_g=jnp.pad(p["mla_qa_g"][l], (0, 2 * LANES - MLA_QLORA))[None, :], kva_g=p["mla_kva_g"][l][None, :],
        mqn_g=pad_head(p["mla_qn"][l]), mkn_g=pad_head(p["mla_kn"][l]),
        wuq=wuq.astype(BF16), wuk=wuk.astype(BF16), wuv=wuv.astype(BF16),
        na_bias=_na_bias_tables(p["na_rpb"][l]),
        s5_ws=s5_ws, s5_wct=s5_wct, s5_bb=s5_bb, s5_qj=s5_qj, s5_a=s5_a,
        s5_d=jnp.tile(p["s5_d"][l].reshape(S5_GROUPS, 1, S5_GROUP), (1, 1, S5_CHUNK)),
        w_glu=p["s5_w_glu"][l].astype(BF16), b_glu=p["s5_b_glu"][l][None, :],
    )


def kernel(x_prompt, x_sample, cache_na_k, cache_na_v, state_s5, cache_gqa_k, cache_gqa_v, cache_mla_ckv,
           cache_mla_krope, c, c_ctx, norm1_g, norm2_g, ada_w, ada_b, w_in, na_qn, na_kn, na_rpb, s5_lam_re,
           s5_lam_im, s5_log_dt, s5_b_re, s5_b_im, s5_c_re, s5_c_im, s5_d, s5_w_glu, s5_b_glu, gq_qn, gq_kn,
           mla_qa_g, mla_kva_g, mla_w_uq, mla_w_ukv, mla_qn, mla_kn, w_out, ffn_w_up, ffn_conv_w, ffn_conv_b,
           ffn_w_down):
    p = dict(norm1_g=norm1_g, norm2_g=norm2_g, w_in=w_in, na_qn=na_qn, na_kn=na_kn, na_rpb=na_rpb,
             s5_lam_re=s5_lam_re, s5_lam_im=s5_lam_im, s5_log_dt=s5_log_dt, s5_b_re=s5_b_re, s5_b_im=s5_b_im,
             s5_c_re=s5_c_re, s5_c_im=s5_c_im, s5_d=s5_d, s5_w_glu=s5_w_glu, s5_b_glu=s5_b_glu,
             gq_qn=gq_qn, gq_kn=gq_kn, mla_qa_g=mla_qa_g, mla_kva_g=mla_kva_g, mla_w_uq=mla_w_uq,
             mla_w_ukv=mla_w_ukv, mla_qn=mla_qn, mla_kn=mla_kn, w_out=w_out, ffn_w_up=ffn_w_up,
             ffn_conv_w=ffn_conv_w, ffn_conv_b=ffn_conv_b, ffn_w_down=ffn_w_down)
    nb, nd = BATCH, DEC_BATCH
    cvec = jnp.concatenate([c_ctx[None, :], c, jnp.zeros((8 - 1 - nd, D_MODEL), F32)], axis=0)
    mods_all = _ada_mods(cvec, ada_w, ada_b).reshape(DEPTH, 8, 6, D_MODEL)
    gq_tabs, mla_tabs = _rope_tables()
    fw = dict(w_up=ffn_w_up.astype(BF16), conv_w=ffn_conv_w, conv_b=ffn_conv_b[:, None, :],
              w_down=ffn_w_down.astype(BF16))

    xc = x_prompt.reshape(nb * SEQ, D_MODEL)
    xl = x_sample.reshape(nd * DEC_SEQ, D_MODEL)
    zero_state = jnp.zeros((S5_GROUPS, nb, 4 * S5_N), F32)
    caches = []
    s5_states = []
    for l in range(DEPTH):
        lw = _layer_weights(l, p)
        mods = mods_all[l]
        s5_m = _s5_prep(lw)

        zna, zs5, zgq, zm = _inproj(xc, mods, lw["norm1_g"], lw["w_in"], 0, nb * SEQ)
        (ona, ogq, omla), caches = _ctx_attn(zna, zgq, zm, lw, l, caches)
        ys5a, ys5b, fin = _s5_mixer(zs5, s5_m, lw, zero_state, nb, S5_CTX_BLOCK)
        xc = _outproj(ona, ys5a, ys5b, ogq, omla, xc, mods, lw, 0, nb * SEQ)
        xc = _ffn(xc, mods, lw, fw, l, 0, nb * SEQ, SEQ)
        s5_states.append(fin.reshape(S5_GROUPS, nb, 2, 2, S5_N).transpose(1, 2, 3, 0, 4))

        zna, zs5, zgq, zm = _inproj(xl, mods, lw["norm1_g"], lw["w_in"], 1, DEC_SEQ)
        ona = _na_lat(zna, cache_na_k, cache_na_v, l, lw["na_bias"], lw)
        ogq = _gq_lat(zgq, cache_gqa_k, cache_gqa_v, l, gq_tabs, lw)
        omla = _mla_lat(zm, cache_mla_ckv, cache_mla_krope, l, mla_tabs, lw)
        h0 = state_s5[:, l].astype(F32).transpose(3, 0, 1, 2, 4).reshape(S5_GROUPS, nd, 4 * S5_N)
        ys5a, ys5b, _ = _s5_mixer(zs5, s5_m, lw, h0, nd, nd)
        xl = _outproj(ona, ys5a, ys5b, ogq, omla, xl, mods, lw, 1, DEC_SEQ)
        xl = _ffn(xl, mods, lw, fw, l, 1, DEC_SEQ, DEC_SEQ)

    return (xc.reshape(nb, SEQ, D_MODEL), xl.reshape(nd, DEC_SEQ, D_MODEL), caches[0], caches[1],
            jnp.stack(s5_states, axis=1), caches[2], caches[3], caches[4], caches[5])
```

```python
import functools

import numpy as np
import jax
import jax.numpy as jnp
from jax import lax
from jax.experimental import pallas as pl
from jax.experimental.pallas import tpu as pltpu

F32 = jnp.float32
BF16 = jnp.bfloat16

D_MODEL = 1024
BATCH = 32
SEQ = 256
DEPTH = 2
DEC_BATCH = 4
DEC_SEQ = 1024
PAST_LEN = 512
GRID_W = 64
GRID_ROWS = DEC_SEQ // GRID_W
MIX = D_MODEL // 4
DH = 64
NA_HEADS = MIX // DH
NA_WIN_R = 8
NA_WIN_C = 16
S5_GROUP = 16
S5_GROUPS = MIX // S5_GROUP
S5_N = 64
GQ_HEADS = MIX // DH
GQ_KV = GQ_HEADS // 2
MLA_HEADS = MIX // DH
MLA_NOPE = 64
MLA_ROPE = 32
MLA_QK = MLA_NOPE + MLA_ROPE
MLA_QLORA = (3 * D_MODEL) // 16
MLA_KVLORA = D_MODEL // 8
D_FF = 128 * ((8 * D_MODEL // 3 + 127) // 128)
ROPE_BASE = 10000.0
EPS = 1e-6
NEG = -1e30

LANES = 128
MXU_DIM = 256
VMEM_LIMIT = 48 * 1024 * 1024
MXU_DEN_MIN_KEYS = 1024

Z_NA = 3 * MIX
Z_S5 = MIX
Z_GQ = 2 * MIX
Z_MLA = 3 * LANES
Z_ALL = Z_NA + Z_S5 + Z_GQ + Z_MLA

S5_CHUNK = 16
S5_ROW = S5_CHUNK * S5_GROUP
S5_CTX_BLOCK = 16

PROJ_ROWS = 1024
FF_CHUNK = 256
FF_STEPS = D_FF // FF_CHUNK


def _cparams(sem):
    return pltpu.CompilerParams(dimension_semantics=sem, vmem_limit_bytes=VMEM_LIMIT)


def _dot(a, b):
    return jnp.dot(a, b, preferred_element_type=F32)


def _dot_nt(a, b):
    return lax.dot_general(a, b, (((1,), (1,)), ((), ())), preferred_element_type=F32)


def _dot_nt_f32(a, b):
    def split(x):
        hi = x.astype(BF16)
        r1 = x - hi.astype(F32)
        mid = r1.astype(BF16)
        return hi, mid, (r1 - mid.astype(F32)).astype(BF16)

    pa, pb = split(a), split(b)
    acc = None
    for i in range(3):
        for j in range(3 - i):
            term = _dot_nt(pa[i], pb[j])
            acc = term if acc is None else acc + term
    return acc


def _rms_rows(x, gain, denom=None, valid=None):
    xx = x * x
    if valid is not None and valid != x.shape[-1]:
        lane = lax.broadcasted_iota(jnp.int32, x.shape, 1)
        xx = jnp.where(lane < valid, xx, 0.0)
    denom = denom or (valid or x.shape[-1])
    ss = jnp.sum(xx, axis=-1, keepdims=True)
    return x * lax.rsqrt(ss / denom + EPS) * gain


def _seg_rms(x, seg, denom, gain):
    rows, width = x.shape
    if seg % LANES == 0:
        parts = []
        for s in range(width // seg):
            xs = x[:, s * seg:(s + 1) * seg]
            ss = jnp.sum(xs * xs, axis=-1, keepdims=True)
            parts.append(xs * lax.rsqrt(ss / denom + EPS))
        return jnp.concatenate(parts, axis=-1) * gain
    same_seg = (lax.broadcasted_iota(jnp.int32, (width, width), 0) // seg
                == lax.broadcasted_iota(jnp.int32, (width, width), 1) // seg)
    ones = jnp.where(same_seg, 1.0, 0.0).astype(BF16)
    xx = x * x
    hi = xx.astype(BF16)
    lo = (xx - hi.astype(F32)).astype(BF16)
    ss = _dot(hi, ones) + _dot(lo, ones)
    return x * lax.rsqrt(ss / denom + EPS) * gain


def _rope(x, cos, sin_a, sin_b, half):
    tiles = []
    for t in range(x.shape[-1] // LANES):
        xt = x[:, t * LANES:(t + 1) * LANES]
        up = pltpu.roll(xt, LANES - half, axis=1)
        dn = pltpu.roll(xt, half, axis=1)
        tiles.append(xt * cos + up * sin_a + dn * sin_b)
    return tiles[0] if len(tiles) == 1 else jnp.concatenate(tiles, axis=-1)


def _packed_attn(q, parts, nseg, scale, biases=None):
    tq, width = q.shape
    seg_id = lax.broadcasted_iota(jnp.int32, q.shape, 1) // DH
    lhs = jnp.concatenate([jnp.where(seg_id == h, q, 0.0) for h in range(nseg)], axis=0).astype(BF16)
    scores = []
    for i, (kb, _) in enumerate(parts):
        s = _dot_nt(lhs, kb) * scale
        if biases is not None and biases[i] is not None:
            s = s + biases[i]
        scores.append(s)
    m = scores[0].max(axis=-1, keepdims=True)
    for s in scores[1:]:
        m = jnp.maximum(m, s.max(axis=-1, keepdims=True))
    mxu_den = sum(kb.shape[0] for kb, _ in parts) >= MXU_DEN_MIN_KEYS
    den = None
    pv = None
    for s, (_, vb) in zip(scores, parts):
        p = jnp.exp(s - m)
        pb = p.astype(BF16)
        ps = _dot(pb, jnp.ones(vb.shape, BF16)) if mxu_den else p.sum(axis=-1, keepdims=True)
        den = ps if den is None else den + ps
        c = _dot(pb, vb)
        pv = c if pv is None else pv + c
    pv = pv / den
    out = jnp.zeros((tq, width), F32)
    for h in range(nseg):
        out = out + jnp.where(seg_id == h, pv[h * tq:(h + 1) * tq], 0.0)
    return out


def _mla_attn(q, parts, scale):
    tq = q.shape[0]
    vseg = None
    out = jnp.zeros((tq, MIX), F32)
    for h in range(MLA_HEADS):
        qh = q[:, h * LANES:(h + 1) * LANES].astype(BF16)
        scores = [_dot_nt(qh, kb[:, h * LANES:(h + 1) * LANES]) * scale for kb, _ in parts]
        m = scores[0].max(axis=-1, keepdims=True)
        for s in scores[1:]:
            m = jnp.maximum(m, s.max(axis=-1, keepdims=True))
        den = None
        pv = None
        for s, (_, vb) in zip(scores, parts):
            p = jnp.exp(s - m)
            ps = p.sum(axis=-1, keepdims=True)
            den = ps if den is None else den + ps
            vseg = lax.broadcasted_iota(jnp.int32, vb.shape, 1) // DH
            c = _dot(p.astype(BF16), jnp.where(vseg == h, vb, jnp.zeros_like(vb)))
            pv = c if pv is None else pv + c
        out = out + pv / den
    return out


def _mla_qkv(zm, qa_g, kva_g, mqn_g, mkn_g, wuq, wuk, wuv):
    ckv = _rms_rows(zm[:, 0:LANES], kva_g)
    col3 = zm[:, 2 * LANES:3 * LANES]
    lane = lax.broadcasted_iota(jnp.int32, col3.shape, 1)
    kr_placed = jnp.where((lane >= MLA_NOPE) & (lane < MLA_QK), col3, 0.0)
    cq = _rms_rows(zm[:, LANES:3 * LANES], qa_g, valid=MLA_QLORA)
    q = _seg_rms(_dot(cq.astype(BF16), wuq), LANES, MLA_QK, mqn_g)
    k, v = _mla_kv(ckv, kr_placed, mkn_g, wuk, wuv)
    return q, k, v, ckv, col3


def _mla_kv(ckv, kr_placed, mkn_g, wuk, wuv):
    cb = ckv.astype(BF16)
    kf = _dot(cb, wuk) + jnp.concatenate([kr_placed] * MLA_HEADS, axis=-1)
    return _seg_rms(kf, LANES, MLA_QK, mkn_g), _dot(cb, wuv)


def _ada_kernel(c_ref, w_ref, b_ref, o_ref):
    cv = c_ref[...]
    s = cv * jax.nn.sigmoid(cv)
    o_ref[0] = _dot(s.astype(BF16), w_ref[0].astype(BF16)) + b_ref[0]


def _ada_mods(cvec, ada_w, ada_b):
    tn = 1536
    n = ada_w.shape[-1]
    return pl.pallas_call(
        _ada_kernel,
        grid=(DEPTH, n // tn),
        in_specs=[pl.BlockSpec((8, D_MODEL), lambda l, j: (0, 0)),
                  pl.BlockSpec((1, D_MODEL, tn), lambda l, j: (l, 0, j)),
                  pl.BlockSpec((1, 1, tn), lambda l, j: (l, 0, j))],
        out_specs=pl.BlockSpec((1, 8, tn), lambda l, j: (l, 0, j)),
        out_shape=jax.ShapeDtypeStruct((DEPTH, 8, n), F32),
        compiler_params=_cparams(("arbitrary", "arbitrary")),
        name="ada_mods",
    )(cvec, ada_w, ada_b.reshape(DEPTH, 1, n))


def _inproj_kernel(x_ref, mod_ref, g_ref, w_ref, ona_ref, os5_ref, ogq_ref, omla_ref):
    x = x_ref[...]
    y = x * lax.rsqrt(jnp.mean(x * x, axis=-1, keepdims=True) + EPS) * g_ref[...]
    h = y * (1.0 + mod_ref[0, 1:2, :]) + mod_ref[0, 0:1, :]
    z = _dot(h.astype(BF16), w_ref[...])
    ona_ref[...] = z[:, 0:Z_NA]
    os5_ref[...] = z[:, Z_NA:Z_NA + Z_S5]
    ogq_ref[...] = z[:, Z_NA + Z_S5:Z_NA + Z_S5 + Z_GQ]
    omla_ref[...] = z[:, Z_NA + Z_S5 + Z_GQ:Z_ALL]


def _inproj(x, mods, g, w, mod_base, rows_per_mod):
    n = x.shape[0]
    tm = PROJ_ROWS
    row = lambda i: (mod_base + (i * tm) // rows_per_mod, 0, 0)
    widths = (Z_NA, Z_S5, Z_GQ, Z_MLA)
    return pl.pallas_call(
        _inproj_kernel,
        grid=(n // tm,),
        in_specs=[pl.BlockSpec((tm, D_MODEL), lambda i: (i, 0)),
                  pl.BlockSpec((1, 6, D_MODEL), row),
                  pl.BlockSpec((1, D_MODEL), lambda i: (0, 0)),
                  pl.BlockSpec((D_MODEL, Z_ALL), lambda i: (0, 0))],
        out_specs=[pl.BlockSpec((tm, wd), lambda i: (i, 0)) for wd in widths],
        out_shape=[jax.ShapeDtypeStruct((n, wd), F32) for wd in widths],
        compiler_params=_cparams(("arbitrary",)),
        name="inproj",
    )(x, mods, g, w)


CTX_SEQS = 2


def _store_heads(ref, i, x, heads):
    for h in range(heads):
        ref[i, 0, h] = x[:, h * DH:(h + 1) * DH]


def _ctx_attn_kernel(zna_ref, zgq_ref, zm_ref, naq_g, nak_g, gqq_g, gqk_g, qa_g, kva_g, mqn_g, mkn_g,
                     wuq_ref, wuk_ref, wuv_ref, *refs):
    ona_ref, ogq_ref, omla_ref, nk_ref, nv_ref, gk_ref, gv_ref, ckv_ref, kr_ref = refs[-9:]
    for ref in refs[-6:]:
        for later in range(1, ref.shape[1]):
            ref[:, later] = jnp.zeros((CTX_SEQS,) + ref.shape[2:], F32)
    for i in range(CTX_SEQS):
        rows = slice(i * SEQ, (i + 1) * SEQ)
        zna = zna_ref[rows, :]
        q = _seg_rms(zna[:, 0:MIX], DH, DH, naq_g[...])
        k = _seg_rms(zna[:, MIX:2 * MIX], DH, DH, nak_g[...])
        v = zna[:, 2 * MIX:3 * MIX]
        _store_heads(nk_ref, i, k, NA_HEADS)
        _store_heads(nv_ref, i, v, NA_HEADS)
        ona_ref[rows, :] = _packed_attn(q, [(k.astype(BF16), v.astype(BF16))], NA_HEADS, DH ** -0.5)

        zgq = zgq_ref[rows, :]
        gq = _seg_rms(zgq[:, 0:MIX], DH, DH, gqq_g[...])
        gk = _seg_rms(zgq[:, MIX:MIX + LANES], DH, DH, gqk_g[...])
        gv = zgq[:, MIX + LANES:2 * MIX]
        _store_heads(gk_ref, i, gk, GQ_KV)
        _store_heads(gv_ref, i, gv, GQ_KV)
        kv = [(gk.astype(BF16), gv.astype(BF16))]
        ogq_ref[rows, :] = jnp.concatenate(
            [_packed_attn(gq[:, r * LANES:(r + 1) * LANES], kv, GQ_KV, DH ** -0.5) for r in range(2)], axis=-1)

        mq, mk, mv, ckv, col3 = _mla_qkv(zm_ref[rows, :], qa_g[...], kva_g[...], mqn_g[...], mkn_g[...],
                                         wuq_ref[...], wuk_ref[...], wuv_ref[...])
        ckv_ref[i, 0] = ckv
        kr_ref[i, 0] = pltpu.roll(col3, LANES - MLA_NOPE, axis=1)[:, 0:MLA_ROPE]
        omla_ref[rows, :] = _mla_attn(mq, [(mk.astype(BF16), mv.astype(BF16))], MLA_QK ** -0.5)


def _full(shape):
    nd = len(shape)
    return pl.BlockSpec(shape, lambda *a: (0,) * nd)


CACHE_SHAPES = ((NA_HEADS, SEQ, DH), (NA_HEADS, SEQ, DH), (GQ_KV, SEQ, DH), (GQ_KV, SEQ, DH),
                (SEQ, MLA_KVLORA), (SEQ, MLA_ROPE))


def _ctx_attn(zna, zgq, zm, lw, layer, caches):
    n = zna.shape[0]
    t = CTX_SEQS * SEQ
    rows = lambda wd: pl.BlockSpec((t, wd), lambda b: (b, 0))
    if caches:
        cache_spec = lambda shp: pl.BlockSpec((CTX_SEQS, 1) + shp, lambda b: (b, layer) + (0,) * len(shp))
    else:
        assert layer == 0
        cache_spec = lambda shp: pl.BlockSpec((CTX_SEQS, DEPTH) + shp, lambda b: (b, 0) + (0,) * len(shp))
    gains = [lw["naq_g"], lw["nak_g"], lw["gqq_g"], lw["gqk_g"], lw["qa_g"], lw["kva_g"], lw["mqn_g"], lw["mkn_g"]]
    weights = [lw["wuq"], lw["wuk"], lw["wuv"]]
    n_in = 3 + len(gains) + len(weights)
    outs = pl.pallas_call(
        _ctx_attn_kernel,
        grid=(n // t,),
        in_specs=[rows(Z_NA), rows(Z_GQ), rows(Z_MLA)] + [_full(a.shape) for a in gains + weights]
                 + [pl.BlockSpec(memory_space=pl.ANY)] * len(caches),
        out_specs=[rows(MIX)] * 3 + [cache_spec(shp) for shp in CACHE_SHAPES],
        out_shape=[jax.ShapeDtypeStruct((n, MIX), F32)] * 3
                  + [jax.ShapeDtypeStruct((n // SEQ, DEPTH) + shp, F32) for shp in CACHE_SHAPES],
        input_output_aliases={n_in + i: 3 + i for i in range(len(caches))},
        compiler_params=_cparams(("arbitrary",)),
        name="ctx_attn",
    )(zna, zgq, zm, *gains, *weights, *caches)
    return outs[:3], list(outs[3:])


NA_ROWS_PER_STEP = 8


def _na_window_start(r):
    return jnp.clip(r - NA_WIN_R // 2, 0, GRID_ROWS - NA_WIN_R)


def _pack_heads(ref):
    return jnp.concatenate([ref[0, 0, h] for h in range(ref.shape[2])], axis=-1)


def _na_lat_kernel(zna_ref, kc_ref, vc_ref, bias_ref, naq_g, nak_g, o_ref, k_scr, v_scr, kc_scr, vc_scr):
    r = pl.program_id(1)

    @pl.when(r == 0)
    def _():
        k_scr[...] = _seg_rms(zna_ref[:, MIX:2 * MIX], DH, DH, nak_g[...]).astype(BF16)
        v_scr[...] = zna_ref[:, 2 * MIX:3 * MIX].astype(BF16)
        kc_scr[...] = _pack_heads(kc_ref).astype(BF16)
        vc_scr[...] = _pack_heads(vc_ref).astype(BF16)

    span = NA_WIN_R * GRID_W
    for sub in range(NA_ROWS_PER_STEP):
        row = r * NA_ROWS_PER_STEP + sub
        q = _seg_rms(zna_ref[pl.ds(pl.multiple_of(row * GRID_W, GRID_W), GRID_W), 0:MIX], DH, DH, naq_g[...])
        start = pl.multiple_of(_na_window_start(row) * GRID_W, GRID_W)
        parts = [(k_scr[pl.ds(start, span), :], v_scr[pl.ds(start, span), :]),
                 (kc_scr[...], vc_scr[...])]
        d0 = _na_window_start(row) - row + NA_WIN_R - 1
        bias = jnp.concatenate([bias_ref[d0 + 2 * i] for i in range(NA_WIN_R // 2)], axis=-1)
        o_ref[sub * GRID_W:(sub + 1) * GRID_W, :] = _packed_attn(q, parts, NA_HEADS, DH ** -0.5,
                                                                 biases=[bias, None])


def _na_lat(zna, kc, vc, layer, bias, lw):
    nb = DEC_BATCH
    cache = pl.BlockSpec((1, 1, NA_HEADS, PAST_LEN, DH), lambda b, r: (b, layer, 0, 0, 0))
    return pl.pallas_call(
        _na_lat_kernel,
        grid=(nb, GRID_ROWS // NA_ROWS_PER_STEP),
        in_specs=[pl.BlockSpec((DEC_SEQ, Z_NA), lambda b, r: (b, 0)),
                  cache, cache,
                  _full(bias.shape),
                  _full(lw["naq_g"].shape), _full(lw["nak_g"].shape)],
        out_specs=pl.BlockSpec((NA_ROWS_PER_STEP * GRID_W, MIX),
                               lambda b, r: (b * (GRID_ROWS // NA_ROWS_PER_STEP) + r, 0)),
        out_shape=jax.ShapeDtypeStruct((nb * DEC_SEQ, MIX), F32),
        scratch_shapes=[pltpu.VMEM((DEC_SEQ, MIX), BF16), pltpu.VMEM((DEC_SEQ, MIX), BF16),
                        pltpu.VMEM((PAST_LEN, MIX), BF16), pltpu.VMEM((PAST_LEN, MIX), BF16)],
        compiler_params=_cparams(("arbitrary", "arbitrary")),
        name="na_latent",
    )(zna, kc, vc, bias, lw["naq_g"], lw["nak_g"])


LAT_QB = 256
LAT_KEYS = DEC_SEQ + PAST_LEN


def _gq_lat_kernel(zgq_ref, kc_ref, vc_ref, cos_ref, sa_ref, sb_ref, gqq_g, gqk_g, o_ref, k_scr, v_scr):
    j = pl.program_id(1)

    @pl.when(j == 0)
    def _():
        k = _seg_rms(zgq_ref[:, MIX:MIX + LANES], DH, DH, gqk_g[...])
        k = _rope(k, cos_ref[...], sa_ref[...], sb_ref[...], DH // 4)
        k_scr[0:DEC_SEQ, :] = k.astype(BF16)
        k_scr[DEC_SEQ:LAT_KEYS, :] = _pack_heads(kc_ref).astype(BF16)
        v_scr[0:DEC_SEQ, :] = zgq_ref[:, MIX + LANES:2 * MIX].astype(BF16)
        v_scr[DEC_SEQ:LAT_KEYS, :] = _pack_heads(vc_ref).astype(BF16)

    rows = pl.ds(pl.multiple_of(j * LAT_QB, LAT_QB), LAT_QB)
    q = _seg_rms(zgq_ref[rows, 0:MIX], DH, DH, gqq_g[...])
    q = _rope(q, cos_ref[rows, :], sa_ref[rows, :], sb_ref[rows, :], DH // 4)
    kv = [(k_scr[...], v_scr[...])]
    o_ref[...] = jnp.concatenate(
        [_packed_attn(q[:, r * LANES:(r + 1) * LANES], kv, GQ_KV, DH ** -0.5) for r in range(2)], axis=-1)


def _gq_lat(zgq, kc, vc, layer, tabs, lw):
    nb = DEC_BATCH
    nq = DEC_SEQ // LAT_QB
    cache = pl.BlockSpec((1, 1, GQ_KV, PAST_LEN, DH), lambda b, j: (b, layer, 0, 0, 0))
    return pl.pallas_call(
        _gq_lat_kernel,
        grid=(nb, nq),
        in_specs=[pl.BlockSpec((DEC_SEQ, Z_GQ), lambda b, j: (b, 0)), cache, cache]
                 + [_full(t.shape) for t in tabs] + [_full(lw["gqq_g"].shape), _full(lw["gqk_g"].shape)],
        out_specs=pl.BlockSpec((LAT_QB, MIX), lambda b, j: (b * nq + j, 0)),
        out_shape=jax.ShapeDtypeStruct((nb * DEC_SEQ, MIX), F32),
        scratch_shapes=[pltpu.VMEM((LAT_KEYS, LANES), BF16), pltpu.VMEM((LAT_KEYS, LANES), BF16)],
        compiler_params=_cparams(("arbitrary", "arbitrary")),
        name="gq_latent",
    )(zgq, kc, vc, *tabs, lw["gqq_g"], lw["gqk_g"])


def _mla_lat_kernel(zm_ref, ckvc_ref, krc_ref, cos_ref, sa_ref, sb_ref, qa_g, kva_g, mqn_g, mkn_g,
                    wuq_ref, wuk_ref, wuv_ref, o_ref, k_scr, v_scr):
    j = pl.program_id(1)
    half = MLA_ROPE // 4

    @pl.when(j == 0)
    def _():
        zm = zm_ref[...]
        ckv = _rms_rows(zm[:, 0:LANES], kva_g[...])
        col3 = zm[:, 2 * LANES:3 * LANES]
        lane = lax.broadcasted_iota(jnp.int32, col3.shape, 1)
        kr_placed = jnp.where((lane >= MLA_NOPE) & (lane < MLA_QK), col3, 0.0)
        k, v = _mla_kv(ckv, kr_placed, mkn_g[...], wuk_ref[...], wuv_ref[...])
        k = _rope(k, cos_ref[...], sa_ref[...], sb_ref[...], half)
        k_scr[0:DEC_SEQ, :] = k.astype(BF16)
        v_scr[0:DEC_SEQ, :] = v.astype(BF16)
        krc = jnp.concatenate([jnp.zeros((PAST_LEN, MLA_NOPE), F32), krc_ref[0, 0],
                               jnp.zeros((PAST_LEN, LANES - MLA_QK), F32)], axis=-1)
        kc, vc = _mla_kv(ckvc_ref[0, 0], krc, mkn_g[...], wuk_ref[...], wuv_ref[...])
        k_scr[DEC_SEQ:LAT_KEYS, :] = kc.astype(BF16)
        v_scr[DEC_SEQ:LAT_KEYS, :] = vc.astype(BF16)

    rows = pl.ds(pl.multiple_of(j * LAT_QB, LAT_QB), LAT_QB)
    cq = _rms_rows(zm_ref[rows, LANES:3 * LANES], qa_g[...], valid=MLA_QLORA)
    q = _seg_rms(_dot(cq.astype(BF16), wuq_ref[...]), LANES, MLA_QK, mqn_g[...])
    q = _rope(q, cos_ref[rows, :], sa_ref[rows, :], sb_ref[rows, :], half)
    o_ref[...] = _mla_attn(q, [(k_scr[...], v_scr[...])], MLA_QK ** -0.5)


def _mla_lat(zm, ckvc, krc, layer, tabs, lw):
    nb = DEC_BATCH
    nq = DEC_SEQ // LAT_QB
    small = [lw["qa_g"], lw["kva_g"], lw["mqn_g"], lw["mkn_g"], lw["wuq"], lw["wuk"], lw["wuv"]]
    return pl.pallas_call(
        _mla_lat_kernel,
        grid=(nb, nq),
        in_specs=[pl.BlockSpec((DEC_SEQ, Z_MLA), lambda b, j: (b, 0)),
                  pl.BlockSpec((1, 1, PAST_LEN, MLA_KVLORA), lambda b, j: (b, layer, 0, 0)),
                  pl.BlockSpec((1, 1, PAST_LEN, MLA_ROPE), lambda b, j: (b, layer, 0, 0))]
                 + [_full(t.shape) for t in tabs] + [_full(a.shape) for a in small],
        out_specs=pl.BlockSpec((LAT_QB, MIX), lambda b, j: (b * nq + j, 0)),
        out_shape=jax.ShapeDtypeStruct((nb * DEC_SEQ, MIX), F32),
        scratch_shapes=[pltpu.VMEM((LAT_KEYS, MLA_HEADS * LANES), BF16), pltpu.VMEM((LAT_KEYS, MIX), BF16)],
        compiler_params=_cparams(("arbitrary", "arbitrary")),
        name="mla_latent",
    )(zm, ckvc, krc, *tabs, *small)


def _s5_prep_kernel(bb_ref, qj_ref, m_ref):
    bb, qj = bb_ref[0], qj_ref[0]
    kt = _dot_nt_f32(bb[:, 0:LANES], qj[:, 0:LANES])
    kr = _dot_nt_f32(bb[:, LANES:2 * LANES], qj[:, LANES:2 * LANES])
    lane = lax.broadcasted_iota(jnp.int32, kt.shape, 1)
    blocks = []
    for s_pos in range(S5_CHUNK):
        sh_f = S5_GROUP * s_pos
        sh_b = S5_GROUP * (S5_CHUNK - 1 - s_pos)
        f = jnp.where(lane >= sh_f, pltpu.roll(kt, sh_f, axis=1), 0.0) if sh_f else kt
        b = jnp.where(lane < S5_ROW - sh_b, pltpu.roll(kr, S5_ROW - sh_b, axis=1), 0.0) if sh_b else kr
        blocks.append(f + b)
    m_ref[0] = jnp.concatenate(blocks, axis=0).astype(BF16)


def _s5_prep(lw):
    blk = lambda *shape: pl.BlockSpec((1,) + shape, lambda i: (i, 0, 0))
    return pl.pallas_call(
        _s5_prep_kernel,
        grid=(S5_GROUPS,),
        in_specs=[blk(S5_GROUP, S5_ROW), blk(S5_ROW, S5_ROW)],
        out_specs=blk(S5_ROW, S5_ROW),
        out_shape=jax.ShapeDtypeStruct((S5_GROUPS, S5_ROW, S5_ROW), BF16),
        compiler_params=_cparams(("arbitrary",)),
        name="s5_prep",
    )(lw["s5_bb"], lw["s5_qj"])


S5_TILE_BLOCKS = LANES // S5_GROUP


S5_INTERLEAVE = 4


def _block_transpose(xs, lane_blk):
    n = len(xs)
    d = n // 2
    while d >= 1:
        hi = (lane_blk & d) != 0
        nxt = list(xs)
        for i in range(n):
            if i & d:
                continue
            lo_arr, hi_arr = xs[i], xs[i + d]
            nxt[i] = jnp.where(hi, pltpu.roll(hi_arr, d * S5_GROUP, axis=1), lo_arr)
            nxt[i + d] = jnp.where(hi, hi_arr, pltpu.roll(lo_arr, LANES - d * S5_GROUP, axis=1))
        xs = nxt
        d //= 2
    return xs


def _s5_kernel(nb, nch, ua_ref, ub_ref, m_ref, ws_ref, wct_ref, a_ref, d_ref, h0_ref, ya_ref, yb_ref, fin_ref,
               ug_scr, yg_scr, *tmp_scr):
    rows = nb * nch
    per = S5_TILE_BLOCKS
    lane_blk = lax.broadcasted_iota(jnp.int32, (rows, LANES), 1) // S5_GROUP
    u_refs = (ua_ref, ub_ref)
    y_refs = (ya_ref, yb_ref)
    tile = lambda t: slice(t * LANES, (t + 1) * LANES)

    for half in range(S5_GROUPS // per):
        for t in range(S5_CHUNK // per):
            pieces = [u_refs[half][pl.ds(t * per + i, rows, stride=S5_CHUNK), :] for i in range(per)]
            for j, arr in enumerate(_block_transpose(pieces, lane_blk)):
                ug_scr[half * per + j, :, tile(t)] = arr

    def group_set(i, carry):
        gs = [i * S5_INTERLEAVE + n for n in range(S5_INTERLEAVE)]
        sets = [tmp_scr[6 * n:6 * n + 6] for n in range(S5_INTERLEAVE)]
        state = []
        for g, (sf_scr, sb_scr, sfs_scr, sbs_scr, _, _) in zip(gs, sets):
            s = _dot(ug_scr[g].astype(BF16), ws_ref[g])
            sf_scr[...] = s[:, 0:LANES]
            sb_scr[...] = s[:, LANES:2 * LANES]
            sfs_scr[...] = pltpu.roll(s[:, 0:LANES], S5_N, axis=1)
            sbs_scr[...] = pltpu.roll(s[:, LANES:2 * LANES], S5_N, axis=1)
            h0 = h0_ref[g]
            h_f = h0[:, 0:LANES]
            h_b = h0[:, LANES:2 * LANES]
            state.append([h_f, pltpu.roll(h_f, S5_N, axis=1), h_b, pltpu.roll(h_b, S5_N, axis=1)])
        for k in range(nch):
            rows_f = pl.ds(k, nb, stride=nch)
            rows_b = pl.ds(nch - 1 - k, nb, stride=nch)
            for n, (g, (sf_scr, sb_scr, sfs_scr, sbs_scr, hf_scr, hb_scr)) in enumerate(zip(gs, sets)):
                a = a_ref[g]
                h_f, g_f, h_b, g_b = state[n]
                hf_scr[rows_f, :] = h_f
                hb_scr[rows_b, :] = h_b
                state[n] = [a[0:1] * h_f + a[1:2] * g_f + sf_scr[rows_f, :],
                            a[0:1] * g_f + a[2:3] * h_f + sfs_scr[rows_f, :],
                            a[3:4] * h_b + a[4:5] * g_b + sb_scr[rows_b, :],
                            a[3:4] * g_b + a[5:6] * h_b + sbs_scr[rows_b, :]]
        for n, (g, (_, _, _, _, hf_scr, hb_scr)) in enumerate(zip(gs, sets)):
            fin_ref[g] = jnp.concatenate([state[n][0], state[n][2]], axis=-1)
            hp = jnp.concatenate([hf_scr[...], hb_scr[...]], axis=-1).astype(BF16)
            x = ug_scr[g]
            yg_scr[g] = _dot(x.astype(BF16), m_ref[g]) + _dot_nt(hp, wct_ref[g]) + d_ref[g] * x
        return carry

    lax.fori_loop(0, S5_GROUPS // S5_INTERLEAVE, group_set, 0)

    for half in range(S5_GROUPS // per):
        for t in range(S5_CHUNK // per):
            pieces = [yg_scr[half * per + j, :, tile(t)] for j in range(per)]
            for i, arr in enumerate(_block_transpose(pieces, lane_blk)):
                y_refs[half][pl.ds(t * per + i, rows, stride=S5_CHUNK), :] = arr


def _s5_mixer(u, m, lw, h0, nb, nb_step):
    n = u.shape[0]
    nch = n // nb // S5_CHUNK
    tok = nb_step * nch * S5_CHUNK
    rows = nb_step * nch
    g = S5_GROUPS
    res = lambda a: pl.BlockSpec(a.shape, lambda i: (0, 0, 0))
    state = pl.BlockSpec((g, nb_step, S5_ROW), lambda i: (0, i, 0))
    half = lambda j: pl.BlockSpec((tok, LANES), lambda i: (i, j))
    ya, yb, fin = pl.pallas_call(
        functools.partial(_s5_kernel, nb_step, nch),
        grid=(nb // nb_step,),
        in_specs=[half(0), half(1), res(m), res(lw["s5_ws"]), res(lw["s5_wct"]), res(lw["s5_a"]), res(lw["s5_d"]), state],
        out_specs=[half(0), half(0), state],
        out_shape=[jax.ShapeDtypeStruct((n, LANES), F32), jax.ShapeDtypeStruct((n, LANES), F32),
                   jax.ShapeDtypeStruct((g, nb, S5_ROW), F32)],
        scratch_shapes=[pltpu.VMEM((g, rows, S5_ROW), F32), pltpu.VMEM((g, rows, S5_ROW), F32)]
                       + [pltpu.VMEM((rows, LANES), F32)] * (6 * S5_INTERLEAVE),
        compiler_params=_cparams(("arbitrary",)),
        name="s5_mixer",
    )(u, u, m, lw["s5_ws"], lw["s5_wct"], lw["s5_a"], lw["s5_d"], h0)
    return ya, yb, fin


def _outproj_kernel(ona_ref, ys5a_ref, ys5b_ref, ogq_ref, omla_ref, x_ref, mod_ref, w_ref, wglu_ref, bglu_ref,
                    o_ref):
    y = jax.nn.gelu(jnp.concatenate([ys5a_ref[...], ys5b_ref[...]], axis=-1), approximate=True)
    y = y * jax.nn.sigmoid(_dot(y.astype(BF16), wglu_ref[...]) + bglu_ref[...])
    mixed = _dot(ona_ref[...].astype(BF16), w_ref[0:MIX, :])
    mixed = mixed + _dot(y.astype(BF16), w_ref[MIX:2 * MIX, :])
    mixed = mixed + _dot(ogq_ref[...].astype(BF16), w_ref[2 * MIX:3 * MIX, :])
    mixed = mixed + _dot(omla_ref[...].astype(BF16), w_ref[3 * MIX:4 * MIX, :])
    o_ref[...] = x_ref[...] + mod_ref[0, 2:3, :] * mixed


def _outproj(ona, ys5a, ys5b, ogq, omla, x, mods, lw, mod_base, rows_per_mod):
    n = x.shape[0]
    tm = PROJ_ROWS
    row = lambda i: (mod_base + (i * tm) // rows_per_mod, 0, 0)
    part = pl.BlockSpec((tm, MIX), lambda i: (i, 0))
    half = pl.BlockSpec((tm, LANES), lambda i: (i, 0))
    return pl.pallas_call(
        _outproj_kernel,
        grid=(n // tm,),
        in_specs=[part, half, half, part, part,
                  pl.BlockSpec((tm, D_MODEL), lambda i: (i, 0)),
                  pl.BlockSpec((1, 6, D_MODEL), row),
                  _full(lw["w_out"].shape), _full(lw["w_glu"].shape), _full(lw["b_glu"].shape)],
        out_specs=pl.BlockSpec((tm, D_MODEL), lambda i: (i, 0)),
        out_shape=jax.ShapeDtypeStruct((n, D_MODEL), F32),
        compiler_params=_cparams(("arbitrary",)),
        name="outproj",
    )(ona, ys5a, ys5b, ogq, omla, x, mods, lw["w_out"], lw["w_glu"], lw["b_glu"])


FF_ROWS = 1024
FFN_VMEM_LIMIT = 58 * 1024 * 1024


def _ffn_kernel(seq, x_ref, mod_ref, g_ref, wup_ref, cw_ref, cb_ref, wd_ref, o_ref, h_scr, ug_scr, uu_scr, act_scr):
    x = x_ref[...]
    y = x * lax.rsqrt(jnp.mean(x * x, axis=-1, keepdims=True) + EPS) * g_ref[...]
    h_scr[...] = (y * (1.0 + mod_ref[0, 4:5, :]) + mod_ref[0, 3:4, :]).astype(BF16)

    pos = lax.broadcasted_iota(jnp.int32, (FF_ROWS, FF_CHUNK), 0) % seq
    first = pos == 0
    last = pos == seq - 1

    def conv(u, cw, cb):
        prev = jnp.where(first, 0.0, pltpu.roll(u, 1, axis=0))
        nxt = jnp.where(last, 0.0, pltpu.roll(u, FF_ROWS - 1, axis=0))
        return cw[0:1, :] * prev + cw[1:2, :] * u + cw[2:3, :] * nxt + cb

    def cols(j, base=0):
        return pl.ds(pl.multiple_of(base + j * FF_CHUNK, FF_CHUNK), FF_CHUNK)

    def up_proj(j, slot):
        h = h_scr[...]
        ug_scr[slot] = _dot(h, wup_ref[:, cols(j)])
        uu_scr[slot] = _dot(h, wup_ref[:, cols(j, D_FF)])

    def gate_act(j, slot):
        gate = conv(ug_scr[slot], cw_ref[:, cols(j)], cb_ref[:, cols(j)])
        up = conv(uu_scr[slot], cw_ref[:, cols(j, D_FF)], cb_ref[:, cols(j, D_FF)])
        act_scr[slot] = (gate * jax.nn.sigmoid(gate) * up).astype(BF16)

    def down_proj(j, slot):
        return _dot(act_scr[slot], wd_ref[cols(j), :])

    last_j = FF_STEPS - 1
    up_proj(0, 0)
    up_proj(1, 1)
    gate_act(0, 0)
    up_proj(2, 0)
    gate_act(1, 1)
    o_ref[...] = down_proj(0, 0)

    def body(j, carry):
        slot = j % 2
        o_ref[...] += down_proj(j, slot)
        gate_act(j + 1, 1 - slot)
        up_proj(j + 2, slot)
        return carry

    lax.fori_loop(1, last_j - 1, body, 0)
    gate_act(last_j, last_j % 2)
    o_ref[...] += down_proj(last_j - 1, (last_j - 1) % 2)
    o_ref[...] = x_ref[...] + mod_ref[0, 5:6, :] * (o_ref[...] + down_proj(last_j, last_j % 2))


def _ffn(x, mods, lw, fw, layer, mod_base, rows_per_mod, seq):
    n = x.shape[0]
    row = lambda i: (mod_base + (i * FF_ROWS) // rows_per_mod, 0, 0)
    resident = lambda a: pl.BlockSpec((None,) + a.shape[1:], lambda i: (layer,) + (0,) * (a.ndim - 1),
                                      pipeline_mode=pl.Buffered(1))
    return pl.pallas_call(
        functools.partial(_ffn_kernel, seq),
        grid=(n // FF_ROWS,),
        in_specs=[pl.BlockSpec((FF_ROWS, D_MODEL), lambda i: (i, 0)),
                  pl.BlockSpec((1, 6, D_MODEL), row),
                  pl.BlockSpec((1, D_MODEL), lambda i: (0, 0)),
                  resident(fw["w_up"]), resident(fw["conv_w"]), resident(fw["conv_b"]), resident(fw["w_down"])],
        out_specs=pl.BlockSpec((FF_ROWS, D_MODEL), lambda i: (i, 0)),
        out_shape=jax.ShapeDtypeStruct((n, D_MODEL), F32),
        scratch_shapes=[pltpu.VMEM((FF_ROWS, D_MODEL), BF16),
                        pltpu.VMEM((2, FF_ROWS, FF_CHUNK), F32), pltpu.VMEM((2, FF_ROWS, FF_CHUNK), F32),
                        pltpu.VMEM((2, FF_ROWS, FF_CHUNK), BF16)],
        compiler_params=pltpu.CompilerParams(dimension_semantics=("arbitrary",), vmem_limit_bytes=FFN_VMEM_LIMIT),
        name="conv_ffn",
    )(x, mods, lw["norm2_g"], fw["w_up"], fw["conv_w"], fw["conv_b"], fw["w_down"])


def _rope_tables():
    pos = np.arange(DEC_SEQ)

    def ang(p, half):
        inv = ROPE_BASE ** (-np.arange(half, dtype=np.float64) / half)
        a = p.astype(np.float64)[:, None] * inv[None, :]
        return np.concatenate([a, a], axis=-1)

    def tables(dim, lanes_before, lanes_after, reps):
        a = np.concatenate([ang(pos // GRID_W, dim // 4), ang(pos % GRID_W, dim // 4)], axis=-1)
        half = dim // 4
        first = (np.arange(dim) % (2 * half)) < half
        cos = np.cos(a)
        sin_a = np.where(first[None, :], -np.sin(a), 0.0)
        sin_b = np.where(first[None, :], 0.0, np.sin(a))

        def place(t, fill):
            t = np.concatenate([np.full((DEC_SEQ, lanes_before), fill), t,
                                np.full((DEC_SEQ, lanes_after), fill)], axis=-1)
            return jnp.asarray(np.tile(t, (1, reps)), F32)

        return place(cos, 1.0), place(sin_a, 0.0), place(sin_b, 0.0)

    gq = tables(DH, 0, 0, LANES // DH)
    mla = tables(MLA_ROPE, MLA_NOPE, LANES - MLA_QK, 1)
    return gq, mla


def _na_bias_tables(rpb):
    qc = np.arange(GRID_W)[:, None]
    kc = np.arange(GRID_W)[None, :]
    col_start = np.clip(qc - NA_WIN_C // 2, 0, GRID_W - NA_WIN_C)
    valid = (kc >= col_start) & (kc < col_start + NA_WIN_C)
    d_c = np.clip(kc - qc, 1 - NA_WIN_C, NA_WIN_C - 1) + NA_WIN_C - 1
    nrel = 2 * NA_WIN_C - 1
    onehot = jnp.asarray((d_c.reshape(-1)[None, :] == np.arange(nrel)[:, None]).astype(np.float32))
    t = jnp.dot(rpb.astype(F32).reshape(-1, nrel), onehot, precision=lax.Precision.HIGHEST)
    t = t.reshape(NA_HEADS, 2 * NA_WIN_R - 1, GRID_W, GRID_W)
    t = jnp.where(jnp.asarray(valid)[None, None], t, NEG)
    t = t.transpose(1, 0, 2, 3).reshape(2 * NA_WIN_R - 1, NA_HEADS * GRID_W, GRID_W)
    return jnp.concatenate([t[:-1], t[1:]], axis=-1)


def _s5_matrices(lam_re, lam_im, log_dt, b_re, b_im, c_re, c_im):
    c = S5_CHUNK
    cat = lambda parts: jnp.concatenate(parts, axis=-1)
    pw, bre, bim, cre, cim = [], [], [], [], []
    for d in range(2):
        dt = jnp.exp(log_dt[d].astype(F32))[:, None]
        lr, li = lam_re[d].astype(F32), lam_im[d].astype(F32)
        mag = jnp.exp(lr * dt)
        a_re, a_im = mag * jnp.cos(li * dt), mag * jnp.sin(li * dt)
        den = lr * lr + li * li
        f_re = ((a_re - 1.0) * lr + a_im * li) / den
        f_im = (a_im * lr - (a_re - 1.0) * li) / den
        br, bi = b_re[d].astype(F32), b_im[d].astype(F32)
        bre.append((f_re[..., None] * br - f_im[..., None] * bi).transpose(0, 2, 1))
        bim.append((f_re[..., None] * bi + f_im[..., None] * br).transpose(0, 2, 1))
        cre.append(c_re[d].astype(F32))
        cim.append(c_im[d].astype(F32))
        p_re, p_im = [jnp.ones_like(a_re)], [jnp.zeros_like(a_re)]
        for _ in range(c):
            p_re, p_im = (p_re + [p_re[-1] * a_re - p_im[-1] * a_im], p_im + [p_re[-1] * a_im + p_im[-1] * a_re])
        p_re, p_im = jnp.stack(p_re, axis=1), jnp.stack(p_im, axis=1)
        pw.append((p_re, p_im))
    (pfr, pfi), (pbr, pbi) = pw
    up = lambda x, off=0: x[:, off:off + c]
    dn = lambda x, off=0: x[:, off:off + c][:, ::-1]
    outer = lambda x, y: (x[:, :, None, :] * y[:, None, :, :])
    e_fr, e_fi, e_br, e_bi = dn(pfr), dn(pfi), up(pbr), up(pbi)
    bb = cat([bre[0], bim[0], bre[1], bim[1]])
    ws = (outer(cat([e_fr, e_fr, e_br, e_br]), bb)
          + outer(cat([-e_fi, e_fi, -e_bi, e_bi]), cat([bim[0], bre[0], bim[1], bre[1]])))
    e_fr, e_fi, e_br, e_bi = up(pfr, 1), up(pfi, 1), dn(pbr, 1), dn(pbi, 1)
    c_a, c_b = cat([cre[0], cre[0], cre[1], cre[1]]), cat([cim[0], cim[0], cim[1], cim[1]])
    wct = outer(cat([e_fr, -e_fi, e_br, -e_bi]), c_a) + outer(cat([-e_fi, -e_fr, -e_bi, -e_br]), c_b)
    e_fr, e_fi, e_br, e_bi = up(pfr), up(pfi), dn(pbr), dn(pbi)
    qj = outer(cat([e_fr, -e_fi, e_br, -e_bi]), c_a) + outer(cat([-e_fi, -e_fr, -e_bi, -e_br]), c_b)
    rows = lambda w: w.reshape(S5_GROUPS, S5_ROW, 4 * S5_N).astype(BF16)
    av = []
    for p_re, p_im in ((pfr[:, c], pfi[:, c]), (pbr[:, c], pbi[:, c])):
        av += [cat([p_re, p_re]), cat([-p_im, p_im]), cat([p_im, -p_im])]
    return rows(ws), rows(wct), bb, qj.reshape(S5_GROUPS, S5_ROW, 4 * S5_N), jnp.stack(av, axis=1)


def _layer_weights(l, p):
    w = p["w_in"][l]
    q0 = Z_NA + Z_S5
    gq = lambda h: w[:, q0 + h * DH:q0 + (h + 1) * DH]
    m0 = q0 + 2 * MIX
    w_in = jnp.concatenate(
        [w[:, 0:q0], gq(0), gq(2), gq(1), gq(3), w[:, q0 + MIX:m0],
         w[:, m0 + MLA_QLORA:m0 + MLA_QLORA + MLA_KVLORA], w[:, m0:m0 + MLA_QLORA],
         w[:, m0 + MLA_QLORA + MLA_KVLORA:], jnp.zeros((D_MODEL, Z_ALL - w.shape[1]), w.dtype)],
        axis=1).astype(BF16)
    wo = p["w_out"][l]
    og = lambda h: wo[2 * MIX + h * DH:2 * MIX + (h + 1) * DH]
    w_out = jnp.concatenate([wo[0:2 * MIX], og(0), og(2), og(1), og(3), wo[3 * MIX:]], axis=0).astype(BF16)
    pad_head = lambda g: jnp.tile(jnp.pad(g, (0, LANES - MLA_QK)), MLA_HEADS)[None, :]
    wuq = jnp.pad(p["mla_w_uq"][l].reshape(MLA_QLORA, MLA_HEADS, MLA_QK),
                  ((0, 2 * LANES - MLA_QLORA), (0, 0), (0, LANES - MLA_QK))).reshape(2 * LANES, MLA_HEADS * LANES)
    wukv = p["mla_w_ukv"][l].reshape(MLA_KVLORA, MLA_HEADS, MLA_NOPE + DH)
    wuk = jnp.pad(wukv[:, :, :MLA_NOPE], ((0, 0), (0, 0), (0, LANES - MLA_NOPE))).reshape(MLA_KVLORA, MLA_HEADS * LANES)
    wuv = wukv[:, :, MLA_NOPE:].reshape(MLA_KVLORA, MIX)
    s5_ws, s5_wct, s5_bb, s5_qj, s5_a = _s5_matrices(p["s5_lam_re"][l], p["s5_lam_im"][l], p["s5_log_dt"][l],
                                                     p["s5_b_re"][l], p["s5_b_im"][l], p["s5_c_re"][l], p["s5_c_im"][l])
    return dict(
        norm1_g=p["norm1_g"][l][None, :], norm2_g=p["norm2_g"][l][None, :],
        w_in=w_in, w_out=w_out,
        naq_g=jnp.tile(p["na_qn"][l], NA_HEADS)[None, :], nak_g=jnp.tile(p["na_kn"][l], NA_HEADS)[None, :],
        gqq_g=jnp.tile(p["gq_qn"][l], GQ_HEADS)[None, :], gqk_g=jnp.tile(p["gq_kn"][l], GQ_KV)[None, :],
        qa_g=jnp.pad(p["mla_qa_g"][l], (0, 2 * LANES - MLA_QLORA))[None, :], kva_g=p["mla_kva_g"][l][None, :],
        mqn_g=pad_head(p["mla_qn"][l]), mkn_g=pad_head(p["mla_kn"][l]),
        wuq=wuq.astype(BF16), wuk=wuk.astype(BF16), wuv=wuv.astype(BF16),
        na_bias=_na_bias_tables(p["na_rpb"][l]),
        s5_ws=s5_ws, s5_wct=s5_wct, s5_bb=s5_bb, s5_qj=s5_qj, s5_a=s5_a,
        s5_d=jnp.tile(p["s5_d"][l].reshape(S5_GROUPS, 1, S5_GROUP), (1, 1, S5_CHUNK)),
        w_glu=p["s5_w_glu"][l].astype(BF16), b_glu=p["s5_b_glu"][l][None, :],
    )


def kernel(x_prompt, x_sample, cache_na_k, cache_na_v, state_s5, cache_gqa_k, cache_gqa_v, cache_mla_ckv,
           cache_mla_krope, c, c_ctx, norm1_g, norm2_g, ada_w, ada_b, w_in, na_qn, na_kn, na_rpb, s5_lam_re,
           s5_lam_im, s5_log_dt, s5_b_re, s5_b_im, s5_c_re, s5_c_im, s5_d, s5_w_glu, s5_b_glu, gq_qn, gq_kn,
           mla_qa_g, mla_kva_g, mla_w_uq, mla_w_ukv, mla_qn, mla_kn, w_out, ffn_w_up, ffn_conv_w, ffn_conv_b,
           ffn_w_down):
    p = dict(norm1_g=norm1_g, norm2_g=norm2_g, w_in=w_in, na_qn=na_qn, na_kn=na_kn, na_rpb=na_rpb,
             s5_lam_re=s5_lam_re, s5_lam_im=s5_lam_im, s5_log_dt=s5_log_dt, s5_b_re=s5_b_re, s5_b_im=s5_b_im,
             s5_c_re=s5_c_re, s5_c_im=s5_c_im, s5_d=s5_d, s5_w_glu=s5_w_glu, s5_b_glu=s5_b_glu,
             gq_qn=gq_qn, gq_kn=gq_kn, mla_qa_g=mla_qa_g, mla_kva_g=mla_kva_g, mla_w_uq=mla_w_uq,
             mla_w_ukv=mla_w_ukv, mla_qn=mla_qn, mla_kn=mla_kn, w_out=w_out, ffn_w_up=ffn_w_up,
             ffn_conv_w=ffn_conv_w, ffn_conv_b=ffn_conv_b, ffn_w_down=ffn_w_down)
    nb, nd = BATCH, DEC_BATCH
    cvec = jnp.concatenate([c_ctx[None, :], c, jnp.zeros((8 - 1 - nd, D_MODEL), F32)], axis=0)
    mods_all = _ada_mods(cvec, ada_w, ada_b).reshape(DEPTH, 8, 6, D_MODEL)
    gq_tabs, mla_tabs = _rope_tables()
    fw = dict(w_up=ffn_w_up.astype(BF16), conv_w=ffn_conv_w, conv_b=ffn_conv_b[:, None, :],
              w_down=ffn_w_down.astype(BF16))

    xc = x_prompt.reshape(nb * SEQ, D_MODEL)
    xl = x_sample.reshape(nd * DEC_SEQ, D_MODEL)
    zero_state = jnp.zeros((S5_GROUPS, nb, 4 * S5_N), F32)
    caches = []
    s5_states = []
    for l in range(DEPTH):
        lw = _layer_weights(l, p)
        mods = mods_all[l]
        s5_m = _s5_prep(lw)

        zna, zs5, zgq, zm = _inproj(xc, mods, lw["norm1_g"], lw["w_in"], 0, nb * SEQ)
        (ona, ogq, omla), caches = _ctx_attn(zna, zgq, zm, lw, l, caches)
        ys5a, ys5b, fin = _s5_mixer(zs5, s5_m, lw, zero_state, nb, S5_CTX_BLOCK)
        xc = _outproj(ona, ys5a, ys5b, ogq, omla, xc, mods, lw, 0, nb * SEQ)
        xc = _ffn(xc, mods, lw, fw, l, 0, nb * SEQ, SEQ)
        s5_states.append(fin.reshape(S5_GROUPS, nb, 2, 2, S5_N).transpose(1, 2, 3, 0, 4))

        zna, zs5, zgq, zm = _inproj(xl, mods, lw["norm1_g"], lw["w_in"], 1, DEC_SEQ)
        ona = _na_lat(zna, cache_na_k, cache_na_v, l, lw["na_bias"], lw)
        ogq = _gq_lat(zgq, cache_gqa_k, cache_gqa_v, l, gq_tabs, lw)
        omla = _mla_lat(zm, cache_mla_ckv, cache_mla_krope, l, mla_tabs, lw)
        h0 = state_s5[:, l].astype(F32).transpose(3, 0, 1, 2, 4).reshape(S5_GROUPS, nd, 4 * S5_N)
        ys5a, ys5b, _ = _s5_mixer(zs5, s5_m, lw, h0, nd, nd)
        xl = _outproj(ona, ys5a, ys5b, ogq, omla, xl, mods, lw, 1, DEC_SEQ)
        xl = _ffn(xl, mods, lw, fw, l, 1, DEC_SEQ, DEC_SEQ)

    return (xc.reshape(nb, SEQ, D_MODEL), xl.reshape(nd, DEC_SEQ, D_MODEL), caches[0], caches[1],
            jnp.stack(s5_states, axis=1), caches[2], caches[3], caches[4], caches[5])
```
